```python
import math
import jax, jax.numpy as jnp
from jax import lax
import numpy as np

D_MODEL = 2048
BATCH = 2
SEQ = 8192
DEPTH = 1

GRID_W = 64
CTX_LEN = 256

GLA_HEADS = 4
GLA_DK = 128
GLA_DV = 256
GLA_KEY_WIDTH = GLA_HEADS * GLA_DK
GLA_VAL_WIDTH = GLA_HEADS * GLA_DV
GATE_RANK = 16
GATE_TAU = 16.0
CHUNK = 64

FOURIER_GROUPS = 4
FOURIER_GROUP_DIM = 256
FOURIER_WIDTH = FOURIER_GROUPS * FOURIER_GROUP_DIM

N_BRANCHES = 2
D_FF = 4 * D_MODEL
N_MOD = 6
EPS = 1e-6
POS_TEMP = 10000.0

IN_SPLIT_WIDTHS = (GLA_KEY_WIDTH, GLA_KEY_WIDTH, GLA_VAL_WIDTH, GLA_VAL_WIDTH,
                   GATE_RANK, GATE_RANK, FOURIER_WIDTH, N_BRANCHES * D_MODEL)
IN_WIDTH = (2 * GLA_KEY_WIDTH + 2 * GLA_VAL_WIDTH + 2 * GATE_RANK
            + FOURIER_WIDTH + N_BRANCHES * D_MODEL)

kernel_name = "hybrid_fnet_gla_dit_block"


def rmsnorm(x, g):
    xf = x.astype(jnp.float32)
    y = xf * lax.rsqrt(jnp.mean(jnp.square(xf), axis=-1, keepdims=True) + EPS)
    return (y * g.astype(jnp.float32)).astype(x.dtype)


def modulate(h, shift, scale):
    return h * (1.0 + scale) + shift


def pos_embed_2d(n_tokens, dtype):
    rows = n_tokens // GRID_W
    row = jnp.repeat(jnp.arange(rows, dtype=jnp.float32), GRID_W)
    col = jnp.tile(jnp.arange(GRID_W, dtype=jnp.float32), rows)
    quarter = D_MODEL // 4
    omega = 1.0 / (POS_TEMP ** (jnp.arange(quarter, dtype=jnp.float32) / quarter))
    er = row[:, None] * omega[None, :]
    ec = col[:, None] * omega[None, :]
    return jnp.concatenate([jnp.sin(er), jnp.cos(er), jnp.sin(ec), jnp.cos(ec)], axis=-1).astype(dtype)


def split_proj(proj):
    parts = []
    start = 0
    for w in IN_SPLIT_WIDTHS:
        parts.append(proj[..., start:start + w])
        start += w
    return parts


def gla_inputs(parts, w_lr_f, b_lr_f, w_lr_b, b_lr_b):
    q, k, v, _, lr_f, lr_b, _, _ = parts
    b, n, _ = q.shape
    heads = lambda a, d: a.astype(jnp.float32).reshape(b, n, GLA_HEADS, d)
    q = heads(q, GLA_DK) * (GLA_DK ** -0.5)
    k = heads(k, GLA_DK)
    v = heads(v, GLA_DV)
    la_f = jax.nn.log_sigmoid((lr_f @ w_lr_f + b_lr_f).astype(jnp.float32)) / GATE_TAU
    la_b = jax.nn.log_sigmoid((lr_b @ w_lr_b + b_lr_b).astype(jnp.float32)) / GATE_TAU
    return q, k, v, heads(la_f, GLA_DK), heads(la_b, GLA_DK)


def gla_chunked(q, k, v, log_a, s0):
    b, t, h, dk = q.shape
    dv = v.shape[-1]
    n = t // CHUNK
    chunks = lambda a: a.reshape(b, n, CHUNK, h, a.shape[-1])
    q, k, v, log_a = chunks(q), chunks(k), chunks(v), chunks(log_a)
    cum = jnp.cumsum(log_a, axis=2)
    cum_last = cum[:, :, -1:]
    q_dec = q * jnp.exp(cum)
    k_inv = k * jnp.exp(-cum)
    k_end = k * jnp.exp(cum_last - cum)
    causal_in_scan = jnp.tril(jnp.ones((CHUNK, CHUNK), dtype=bool))
    scores = jnp.einsum('bnthd,bnshd->bnhts', q_dec, k_inv)
    scores = jnp.where(causal_in_scan, scores, 0.0)
    o_intra = jnp.einsum('bnhts,bnshv->bnthv', scores, v)

    def step(s, xs):
        qd, ke, vc, dec = xs
        o = jnp.einsum('bthd,bhdv->bthv', qd, s)
        s = dec[..., None] * s + jnp.einsum('bshd,bshv->bhdv', ke, vc)
        return s, o

    xs = (q_dec.swapaxes(0, 1), k_end.swapaxes(0, 1), v.swapaxes(0, 1),
          jnp.exp(cum_last[:, :, 0]).swapaxes(0, 1))
    s_final, o_inter = lax.scan(step, s0, xs)
    o = o_intra + o_inter.swapaxes(0, 1)
    return o.reshape(b, t, h, dv), s_final


def bidir_gla(q, k, v, la_f, la_b, s_f0, s_b0):
    o_f, s_f = gla_chunked(q, k, v, la_f, s_f0)
    flip = lambda a: jnp.flip(a, axis=1)
    o_b, s_b = gla_chunked(flip(q), flip(k), flip(v), flip(la_b), s_b0)
    return o_f + flip(o_b), s_f, s_b


def fourier_mix(u):
    b, n, _ = u.shape
    ug = u.astype(jnp.float32).reshape(b, n, FOURIER_GROUPS, FOURIER_GROUP_DIM)
    y = jnp.real(jnp.fft.fft2(ug, axes=(1, 3), norm='ortho'))
    return y.reshape(b, n, FOURIER_WIDTH).astype(u.dtype)


def merge_branches(parts, o_gla, gla_norm_g, w_fourier_out, w_gla_out, w_out):
    _, _, _, g_out, _, _, u_f, gates = parts
    b, n, _ = u_f.shape
    o = rmsnorm(o_gla, gla_norm_g).reshape(b, n, GLA_VAL_WIDTH).astype(u_f.dtype)
    y_gla = (o * jax.nn.silu(g_out)) @ w_gla_out
    y_fft = fourier_mix(u_f) @ w_fourier_out
    gate_fft, gate_gla = jnp.split(jax.nn.sigmoid(gates), N_BRANCHES, axis=-1)
    return (gate_fft * y_fft + gate_gla * y_gla) @ w_out


def sq_relu_mlp(h, w_in, w_out):
    return jnp.square(jax.nn.relu(h @ w_in)) @ w_out


def setup_inputs(seed: int = 0) -> dict:
    key = jax.random.key(seed)
    ks = jax.random.split(key, 24)
    nrm = lambda k, shape, s: jax.random.normal(k, shape, jnp.float32) * s
    L = DEPTH
    return {
        "x": nrm(ks[0], (BATCH, SEQ, D_MODEL), 1.0),
        "c": nrm(ks[1], (BATCH, D_MODEL), 1.0),
        "ctx": nrm(ks[2], (BATCH, CTX_LEN, D_MODEL), 1.0),
        "c_ctx": nrm(ks[3], (D_MODEL,), 1.0),
        "w_mod": nrm(ks[4], (L, D_MODEL, N_MOD * D_MODEL), 0.5 * D_MODEL ** -0.5),
        "b_mod": nrm(ks[5], (L, N_MOD * D_MODEL), 0.02),
        "norm1_g": 1.0 + nrm(ks[6], (L, D_MODEL), 0.05),
        "norm2_g": 1.0 + nrm(ks[7], (L, D_MODEL), 0.05),
        "w_in": nrm(ks[8], (L, D_MODEL, IN_WIDTH), D_MODEL ** -0.5),
        "w_lr_f": nrm(ks[9], (L, GATE_RANK, GLA_KEY_WIDTH), GATE_RANK ** -0.5),
        "b_lr_f": nrm(ks[10], (L, GLA_KEY_WIDTH), 0.1),
        "w_lr_b": nrm(ks[11], (L, GATE_RANK, GLA_KEY_WIDTH), GATE_RANK ** -0.5),
        "b_lr_b": nrm(ks[12], (L, GLA_KEY_WIDTH), 0.1),
        "gla_norm_g": 1.0 + nrm(ks[13], (L, GLA_DV), 0.05),
        "w_fourier_out": nrm(ks[14], (L, FOURIER_WIDTH, D_MODEL), FOURIER_WIDTH ** -0.5),
        "w_gla_out": nrm(ks[15], (L, GLA_VAL_WIDTH, D_MODEL), GLA_VAL_WIDTH ** -0.5),
        "w_out": nrm(ks[16], (L, D_MODEL, D_MODEL), D_MODEL ** -0.5),
        "w_mlp_in": nrm(ks[17], (L, D_MODEL, D_FF), D_MODEL ** -0.5),
        "w_mlp_out": nrm(ks[18], (L, D_FF, D_MODEL), D_FF ** -0.5),
        "final_norm_g": 1.0 + nrm(ks[19], (D_MODEL,), 0.05),
    }


def reference(x, c, ctx, c_ctx, w_mod, b_mod, norm1_g, norm2_g, w_in, w_lr_f, b_lr_f,
              w_lr_b, b_lr_b, gla_norm_g, w_fourier_out, w_gla_out, w_out, w_mlp_in,
              w_mlp_out, final_norm_g):
    b = x.shape[0]
    x = x + pos_embed_2d(x.shape[1], x.dtype)[None]
    x_ctx = ctx
    for i in range(DEPTH):
        last = i == DEPTH - 1
        mod_lat = jax.nn.silu(c) @ w_mod[i] + b_mod[i]
        mod_ctx = (jax.nn.silu(c_ctx) @ w_mod[i] + b_mod[i])[None]
        sh1, sc1, gt1, sh2, sc2, gt2 = [m[:, None] for m in jnp.split(mod_lat, N_MOD, axis=-1)]
        csh1, csc1, cgt1, csh2, csc2, cgt2 = [m[:, None] for m in jnp.split(mod_ctx, N_MOD, axis=-1)]

        h_ctx = modulate(rmsnorm(x_ctx, norm1_g[i]), csh1, csc1)
        parts_ctx = split_proj(h_ctx @ w_in[i])
        q_c, k_c, v_c, laf_c, lab_c = gla_inputs(parts_ctx, w_lr_f[i], b_lr_f[i], w_lr_b[i], b_lr_b[i])
        s_zero = jnp.zeros((b, GLA_HEADS, GLA_DK, GLA_DV), jnp.float32)
        o_ctx, s_f_ctx, s_b_ctx = bidir_gla(q_c, k_c, v_c, laf_c, lab_c, s_zero, s_zero)

        h = modulate(rmsnorm(x, norm1_g[i]), sh1, sc1)
        parts = split_proj(h @ w_in[i])
        q, k, v, la_f, la_b = gla_inputs(parts, w_lr_f[i], b_lr_f[i], w_lr_b[i], b_lr_b[i])
        o_lat, _, _ = bidir_gla(q, k, v, la_f, la_b, s_f_ctx, s_b_ctx)
        y = merge_branches(parts, o_lat, gla_norm_g[i], w_fourier_out[i], w_gla_out[i], w_out[i])
        x = x + gt1 * y

        h2 = modulate(rmsnorm(x, norm2_g[i]), sh2, sc2)
        x = x + gt2 * sq_relu_mlp(h2, w_mlp_in[i], w_mlp_out[i])

        if not last:
            y_ctx = merge_branches(parts_ctx, o_ctx, gla_norm_g[i], w_fourier_out[i], w_gla_out[i], w_out[i])
            x_ctx = x_ctx + cgt1 * y_ctx
            h2_ctx = modulate(rmsnorm(x_ctx, norm2_g[i]), csh2, csc2)
            x_ctx = x_ctx + cgt2 * sq_relu_mlp(h2_ctx, w_mlp_in[i], w_mlp_out[i])

    return rmsnorm(x, final_norm_g)
```

```python
import functools
import math

import jax
import jax.numpy as jnp
import numpy as np
from jax import lax
from jax.experimental import pallas as pl
from jax.experimental.pallas import tpu as pltpu

F32 = jnp.float32
BF16 = jnp.bfloat16

D_MODEL = 2048
GRID_W = 64
HEADS = 4
DK = 128
DV = 256
KEY_W = HEADS * DK
VAL_W = HEADS * DV
RANK = 16
TAU = 16.0
FGROUPS = 4
FGDIM = 256
FWIDTH = FGROUPS * FGDIM
D_FF = 4 * D_MODEL
N_MOD = 6
EPS = 1e-6
POS_TEMP = 10000.0

P_WIDTH = 2 * KEY_W + 2 * VAL_W + FWIDTH + 2 * D_MODEL
LR_PAD = 128
GLA_CHUNK = 128
RADIX = 8

V7X_VMEM_LIMIT = 56 * 1024 * 1024


def _cparams(sem, vmem=V7X_VMEM_LIMIT):
    return pltpu.CompilerParams(dimension_semantics=sem, vmem_limit_bytes=vmem)


def _silu(x):
    return x * jax.nn.sigmoid(x)


def _mod_kernel(c_ref, w_ref, b_ref, o_ref):
    s = _silu(c_ref[...])
    o_ref[...] = jnp.dot(s, w_ref[...], preferred_element_type=F32,
                         precision=lax.Precision.HIGHEST) + b_ref[...]


def _mod(cpad, w_mod, b_mod):
    rows, d = cpad.shape
    n = w_mod.shape[1]
    tn = 1024
    return pl.pallas_call(
        _mod_kernel,
        out_shape=jax.ShapeDtypeStruct((rows, n), F32),
        grid=(n // tn,),
        in_specs=[pl.BlockSpec((rows, d), lambda j: (0, 0)),
                  pl.BlockSpec((d, tn), lambda j: (0, j)),
                  pl.BlockSpec((1, tn), lambda j: (0, j))],
        out_specs=pl.BlockSpec((rows, tn), lambda j: (0, j)),
        compiler_params=_cparams(("arbitrary",)),
        name="mod",
    )(cpad, w_mod, b_mod)


def _add_pos(x, rt_ref, ct_ref):
    tm, d = x.shape
    x3 = x.reshape(tm // GRID_W, GRID_W, d)
    half = d // 2
    lo = x3[:, :, :half] + rt_ref[...]
    hi = x3[:, :, half:] + ct_ref[...][None]
    return jnp.concatenate([lo, hi], axis=-1).reshape(tm, d)


def _rms(x, g):
    return x * lax.rsqrt(jnp.mean(x * x, axis=-1, keepdims=True) + EPS) * g


def _inproj_kernel(*refs, add_pos):
    if add_pos:
        x_ref, rt_ref, ct_ref, sh_ref, sc_ref, g_ref, w_ref, wlr_ref, p_ref, lr_ref, h_scr = refs
    else:
        x_ref, sh_ref, sc_ref, g_ref, w_ref, wlr_ref, p_ref, lr_ref, h_scr = refs

    @pl.when(pl.program_id(1) == 0)
    def _():
        x = x_ref[...]
        if add_pos:
            x = _add_pos(x, rt_ref, ct_ref)
        h = _rms(x, g_ref[...]) * (1.0 + sc_ref[...]) + sh_ref[...]
        hb = h.astype(BF16)
        h_scr[...] = hb
        lr_ref[...] = jnp.dot(hb, wlr_ref[...], preferred_element_type=F32)

    p_ref[...] = jnp.dot(h_scr[...], w_ref[...], preferred_element_type=F32).astype(BF16)


def _inproj(x2, mod3, mod_row_of_tile, norm_g, w_main, w_lr, pos_tabs, tm):
    ntok, d = x2.shape
    tn = 1024
    add_pos = pos_tabs is not None
    in_specs = [pl.BlockSpec((tm, d), lambda i, j: (i, 0))]
    args = [x2]
    if add_pos:
        rt3, ct, tiles_per_seq = pos_tabs
        rpt = tm // GRID_W
        in_specs += [pl.BlockSpec((rpt, 1, d // 2), lambda i, j: (i % tiles_per_seq, 0, 0)),
                     pl.BlockSpec((GRID_W, d // 2), lambda i, j: (0, 0))]
        args += [rt3, ct]
    in_specs += [pl.BlockSpec((None, 1, d), lambda i, j: (mod_row_of_tile(i), 0, 0)),
                 pl.BlockSpec((None, 1, d), lambda i, j: (mod_row_of_tile(i), 0, 1)),
                 pl.BlockSpec((1, d), lambda i, j: (0, 0)),
                 pl.BlockSpec((d, tn), lambda i, j: (0, j)),
                 pl.BlockSpec((d, LR_PAD), lambda i, j: (0, 0))]
    args += [mod3, mod3, norm_g, w_main, w_lr]
    return pl.pallas_call(
        functools.partial(_inproj_kernel, add_pos=add_pos),
        out_shape=(jax.ShapeDtypeStruct((ntok, P_WIDTH), BF16),
                   jax.ShapeDtypeStruct((ntok, LR_PAD), F32)),
        grid=(ntok // tm, P_WIDTH // tn),
        in_specs=in_specs,
        out_specs=(pl.BlockSpec((tm, tn), lambda i, j: (i, j)),
                   pl.BlockSpec((tm, LR_PAD), lambda i, j: (i, 0))),
        scratch_shapes=[pltpu.VMEM((tm, d), BF16)],
        compiler_params=_cparams(("arbitrary", "arbitrary")),
        name="inproj_pos" if add_pos else "inproj_ctx",
    )(*args)


def _log_sigmoid(z):
    return jnp.minimum(z, 0.0) - jnp.log1p(jnp.exp(-jnp.abs(z)))


def _gla_direction(q_ref, k_ref, v_ref, lr_ref, wlr_ref, blr_ref, s_scr, o_ref, backward):
    c = q_ref.shape[0]
    row = lax.broadcasted_iota(jnp.int32, (c, c), 0)
    col = lax.broadcasted_iota(jnp.int32, (c, c), 1)
    keep = (row <= col) if backward else (row >= col)
    tri = jnp.where(keep, 1.0, 0.0).astype(BF16)
    last = 0 if backward else c - 1
    mid = c // 2 if backward else c // 2 - 1

    z = jnp.dot(lr_ref[...].astype(BF16), wlr_ref[...], preferred_element_type=F32) + blr_ref[...]
    la = _log_sigmoid(z) * (1.0 / TAU)
    la_hi = la.astype(BF16)
    la_lo = (la - la_hi.astype(F32)).astype(BF16)
    cum2 = jnp.dot(tri, jnp.concatenate([la_hi, la_lo], axis=1), preferred_element_type=F32)
    cum_all = cum2[:, :KEY_W] + cum2[:, KEY_W:]

    for h in range(HEADS):
        cum = cum_all[:, h * DK:(h + 1) * DK]
        tot = cum[last:last + 1, :]
        ref_pt = cum[mid:mid + 1, :]
        q = q_ref[:, h * DK:(h + 1) * DK].astype(F32) * (DK ** -0.5)
        k = k_ref[:, h * DK:(h + 1) * DK].astype(F32)
        v = v_ref[:, h * DV:(h + 1) * DV]
        q_mid = (q * jnp.exp(cum - ref_pt)).astype(BF16)
        k_mid = (k * jnp.exp(ref_pt - cum)).astype(BF16)
        q_dec = (q * jnp.exp(cum)).astype(BF16)
        k_end = k * jnp.exp(tot - cum)
        scores = lax.dot_general(q_mid, k_mid, (((1,), (1,)), ((), ())), preferred_element_type=F32)
        scores = jnp.where(keep, scores, 0.0).astype(BF16)
        s_prev = s_scr[h]
        lhs = jnp.concatenate([scores, q_dec], axis=1)
        rhs = jnp.concatenate([v, s_prev.astype(BF16)], axis=0)
        o = jnp.dot(lhs, rhs, preferred_element_type=F32)
        if o_ref is not None:
            o_ref[:, h * DV:(h + 1) * DV] = o.astype(o_ref.dtype)
        kv = jnp.dot(k_end.T.astype(BF16), v, preferred_element_type=F32)
        dec_col = jnp.broadcast_to(jnp.exp(tot), (DK, DK)).T
        dec = jnp.concatenate([dec_col] * (DV // DK), axis=1)
        s_scr[h] = s_prev * dec + kv


def _gla_kernel(*refs, emit_o):
    (qf, kf, vf, lrf, qb, kb, vb, lrb, wf, bf, wb, bb, s0f, s0b) = refs[:14]
    if emit_o:
        of, ob, sf_out, sb_out, s_scr = refs[14:]
    else:
        sf_out, sb_out, s_scr = refs[14:]
        of = ob = None
    i = pl.program_id(1)

    @pl.when(i == 0)
    def _():
        s_scr[0] = s0f[...]
        s_scr[1] = s0b[...]

    _gla_direction(qf, kf, vf, lrf, wf, bf, s_scr.at[0], of, backward=False)
    _gla_direction(qb, kb, vb, lrb, wb, bb, s_scr.at[1], ob, backward=True)

    @pl.when(i == pl.num_programs(1) - 1)
    def _():
        sf_out[...] = s_scr[0]
        sb_out[...] = s_scr[1]


def _gla(p3, lr3, wlr_f, blr_f, wlr_b, blr_b, s0f, s0b, emit_o):
    b, t, _ = p3.shape
    c = GLA_CHUNK
    n = t // c
    fwd = lambda blk: (lambda bi, i: (bi, i, blk))
    bwd = lambda blk: (lambda bi, i: (bi, n - 1 - i, blk))

    def seq_specs(mk):
        return [pl.BlockSpec((None, c, KEY_W), mk(0)),
                pl.BlockSpec((None, c, KEY_W), mk(1)),
                pl.BlockSpec((None, c, VAL_W), mk(1)),
                pl.BlockSpec((None, c, LR_PAD), mk(0))]
    full2 = lambda shape: pl.BlockSpec(shape, lambda bi, i: (0, 0))
    st_spec = pl.BlockSpec((None, HEADS, DK, DV), lambda bi, i: (bi, 0, 0, 0))
    in_specs = (seq_specs(fwd) + seq_specs(bwd)
                + [full2(wlr_f.shape), full2(blr_f.shape), full2(wlr_b.shape), full2(blr_b.shape),
                   st_spec, st_spec])
    st_shape = jax.ShapeDtypeStruct((b, HEADS, DK, DV), F32)
    out_shape = [st_shape, st_shape]
    out_specs = [st_spec, st_spec]
    if emit_o:
        o_shape = jax.ShapeDtypeStruct((b, t, VAL_W), BF16)
        out_shape = [o_shape, o_shape] + out_shape
        out_specs = [pl.BlockSpec((None, c, VAL_W), fwd(0)), pl.BlockSpec((None, c, VAL_W), bwd(0))] + out_specs
    return pl.pallas_call(
        functools.partial(_gla_kernel, emit_o=emit_o),
        out_shape=tuple(out_shape),
        grid=(b, n),
        in_specs=in_specs,
        out_specs=tuple(out_specs),
        scratch_shapes=[pltpu.VMEM((2, HEADS, DK, DV), F32)],
        compiler_params=_cparams(("arbitrary", "arbitrary")),
        name="gla_seq" if emit_o else "gla_ctx",
    )(p3, p3, p3, lr3, p3, p3, p3, lr3, wlr_f, blr_f, wlr_b, blr_b, s0f, s0b)


def _cadd(a, b):
    return a[0] + b[0], a[1] + b[1]


def _csub(a, b):
    return a[0] - b[0], a[1] - b[1]


def _cmul_neg_i(a):
    return a[1], -a[0]


def _dft4(y):
    t0, t1 = _cadd(y[0], y[2]), _csub(y[0], y[2])
    t2, t3 = _cadd(y[1], y[3]), _cmul_neg_i(_csub(y[1], y[3]))
    return [_cadd(t0, t2), _cadd(t1, t3), _csub(t0, t2), _csub(t1, t3)]


def _dft8(z):
    r = math.sqrt(0.5)
    s = [_cadd(z[a], z[a + 4]) for a in range(4)]
    d = [_csub(z[a], z[a + 4]) for a in range(4)]
    d1 = ((d[1][0] + d[1][1]) * r, (d[1][1] - d[1][0]) * r)
    d2 = _cmul_neg_i(d[2])
    d3 = ((d[3][1] - d[3][0]) * r, (-d[3][1] - d[3][0]) * r)
    ev = _dft4(s)
    od = _dft4([d[0], d1, d2, d3])
    out = [None] * 8
    for j in range(4):
        out[2 * j] = ev[j]
        out[2 * j + 1] = od[j]
    return out


def _fft_kernel(u_ref, f_ref, twc_ref, tws_ref, xr_ref, xi_ref):
    t = f_ref.shape[0] // 2
    f = f_ref[...]
    z = []
    for a in range(RADIX):
        za = jnp.dot(f, u_ref[a], preferred_element_type=F32)
        zr, zi = za[:t], za[t:]
        cc, ss = twc_ref[a], tws_ref[a]
        z.append((zr * cc + zi * ss, zi * cc - zr * ss))
    x = _dft8(z)
    for k1 in range(RADIX):
        xr_ref[k1] = x[k1][0].astype(xr_ref.dtype)
        xi_ref[k1] = x[k1][1].astype(xi_ref.dtype)


def _fft(u4, fmat, twc, tws):
    b, _, m, w = u4.shape
    nt, t2, _ = fmat.shape
    t = t2 // 2
    chb = 256
    out = jax.ShapeDtypeStruct((b, RADIX, m, w), BF16)
    o_spec = pl.BlockSpec((None, RADIX, t, chb), lambda bi, cj, kt: (bi, 0, kt, cj))
    return pl.pallas_call(
        _fft_kernel,
        out_shape=(out, out),
        grid=(b, w // chb, nt),
        in_specs=[pl.BlockSpec((None, RADIX, m, chb), lambda bi, cj, kt: (bi, 0, 0, cj)),
                  pl.BlockSpec((None, t2, m), lambda bi, cj, kt: (kt, 0, 0)),
                  pl.BlockSpec((RADIX, t, 1), lambda bi, cj, kt: (0, kt, 0)),
                  pl.BlockSpec((RADIX, t, 1), lambda bi, cj, kt: (0, kt, 0))],
        out_specs=(o_spec, o_spec),
        compiler_params=_cparams(("arbitrary", "arbitrary", "arbitrary")),
        name="fft",
    )(u4, fmat, twc, tws)


def _merge_kernel(of_ref, ob_ref, g_ref, gates_ref, xr_ref, xi_ref, x_ref, rt_ref, ct_ref, gt_ref,
                  gng_ref, wgo_ref, wfo_ref, wo_ref, csg_ref, o_ref):
    o = of_ref[...].astype(F32) + ob_ref[...].astype(F32)
    g = g_ref[...].astype(F32)
    gng = gng_ref[...]
    heads = []
    for h in range(HEADS):
        sl = slice(h * DV, (h + 1) * DV)
        heads.append((_rms(o[:, sl], gng) * _silu(g[:, sl])).astype(BF16))
    y_gla = jnp.dot(jnp.concatenate(heads, axis=1), wgo_ref[...], preferred_element_type=F32)

    csg = csg_ref[...]
    groups = []
    for gi in range(FGROUPS):
        sl = slice(gi * FGDIM, (gi + 1) * FGDIM)
        xg = jnp.concatenate([xr_ref[:, sl], xi_ref[:, sl]], axis=1)
        groups.append(jnp.dot(xg, csg, preferred_element_type=F32).astype(BF16))
    y_fft = jnp.dot(jnp.concatenate(groups, axis=1), wfo_ref[...], preferred_element_type=F32)

    gates = jax.nn.sigmoid(gates_ref[...].astype(F32))
    d = y_fft.shape[1]
    zmix = (gates[:, :d] * y_fft + gates[:, d:] * y_gla).astype(BF16)
    y = jnp.dot(zmix, wo_ref[...], preferred_element_type=F32)
    o_ref[...] = _add_pos(x_ref[...], rt_ref, ct_ref) + gt_ref[...] * y


def _const_spec(shape, nidx):
    zeros = (0,) * len(shape)
    return pl.BlockSpec(shape, lambda *idx: zeros, pipeline_mode=pl.Buffered(1))


def _merge(o_f, o_b, p2, xr, xi, x2, rt3, ct, mod3, gng, wgo, wfo, wo, csg, t):
    ntok, d = x2.shape
    tm = 256
    tiles_per_seq = t // tm
    rpt = tm // GRID_W
    row = lambda blk: (lambda i: (i, blk))
    in_specs = [pl.BlockSpec((tm, VAL_W), row(0)),
                pl.BlockSpec((tm, VAL_W), row(0)),
                pl.BlockSpec((tm, VAL_W), row(2)),
                pl.BlockSpec((tm, 2 * d), row(1)),
                pl.BlockSpec((tm, FWIDTH), row(0)),
                pl.BlockSpec((tm, FWIDTH), row(0)),
                pl.BlockSpec((tm, d), row(0)),
                pl.BlockSpec((rpt, 1, d // 2), lambda i: (i % tiles_per_seq, 0, 0)),
                _const_spec((GRID_W, d // 2), 1),
                pl.BlockSpec((None, 1, d), lambda i: (i // tiles_per_seq, 0, 2)),
                _const_spec(gng.shape, 1), _const_spec(wgo.shape, 1), _const_spec(wfo.shape, 1),
                _const_spec(wo.shape, 1), _const_spec(csg.shape, 1)]
    return pl.pallas_call(
        _merge_kernel,
        out_shape=jax.ShapeDtypeStruct((ntok, d), F32),
        grid=(ntok // tm,),
        in_specs=in_specs,
        out_specs=pl.BlockSpec((tm, d), row(0)),
        compiler_params=_cparams(("arbitrary",)),
        name="merge",
    )(o_f, o_b, p2, p2, xr, xi, x2, rt3, ct, mod3, gng, wgo, wfo, wo, csg)


def _mlp_kernel(x_ref, sh_ref, sc_ref, gt_ref, g2_ref, w1_ref, w2_ref, fg_ref, o_ref, h_scr, acc_scr):
    j = pl.program_id(1)

    @pl.when(j == 0)
    def _():
        h = _rms(x_ref[...], g2_ref[...]) * (1.0 + sc_ref[...]) + sh_ref[...]
        h_scr[...] = h.astype(BF16)
        acc_scr[...] = jnp.zeros_like(acc_scr)

    hid = jnp.dot(h_scr[...], w1_ref[...], preferred_element_type=F32)
    hid = jnp.square(jnp.maximum(hid, 0.0)).astype(BF16)
    acc_scr[...] += jnp.dot(hid, w2_ref[...], preferred_element_type=F32)

    @pl.when(j == pl.num_programs(1) - 1)
    def _():
        xo = x_ref[...] + gt_ref[...] * acc_scr[...]
        o_ref[...] = _rms(xo, fg_ref[...])


def _mlp(x2, mod3, g2, w1, w2, fg, t):
    ntok, d = x2.shape
    dff = w1.shape[1]
    tm, tf = 512, 1024
    tiles_per_seq = t // tm
    modspec = lambda blk: pl.BlockSpec((None, 1, d), lambda i, j: (i // tiles_per_seq, 0, blk))
    return pl.pallas_call(
        _mlp_kernel,
        out_shape=jax.ShapeDtypeStruct((ntok, d), F32),
        grid=(ntok // tm, dff // tf),
        in_specs=[pl.BlockSpec((tm, d), lambda i, j: (i, 0)),
                  modspec(3), modspec(4), modspec(5),
                  pl.BlockSpec((1, d), lambda i, j: (0, 0)),
                  pl.BlockSpec((d, tf), lambda i, j: (0, j)),
                  pl.BlockSpec((tf, d), lambda i, j: (j, 0)),
                  pl.BlockSpec((1, d), lambda i, j: (0, 0))],
        out_specs=pl.BlockSpec((tm, d), lambda i, j: (i, 0)),
        scratch_shapes=[pltpu.VMEM((tm, d), BF16), pltpu.VMEM((tm, d), F32)],
        compiler_params=_cparams(("arbitrary", "arbitrary")),
        name="mlp",
    )(x2, mod3, mod3, mod3, g2, w1, w2, fg)


def _pos_tables(t, d):
    quarter = d // 4
    omega = 1.0 / (POS_TEMP ** (jnp.arange(quarter, dtype=F32) / quarter))
    er = jnp.arange(t // GRID_W, dtype=F32)[:, None] * omega[None, :]
    ec = jnp.arange(GRID_W, dtype=F32)[:, None] * omega[None, :]
    rt = jnp.concatenate([jnp.sin(er), jnp.cos(er)], axis=-1)
    ct = jnp.concatenate([jnp.sin(ec), jnp.cos(ec)], axis=-1)
    return rt[:, None, :], ct


def _dft_tables(t):
    m = t // RADIX
    tile = min(256, m)
    k = np.arange(m)
    ang = 2.0 * np.pi * ((k[:, None] * k[None, :]) % m) / m
    cos_t = np.cos(ang).reshape(m // tile, tile, m)
    sin_t = np.sin(ang).reshape(m // tile, tile, m)
    fmat = np.concatenate([cos_t, -sin_t], axis=1)
    a = np.arange(RADIX)
    tw = 2.0 * np.pi * (a[:, None] * k[None, :]) / t
    scale = 1.0 / math.sqrt(t * FGDIM)
    twc = (np.cos(tw) * scale)[:, :, None]
    tws = (np.sin(tw) * scale)[:, :, None]
    c = np.arange(FGDIM)
    cang = 2.0 * np.pi * ((c[:, None] * c[None, :]) % FGDIM) / FGDIM
    csg = np.concatenate([np.cos(cang), np.sin(cang)], axis=0)
    as_f32 = lambda a: jnp.asarray(a.astype(np.float32))
    return as_f32(fmat).astype(BF16), as_f32(twc), as_f32(tws), as_f32(csg).astype(BF16)


def _pad_lr_weight(w_lr, row0):
    out = jnp.zeros((LR_PAD, KEY_W), F32)
    return out.at[row0:row0 + RANK].set(w_lr).astype(BF16)


def kernel(x, c, ctx, c_ctx, w_mod, b_mod, norm1_g, norm2_g, w_in, w_lr_f, b_lr_f, w_lr_b, b_lr_b,
           gla_norm_g, w_fourier_out, w_gla_out, w_out, w_mlp_in, w_mlp_out, final_norm_g):
    b, t, d = x.shape
    tc = ctx.shape[1]
    depth = w_mod.shape[0]
    assert depth == 1 and d == D_MODEL and t % (RADIX * GRID_W) == 0 and tc % GLA_CHUNK == 0
    li = 0

    qkvg = 2 * KEY_W + 2 * VAL_W
    lr0 = qkvg
    u0 = lr0 + 2 * RANK
    wi = w_in[li]
    w_main = jnp.concatenate([wi[:, :qkvg], wi[:, u0:]], axis=1).astype(BF16)
    w_lr = jnp.pad(wi[:, lr0:u0], ((0, 0), (0, LR_PAD - 2 * RANK))).astype(BF16)
    wlr_f, wlr_b = _pad_lr_weight(w_lr_f[li], 0), _pad_lr_weight(w_lr_b[li], RANK)
    blr_f, blr_b = b_lr_f[li][None, :], b_lr_b[li][None, :]
    wgo, wfo, wo = w_gla_out[li].astype(BF16), w_fourier_out[li].astype(BF16), w_out[li].astype(BF16)
    w1, w2 = w_mlp_in[li].astype(BF16), w_mlp_out[li].astype(BF16)

    rt3, ct = _pos_tables(t, d)
    fmat, twc, tws, csg = _dft_tables(t)

    rows = 8
    cpad = jnp.concatenate([c, c_ctx[None, :], jnp.zeros((rows - b - 1, d), F32)], axis=0)
    mod3 = _mod(cpad, w_mod[li], b_mod[li][None, :]).reshape(rows, 1, N_MOD * d)

    tm_ctx = min(512, b * tc)
    p_ctx, lr_ctx = _inproj(ctx.reshape(b * tc, d), mod3, lambda i: b, norm1_g[li][None, :],
                            w_main, w_lr, None, tm_ctx)
    s_zero = jnp.zeros((b, HEADS, DK, DV), F32)
    s_f, s_b = _gla(p_ctx.reshape(b, tc, P_WIDTH), lr_ctx.reshape(b, tc, LR_PAD),
                    wlr_f, blr_f, wlr_b, blr_b, s_zero, s_zero, emit_o=False)

    tm = min(1024, t)
    tiles = t // tm
    x2 = x.reshape(b * t, d)
    p, lr = _inproj(x2, mod3, lambda i: i // tiles, norm1_g[li][None, :], w_main, w_lr,
                    (rt3, ct, tiles), tm)
    p3 = p.reshape(b, t, P_WIDTH)
    o_f, o_b, _, _ = _gla(p3, lr.reshape(b, t, LR_PAD), wlr_f, blr_f, wlr_b, blr_b, s_f, s_b, emit_o=True)

    m = t // RADIX
    u4 = p3[:, :, qkvg:qkvg + FWIDTH].reshape(b, m, RADIX, FWIDTH).transpose(0, 2, 1, 3)
    xr, xi = _fft(u4, fmat, twc, tws)

    x1 = _merge(o_f.reshape(b * t, VAL_W), o_b.reshape(b * t, VAL_W), p,
                xr.reshape(b * t, FWIDTH), xi.reshape(b * t, FWIDTH), x2, rt3, ct, mod3,
                gla_norm_g[li][None, :], wgo, wfo, wo, csg, t)

    out = _mlp(x1, mod3, norm2_g[li][None, :], w1, w2, final_norm_g[None, :], t)
    return out.reshape(b, t, d)
```

```python
import functools
import math

import jax
import jax.numpy as jnp
import numpy as np
from jax import lax
from jax.experimental import pallas as pl
from jax.experimental.pallas import tpu as pltpu

F32 = jnp.float32
BF16 = jnp.bfloat16

D_MODEL = 2048
GRID_W = 64
HEADS = 4
DK = 128
DV = 256
KEY_W = HEADS * DK
VAL_W = HEADS * DV
RANK = 16
TAU = 16.0
FGROUPS = 4
FGDIM = 256
FWIDTH = FGROUPS * FGDIM
D_FF = 4 * D_MODEL
N_MOD = 6
EPS = 1e-6
POS_TEMP = 10000.0

QKVG_W = 2 * KEY_W + 2 * VAL_W
LR_PAD = 128
GLA_CHUNK = 128
RADIX = 8

V7X_VMEM_LIMIT = 56 * 1024 * 1024


def _cparams(sem, vmem=V7X_VMEM_LIMIT):
    return pltpu.CompilerParams(dimension_semantics=sem, vmem_limit_bytes=vmem)


def _silu(x):
    return x * jax.nn.sigmoid(x)


def _mod_kernel(c_ref, w_ref, b_ref, o_ref):
    s = _silu(c_ref[...])
    o_ref[...] = jnp.dot(s, w_ref[...], preferred_element_type=F32,
                         precision=lax.Precision.HIGHEST) + b_ref[...]


def _mod(cpad, w_mod, b_mod):
    rows, d = cpad.shape
    n = w_mod.shape[1]
    tn = 1024
    return pl.pallas_call(
        _mod_kernel,
        out_shape=jax.ShapeDtypeStruct((rows, n), F32),
        grid=(n // tn,),
        in_specs=[pl.BlockSpec((rows, d), lambda j: (0, 0)),
                  pl.BlockSpec((d, tn), lambda j: (0, j)),
                  pl.BlockSpec((1, tn), lambda j: (0, j))],
        out_specs=pl.BlockSpec((rows, tn), lambda j: (0, j)),
        compiler_params=_cparams(("arbitrary",)),
        name="mod",
    )(cpad, w_mod, b_mod)


def _add_pos(x, rt_ref, ct_ref):
    tm, d = x.shape
    x3 = x.reshape(tm // GRID_W, GRID_W, d)
    half = d // 2
    lo = x3[:, :, :half] + rt_ref[...]
    hi = x3[:, :, half:] + ct_ref[...][None]
    return jnp.concatenate([lo, hi], axis=-1).reshape(tm, d)


def _rms(x, g):
    return x * lax.rsqrt(jnp.mean(x * x, axis=-1, keepdims=True) + EPS) * g


def _inproj_kernel(*refs, add_pos, full, n_qkvg_tiles):
    refs = list(refs)
    x_ref = refs.pop(0)
    rt_ref, ct_ref = (refs.pop(0), refs.pop(0)) if add_pos else (None, None)
    sh_ref, sc_ref, g_ref, wa_ref = refs[:4]
    refs = refs[4:]
    wb_ref = refs.pop(0) if full else None
    wlr_ref, p_ref = refs.pop(0), refs.pop(0)
    u_ref, gates_ref = (refs.pop(0), refs.pop(0)) if full else (None, None)
    lr_ref, h_scr = refs
    j = pl.program_id(1)

    @pl.when(j == 0)
    def _():
        x = x_ref[...]
        if add_pos:
            x = _add_pos(x, rt_ref, ct_ref)
        h = _rms(x, g_ref[...]) * (1.0 + sc_ref[...]) + sh_ref[...]
        hb = h.astype(BF16)
        h_scr[...] = hb
        lr_ref[...] = jnp.dot(hb, wlr_ref[...], preferred_element_type=F32)

    def proj(w_ref):
        return jnp.dot(h_scr[...], w_ref[...], preferred_element_type=F32)

    if not full:
        p_ref[...] = proj(wa_ref).astype(BF16)
        return

    @pl.when(j < n_qkvg_tiles)
    def _():
        p_ref[...] = proj(wa_ref).astype(BF16)

    @pl.when(j == n_qkvg_tiles)
    def _():
        u_ref[...] = proj(wb_ref)

    @pl.when(j > n_qkvg_tiles)
    def _():
        gates_ref[...] = proj(wb_ref).astype(BF16)


def _inproj(x2, mod3, mod_row_of_tile, norm_g, wa, wb, w_lr, pos_tabs, tm, full):
    ntok, d = x2.shape
    tn = 1024
    na = QKVG_W // tn
    n_cols = (QKVG_W + FWIDTH + 2 * d) // tn if full else (2 * KEY_W + VAL_W) // tn
    add_pos = pos_tabs is not None
    in_specs = [pl.BlockSpec((tm, d), lambda i, j: (i, 0))]
    args = [x2]
    if add_pos:
        rt3, ct, tiles_per_seq = pos_tabs
        rpt = tm // GRID_W
        in_specs += [pl.BlockSpec((rpt, 1, d // 2), lambda i, j: (i % tiles_per_seq, 0, 0)),
                     pl.BlockSpec((GRID_W, d // 2), lambda i, j: (0, 0))]
        args += [rt3, ct]
    in_specs += [pl.BlockSpec((None, 1, d), lambda i, j: (mod_row_of_tile(i), 0, 0)),
                 pl.BlockSpec((None, 1, d), lambda i, j: (mod_row_of_tile(i), 0, 1)),
                 pl.BlockSpec((1, d), lambda i, j: (0, 0)),
                 pl.BlockSpec((d, tn), lambda i, j: (0, jnp.minimum(j, na - 1)))]
    args += [mod3, mod3, norm_g, wa]
    out_shape = [jax.ShapeDtypeStruct((ntok, min(n_cols, na) * tn), BF16)]
    out_specs = [pl.BlockSpec((tm, tn), lambda i, j: (i, jnp.minimum(j, na - 1)))]
    if full:
        in_specs.append(pl.BlockSpec((d, tn), lambda i, j: (0, jnp.maximum(j - na, 0))))
        args.append(wb)
        out_shape += [jax.ShapeDtypeStruct((ntok, FWIDTH), F32),
                      jax.ShapeDtypeStruct((ntok, 2 * d), BF16)]
        out_specs += [pl.BlockSpec((tm, FWIDTH), lambda i, j: (i, 0)),
                      pl.BlockSpec((tm, tn), lambda i, j: (i, jnp.maximum(j - na - 1, 0)))]
    in_specs.append(pl.BlockSpec((d, LR_PAD), lambda i, j: (0, 0)))
    args.append(w_lr)
    out_shape.append(jax.ShapeDtypeStruct((ntok, LR_PAD), F32))
    out_specs.append(pl.BlockSpec((tm, LR_PAD), lambda i, j: (i, 0)))
    return pl.pallas_call(
        functools.partial(_inproj_kernel, add_pos=add_pos, full=full, n_qkvg_tiles=na),
        out_shape=tuple(out_shape),
        grid=(ntok // tm, n_cols),
        in_specs=in_specs,
        out_specs=tuple(out_specs),
        scratch_shapes=[pltpu.VMEM((tm, d), BF16)],
        compiler_params=_cparams(("arbitrary", "arbitrary")),
        name="inproj_seq" if full else "inproj_ctx",
    )(*args)


def _log_sigmoid(z):
    return jnp.minimum(z, 0.0) - jnp.log1p(jnp.exp(-jnp.abs(z)))


def _gla_direction(q_ref, k_ref, v_ref, lr_ref, wlr_ref, blr_ref, s_scr, o_ref, backward):
    c = q_ref.shape[0]
    row = lax.broadcasted_iota(jnp.int32, (c, c), 0)
    col = lax.broadcasted_iota(jnp.int32, (c, c), 1)
    keep = (row <= col) if backward else (row >= col)
    tri = jnp.where(keep, 1.0, 0.0).astype(BF16)
    last = 0 if backward else c - 1
    mid = c // 2 if backward else c // 2 - 1

    z = jnp.dot(lr_ref[...].astype(BF16), wlr_ref[...], preferred_element_type=F32) + blr_ref[...]
    la = _log_sigmoid(z) * (1.0 / TAU)
    la_hi = la.astype(BF16)
    la_lo = (la - la_hi.astype(F32)).astype(BF16)
    cum2 = jnp.dot(tri, jnp.concatenate([la_hi, la_lo], axis=1), preferred_element_type=F32)
    cum_all = cum2[:, :KEY_W] + cum2[:, KEY_W:]

    for h in range(HEADS):
        cum = cum_all[:, h * DK:(h + 1) * DK]
        tot = cum[last:last + 1, :]
        ref_pt = cum[mid:mid + 1, :]
        q = q_ref[:, h * DK:(h + 1) * DK].astype(F32) * (DK ** -0.5)
        k = k_ref[:, h * DK:(h + 1) * DK].astype(F32)
        v = v_ref[:, h * DV:(h + 1) * DV]
        q_mid = (q * jnp.exp(cum - ref_pt)).astype(BF16)
        k_mid = (k * jnp.exp(ref_pt - cum)).astype(BF16)
        q_dec = (q * jnp.exp(cum)).astype(BF16)
        k_end = k * jnp.exp(tot - cum)
        scores = lax.dot_general(q_mid, k_mid, (((1,), (1,)), ((), ())), preferred_element_type=F32)
        scores = jnp.where(keep, scores, 0.0).astype(BF16)
        s_prev = s_scr[h]
        lhs = jnp.concatenate([scores, q_dec], axis=1)
        rhs = jnp.concatenate([v, s_prev.astype(BF16)], axis=0)
        o = jnp.dot(lhs, rhs, preferred_element_type=F32)
        if o_ref is not None:
            o_ref[:, h * DV:(h + 1) * DV] = o.astype(o_ref.dtype)
        kv = jnp.dot(k_end.T.astype(BF16), v, preferred_element_type=F32)
        dec_col = jnp.broadcast_to(jnp.exp(tot), (DK, DK)).T
        dec = jnp.concatenate([dec_col] * (DV // DK), axis=1)
        s_scr[h] = s_prev * dec + kv


def _gla_kernel(*refs, emit_o):
    (qf, kf, vf, lrf, qb, kb, vb, lrb, wf, bf, wb, bb, s0f, s0b) = refs[:14]
    if emit_o:
        of, ob, sf_out, sb_out, s_scr = refs[14:]
    else:
        sf_out, sb_out, s_scr = refs[14:]
        of = ob = None
    i = pl.program_id(1)

    @pl.when(i == 0)
    def _():
        s_scr[0] = s0f[...]
        s_scr[1] = s0b[...]

    _gla_direction(qf, kf, vf, lrf, wf, bf, s_scr.at[0], of, backward=False)
    _gla_direction(qb, kb, vb, lrb, wb, bb, s_scr.at[1], ob, backward=True)

    @pl.when(i == pl.num_programs(1) - 1)
    def _():
        sf_out[...] = s_scr[0]
        sb_out[...] = s_scr[1]


def _gla(p3, lr3, wlr_f, blr_f, wlr_b, blr_b, s0f, s0b, emit_o):
    b, t, _ = p3.shape
    c = GLA_CHUNK
    n = t // c
    fwd = lambda blk: (lambda bi, i: (bi, i, blk))
    bwd = lambda blk: (lambda bi, i: (bi, n - 1 - i, blk))

    def seq_specs(mk):
        return [pl.BlockSpec((None, c, KEY_W), mk(0)),
                pl.BlockSpec((None, c, KEY_W), mk(1)),
                pl.BlockSpec((None, c, VAL_W), mk(1)),
                pl.BlockSpec((None, c, LR_PAD), mk(0))]
    full2 = lambda shape: pl.BlockSpec(shape, lambda bi, i: (0, 0))
    st_spec = pl.BlockSpec((None, HEADS, DK, DV), lambda bi, i: (bi, 0, 0, 0))
    in_specs = (seq_specs(fwd) + seq_specs(bwd)
                + [full2(wlr_f.shape), full2(blr_f.shape), full2(wlr_b.shape), full2(blr_b.shape),
                   st_spec, st_spec])
    st_shape = jax.ShapeDtypeStruct((b, HEADS, DK, DV), F32)
    out_shape = [st_shape, st_shape]
    out_specs = [st_spec, st_spec]
    if emit_o:
        o_shape = jax.ShapeDtypeStruct((b, t, VAL_W), BF16)
        out_shape = [o_shape, o_shape] + out_shape
        out_specs = [pl.BlockSpec((None, c, VAL_W), fwd(0)), pl.BlockSpec((None, c, VAL_W), bwd(0))] + out_specs
    return pl.pallas_call(
        functools.partial(_gla_kernel, emit_o=emit_o),
        out_shape=tuple(out_shape),
        grid=(b, n),
        in_specs=in_specs,
        out_specs=tuple(out_specs),
        scratch_shapes=[pltpu.VMEM((2, HEADS, DK, DV), F32)],
        compiler_params=_cparams(("arbitrary", "arbitrary")),
        name="gla_seq" if emit_o else "gla_ctx",
    )(p3, p3, p3, lr3, p3, p3, p3, lr3, wlr_f, blr_f, wlr_b, blr_b, s0f, s0b)


def _cadd(a, b):
    return a[0] + b[0], a[1] + b[1]


def _csub(a, b):
    return a[0] - b[0], a[1] - b[1]


def _cmul_neg_i(a):
    return a[1], -a[0]


def _dft4(y):
    t0, t1 = _cadd(y[0], y[2]), _csub(y[0], y[2])
    t2, t3 = _cadd(y[1], y[3]), _cmul_neg_i(_csub(y[1], y[3]))
    return [_cadd(t0, t2), _cadd(t1, t3), _csub(t0, t2), _csub(t1, t3)]


def _dft8(z):
    r = math.sqrt(0.5)
    s = [_cadd(z[a], z[a + 4]) for a in range(4)]
    d = [_csub(z[a], z[a + 4]) for a in range(4)]
    d1 = ((d[1][0] + d[1][1]) * r, (d[1][1] - d[1][0]) * r)
    d2 = _cmul_neg_i(d[2])
    d3 = ((d[3][1] - d[3][0]) * r, (-d[3][1] - d[3][0]) * r)
    ev = _dft4(s)
    od = _dft4([d[0], d1, d2, d3])
    out = [None] * 8
    for j in range(4):
        out[2 * j] = ev[j]
        out[2 * j + 1] = od[j]
    return out


def _fft_kernel(*refs, n_slabs):
    u_refs = refs[:n_slabs]
    f_ref, twc_ref, tws_ref, xr_ref, xi_ref, ub_scr = refs[n_slabs:]
    m = ub_scr.shape[1]

    @pl.when(pl.program_id(2) == 0)
    def _():
        for a in range(RADIX):
            rows = [u[pl.ds(a, m, stride=RADIX), :].astype(BF16) for u in u_refs]
            ub_scr[a] = jnp.concatenate(rows, axis=1)

    t = f_ref.shape[0] // 2
    f = f_ref[...]
    z = []
    for a in range(RADIX):
        za = jnp.dot(f, ub_scr[a], preferred_element_type=F32)
        zr, zi = za[:t], za[t:]
        cc, ss = twc_ref[a], tws_ref[a]
        z.append((zr * cc + zi * ss, zi * cc - zr * ss))
    x = _dft8(z)
    for k1 in range(RADIX):
        xr_ref[k1] = x[k1][0].astype(xr_ref.dtype)
        xi_ref[k1] = x[k1][1].astype(xi_ref.dtype)


LANES = 128


def _fft(u3, fmat, twc, tws):
    b, seq, w = u3.shape
    m = seq // RADIX
    nt, t2, _ = fmat.shape
    t = t2 // 2
    chb = 256
    n_slabs = chb // LANES
    out = jax.ShapeDtypeStruct((b, RADIX, m, w), BF16)
    o_spec = pl.BlockSpec((None, RADIX, t, chb), lambda bi, cj, kt: (bi, 0, kt, cj))
    slab = lambda s: pl.BlockSpec((None, seq, LANES), lambda bi, cj, kt: (bi, 0, cj * n_slabs + s))
    return pl.pallas_call(
        functools.partial(_fft_kernel, n_slabs=n_slabs),
        out_shape=(out, out),
        grid=(b, w // chb, nt),
        scratch_shapes=[pltpu.VMEM((RADIX, m, chb), BF16)],
        in_specs=[slab(s) for s in range(n_slabs)] + [
                  pl.BlockSpec((None, t2, m), lambda bi, cj, kt: (kt, 0, 0)),
                  pl.BlockSpec((RADIX, t, 1), lambda bi, cj, kt: (0, kt, 0)),
                  pl.BlockSpec((RADIX, t, 1), lambda bi, cj, kt: (0, kt, 0))],
        out_specs=(o_spec, o_spec),
        compiler_params=_cparams(("arbitrary", "arbitrary", "arbitrary")),
        name="fft",
    )(*([u3] * n_slabs), fmat, twc, tws)


def _merge_kernel(of_ref, ob_ref, g_ref, gates_ref, xr_ref, xi_ref, x_ref, rt_ref, ct_ref, gt_ref,
                  gng_ref, wgo_ref, wfo_ref, wo_ref, csg_ref, o_ref):
    o = of_ref[...].astype(F32) + ob_ref[...].astype(F32)
    g = g_ref[...].astype(F32)
    gng = gng_ref[...]
    heads = []
    for h in range(HEADS):
        sl = slice(h * DV, (h + 1) * DV)
        heads.append((_rms(o[:, sl], gng) * _silu(g[:, sl])).astype(BF16))
    y_gla = jnp.dot(jnp.concatenate(heads, axis=1), wgo_ref[...], preferred_element_type=F32)

    csg = csg_ref[...]
    groups = []
    for gi in range(FGROUPS):
        sl = slice(gi * FGDIM, (gi + 1) * FGDIM)
        xg = jnp.concatenate([xr_ref[:, sl], xi_ref[:, sl]], axis=1)
        groups.append(jnp.dot(xg, csg, preferred_element_type=F32).astype(BF16))
    y_fft = jnp.dot(jnp.concatenate(groups, axis=1), wfo_ref[...], preferred_element_type=F32)

    gates = jax.nn.sigmoid(gates_ref[...].astype(F32))
    d = y_fft.shape[1]
    zmix = (gates[:, :d] * y_fft + gates[:, d:] * y_gla).astype(BF16)
    y = jnp.dot(zmix, wo_ref[...], preferred_element_type=F32)
    o_ref[...] = _add_pos(x_ref[...], rt_ref, ct_ref) + gt_ref[...] * y


def _const_spec(shape, nidx):
    zeros = (0,) * len(shape)
    return pl.BlockSpec(shape, lambda *idx: zeros, pipeline_mode=pl.Buffered(1))


def _merge(o_f, o_b, p2, gates, xr, xi, x2, rt3, ct, mod3, gng, wgo, wfo, wo, csg, t):
    ntok, d = x2.shape
    tm = 256
    tiles_per_seq = t // tm
    rpt = tm // GRID_W
    row = lambda blk: (lambda i: (i, blk))
    in_specs = [pl.BlockSpec((tm, VAL_W), row(0)),
                pl.BlockSpec((tm, VAL_W), row(0)),
                pl.BlockSpec((tm, VAL_W), row(2)),
                pl.BlockSpec((tm, 2 * d), row(0)),
                pl.BlockSpec((tm, FWIDTH), row(0)),
                pl.BlockSpec((tm, FWIDTH), row(0)),
                pl.BlockSpec((tm, d), row(0)),
                pl.BlockSpec((rpt, 1, d // 2), lambda i: (i % tiles_per_seq, 0, 0)),
                _const_spec((GRID_W, d // 2), 1),
                pl.BlockSpec((None, 1, d), lambda i: (i // tiles_per_seq, 0, 2)),
                _const_spec(gng.shape, 1), _const_spec(wgo.shape, 1), _const_spec(wfo.shape, 1),
                _const_spec(wo.shape, 1), _const_spec(csg.shape, 1)]
    return pl.pallas_call(
        _merge_kernel,
        out_shape=jax.ShapeDtypeStruct((ntok, d), F32),
        grid=(ntok // tm,),
        in_specs=in_specs,
        out_specs=pl.BlockSpec((tm, d), row(0)),
        compiler_params=_cparams(("arbitrary",)),
        name="merge",
    )(o_f, o_b, p2, gates, xr, xi, x2, rt3, ct, mod3, gng, wgo, wfo, wo, csg)


def _mlp_kernel(x_ref, sh_ref, sc_ref, gt_ref, g2_ref, w1_ref, w2_ref, fg_ref, o_ref, h_scr, acc_scr):
    j = pl.program_id(1)

    @pl.when(j == 0)
    def _():
        h = _rms(x_ref[...], g2_ref[...]) * (1.0 + sc_ref[...]) + sh_ref[...]
        h_scr[...] = h.astype(BF16)
        acc_scr[...] = jnp.zeros_like(acc_scr)

    hid = jnp.dot(h_scr[...], w1_ref[...], preferred_element_type=F32)
    hid = jnp.square(jnp.maximum(hid, 0.0)).astype(BF16)
    acc_scr[...] += jnp.dot(hid, w2_ref[...], preferred_element_type=F32)

    @pl.when(j == pl.num_programs(1) - 1)
    def _():
        xo = x_ref[...] + gt_ref[...] * acc_scr[...]
        o_ref[...] = _rms(xo, fg_ref[...])


def _mlp(x2, mod3, g2, w1, w2, fg, t):
    ntok, d = x2.shape
    dff = w1.shape[1]
    tm, tf = 512, 1024
    tiles_per_seq = t // tm
    modspec = lambda blk: pl.BlockSpec((None, 1, d), lambda i, j: (i // tiles_per_seq, 0, blk))
    return pl.pallas_call(
        _mlp_kernel,
        out_shape=jax.ShapeDtypeStruct((ntok, d), F32),
        grid=(ntok // tm, dff // tf),
        in_specs=[pl.BlockSpec((tm, d), lambda i, j: (i, 0)),
                  modspec(3), modspec(4), modspec(5),
                  pl.BlockSpec((1, d), lambda i, j: (0, 0)),
                  pl.BlockSpec((d, tf), lambda i, j: (0, j)),
                  pl.BlockSpec((tf, d), lambda i, j: (j, 0)),
                  pl.BlockSpec((1, d), lambda i, j: (0, 0))],
        out_specs=pl.BlockSpec((tm, d), lambda i, j: (i, 0)),
        scratch_shapes=[pltpu.VMEM((tm, d), BF16), pltpu.VMEM((tm, d), F32)],
        compiler_params=_cparams(("arbitrary", "arbitrary")),
        name="mlp",
    )(x2, mod3, mod3, mod3, g2, w1, w2, fg)


def _pos_tables(t, d):
    quarter = d // 4
    omega = 1.0 / (POS_TEMP ** (jnp.arange(quarter, dtype=F32) / quarter))
    er = jnp.arange(t // GRID_W, dtype=F32)[:, None] * omega[None, :]
    ec = jnp.arange(GRID_W, dtype=F32)[:, None] * omega[None, :]
    rt = jnp.concatenate([jnp.sin(er), jnp.cos(er)], axis=-1)
    ct = jnp.concatenate([jnp.sin(ec), jnp.cos(ec)], axis=-1)
    return rt[:, None, :], ct


def _dft_tables(t):
    m = t // RADIX
    tile = min(256, m)
    k = np.arange(m)
    ang = 2.0 * np.pi * ((k[:, None] * k[None, :]) % m) / m
    cos_t = np.cos(ang).reshape(m // tile, tile, m)
    sin_t = np.sin(ang).reshape(m // tile, tile, m)
    fmat = np.concatenate([cos_t, -sin_t], axis=1)
    a = np.arange(RADIX)
    tw = 2.0 * np.pi * (a[:, None] * k[None, :]) / t
    scale = 1.0 / math.sqrt(t * FGDIM)
    twc = (np.cos(tw) * scale)[:, :, None]
    tws = (np.sin(tw) * scale)[:, :, None]
    c = np.arange(FGDIM)
    cang = 2.0 * np.pi * ((c[:, None] * c[None, :]) % FGDIM) / FGDIM
    csg = np.concatenate([np.cos(cang), np.sin(cang)], axis=0)
    as_f32 = lambda a: jnp.asarray(a.astype(np.float32))
    return as_f32(fmat).astype(BF16), as_f32(twc), as_f32(tws), as_f32(csg).astype(BF16)


def _pad_lr_weight(w_lr, row0):
    out = jnp.zeros((LR_PAD, KEY_W), F32)
    return out.at[row0:row0 + RANK].set(w_lr).astype(BF16)


def kernel(x, c, ctx, c_ctx, w_mod, b_mod, norm1_g, norm2_g, w_in, w_lr_f, b_lr_f, w_lr_b, b_lr_b,
           gla_norm_g, w_fourier_out, w_gla_out, w_out, w_mlp_in, w_mlp_out, final_norm_g):
    b, t, d = x.shape
    tc = ctx.shape[1]
    depth = w_mod.shape[0]
    assert depth == 1 and d == D_MODEL and t % (RADIX * GRID_W) == 0 and tc % GLA_CHUNK == 0
    li = 0

    lr0 = QKVG_W
    u0 = lr0 + 2 * RANK
    wi = w_in[li]
    wa = wi[:, :lr0].astype(BF16)
    wb = wi[:, u0:].astype(BF16)
    w_lr = jnp.pad(wi[:, lr0:u0], ((0, 0), (0, LR_PAD - 2 * RANK))).astype(BF16)
    wlr_f, wlr_b = _pad_lr_weight(w_lr_f[li], 0), _pad_lr_weight(w_lr_b[li], RANK)
    blr_f, blr_b = b_lr_f[li][None, :], b_lr_b[li][None, :]
    wgo, wfo, wo = w_gla_out[li].astype(BF16), w_fourier_out[li].astype(BF16), w_out[li].astype(BF16)
    w1, w2 = w_mlp_in[li].astype(BF16), w_mlp_out[li].astype(BF16)

    rt3, ct = _pos_tables(t, d)
    fmat, twc, tws, csg = _dft_tables(t)

    rows = 8
    cpad = jnp.concatenate([c, c_ctx[None, :], jnp.zeros((rows - b - 1, d), F32)], axis=0)
    mod3 = _mod(cpad, w_mod[li], b_mod[li][None, :]).reshape(rows, 1, N_MOD * d)

    tm_ctx = min(512, b * tc)
    p_ctx, lr_ctx = _inproj(ctx.reshape(b * tc, d), mod3, lambda i: b, norm1_g[li][None, :],
                            wa, None, w_lr, None, tm_ctx, full=False)
    s_zero = jnp.zeros((b, HEADS, DK, DV), F32)
    s_f, s_b = _gla(p_ctx.reshape(b, tc, p_ctx.shape[1]), lr_ctx.reshape(b, tc, LR_PAD),
                    wlr_f, blr_f, wlr_b, blr_b, s_zero, s_zero, emit_o=False)

    tm = min(512, t)
    tiles = t // tm
    x2 = x.reshape(b * t, d)
    p, u, gates, lr = _inproj(x2, mod3, lambda i: i // tiles, norm1_g[li][None, :], wa, wb, w_lr,
                              (rt3, ct, tiles), tm, full=True)
    o_f, o_b, _, _ = _gla(p.reshape(b, t, QKVG_W), lr.reshape(b, t, LR_PAD),
                          wlr_f, blr_f, wlr_b, blr_b, s_f, s_b, emit_o=True)

    xr, xi = _fft(u.reshape(b, t, FWIDTH), fmat, twc, tws)

    x1 = _merge(o_f.reshape(b * t, VAL_W), o_b.reshape(b * t, VAL_W), p, gates,
                xr.reshape(b * t, FWIDTH), xi.reshape(b * t, FWIDTH), x2, rt3, ct, mod3,
                gla_norm_g[li][None, :], wgo, wfo, wo, csg, t)

    out = _mlp(x1, mod3, norm2_g[li][None, :], w1, w2, final_norm_g[None, :], t)
    return out.reshape(b, t, d)
```

```python
import functools
import math

import jax
import jax.numpy as jnp
import numpy as np
from jax import lax
from jax.experimental import pallas as pl
from jax.experimental.pallas import tpu as pltpu

F32 = jnp.float32
BF16 = jnp.bfloat16

D_MODEL = 2048
GRID_W = 64
HEADS = 4
DK = 128
DV = 256
KEY_W = HEADS * DK
VAL_W = HEADS * DV
RANK = 16
TAU = 16.0
FGROUPS = 4
FGDIM = 256
FWIDTH = FGROUPS * FGDIM
D_FF = 4 * D_MODEL
N_MOD = 6
EPS = 1e-6
POS_TEMP = 10000.0

QKVG_W = 2 * KEY_W + 2 * VAL_W
LR_PAD = 128
GLA_CHUNK = 128
RADIX = 8

V7X_VMEM_LIMIT = 56 * 1024 * 1024


def _cparams(sem, vmem=V7X_VMEM_LIMIT):
    return pltpu.CompilerParams(dimension_semantics=sem, vmem_limit_bytes=vmem)


def _silu(x):
    return x * jax.nn.sigmoid(x)


def _mod_kernel(c_ref, w_ref, b_ref, o_ref):
    s = _silu(c_ref[...])
    o_ref[...] = jnp.dot(s, w_ref[...], preferred_element_type=F32,
                         precision=lax.Precision.HIGHEST) + b_ref[...]


def _mod(cpad, w_mod, b_mod):
    rows, d = cpad.shape
    n = w_mod.shape[1]
    tn = 1024
    return pl.pallas_call(
        _mod_kernel,
        out_shape=jax.ShapeDtypeStruct((rows, n), F32),
        grid=(n // tn,),
        in_specs=[pl.BlockSpec((rows, d), lambda j: (0, 0)),
                  pl.BlockSpec((d, tn), lambda j: (0, j)),
                  pl.BlockSpec((1, tn), lambda j: (0, j))],
        out_specs=pl.BlockSpec((rows, tn), lambda j: (0, j)),
        compiler_params=_cparams(("arbitrary",)),
        name="mod",
    )(cpad, w_mod, b_mod)


def _add_pos(x, rt_ref, ct_ref):
    tm, d = x.shape
    x3 = x.reshape(tm // GRID_W, GRID_W, d)
    half = d // 2
    lo = x3[:, :, :half] + rt_ref[...]
    hi = x3[:, :, half:] + ct_ref[...][None]
    return jnp.concatenate([lo, hi], axis=-1).reshape(tm, d)


def _rms(x, g):
    return x * lax.rsqrt(jnp.mean(x * x, axis=-1, keepdims=True) + EPS) * g


def _inproj_kernel(*refs, add_pos, full, n_qkvg_tiles, n_u_tiles):
    refs = list(refs)
    x_ref = refs.pop(0)
    rt_ref, ct_ref = (refs.pop(0), refs.pop(0)) if add_pos else (None, None)
    sh_ref, sc_ref, g_ref, wa_ref = refs[:4]
    refs = refs[4:]
    wb_ref = refs.pop(0) if full else None
    wlr_ref, p_ref = refs.pop(0), refs.pop(0)
    u_ref, gates_ref = (refs.pop(0), refs.pop(0)) if full else (None, None)
    lr_ref, h_scr = refs
    j = pl.program_id(1)

    @pl.when(j == 0)
    def _():
        x = x_ref[...]
        if add_pos:
            x = _add_pos(x, rt_ref, ct_ref)
        h = _rms(x, g_ref[...]) * (1.0 + sc_ref[...]) + sh_ref[...]
        hb = h.astype(BF16)
        h_scr[...] = hb
        lr_ref[...] = jnp.dot(hb, wlr_ref[...], preferred_element_type=F32)

    def proj(w_ref):
        return jnp.dot(h_scr[...], w_ref[...], preferred_element_type=F32)

    if not full:
        p_ref[...] = proj(wa_ref).astype(BF16)
        return

    first_gate_tile = n_qkvg_tiles + n_u_tiles

    @pl.when(j < n_qkvg_tiles)
    def _():
        p_ref[...] = proj(wa_ref).astype(BF16)

    @pl.when((j >= n_qkvg_tiles) & (j < first_gate_tile))
    def _():
        u_ref[...] = proj(wb_ref)

    @pl.when(j >= first_gate_tile)
    def _():
        gates_ref[...] = proj(wb_ref).astype(BF16)


def _inproj(x2, mod3, mod_row_of_tile, norm_g, wa, wb, w_lr, pos_tabs, tm, full):
    ntok, d = x2.shape
    tn = 512
    na = QKVG_W // tn
    nu = FWIDTH // tn
    n_cols = (QKVG_W + FWIDTH + 2 * d) // tn if full else (2 * KEY_W + VAL_W) // tn
    add_pos = pos_tabs is not None
    in_specs = [pl.BlockSpec((tm, d), lambda i, j: (i, 0))]
    args = [x2]
    if add_pos:
        rt3, ct, tiles_per_seq = pos_tabs
        rpt = tm // GRID_W
        in_specs += [pl.BlockSpec((rpt, 1, d // 2), lambda i, j: (i % tiles_per_seq, 0, 0)),
                     pl.BlockSpec((GRID_W, d // 2), lambda i, j: (0, 0))]
        args += [rt3, ct]
    in_specs += [pl.BlockSpec((None, 1, d), lambda i, j: (mod_row_of_tile(i), 0, 0)),
                 pl.BlockSpec((None, 1, d), lambda i, j: (mod_row_of_tile(i), 0, 1)),
                 pl.BlockSpec((1, d), lambda i, j: (0, 0)),
                 pl.BlockSpec((d, tn), lambda i, j: (0, jnp.minimum(j, na - 1)))]
    args += [mod3, mod3, norm_g, wa]
    out_shape = [jax.ShapeDtypeStruct((ntok, min(n_cols, na) * tn), BF16)]
    out_specs = [pl.BlockSpec((tm, tn), lambda i, j: (i, jnp.minimum(j, na - 1)))]
    if full:
        in_specs.append(pl.BlockSpec((d, tn), lambda i, j: (0, jnp.maximum(j - na, 0))))
        args.append(wb)
        out_shape += [jax.ShapeDtypeStruct((ntok, FWIDTH), F32),
                      jax.ShapeDtypeStruct((ntok, 2 * d), BF16)]
        out_specs += [pl.BlockSpec((tm, tn), lambda i, j: (i, jnp.clip(j - na, 0, nu - 1))),
                      pl.BlockSpec((tm, tn), lambda i, j: (i, jnp.maximum(j - na - nu, 0)))]
    in_specs.append(pl.BlockSpec((d, LR_PAD), lambda i, j: (0, 0)))
    args.append(w_lr)
    out_shape.append(jax.ShapeDtypeStruct((ntok, LR_PAD), F32))
    out_specs.append(pl.BlockSpec((tm, LR_PAD), lambda i, j: (i, 0)))
    return pl.pallas_call(
        functools.partial(_inproj_kernel, add_pos=add_pos, full=full, n_qkvg_tiles=na, n_u_tiles=nu),
        out_shape=tuple(out_shape),
        grid=(ntok // tm, n_cols),
        in_specs=in_specs,
        out_specs=tuple(out_specs),
        scratch_shapes=[pltpu.VMEM((tm, d), BF16)],
        compiler_params=_cparams(("arbitrary", "arbitrary")),
        name="inproj_seq" if full else "inproj_ctx",
    )(*args)


def _log_sigmoid(z):
    return jnp.minimum(z, 0.0) - jnp.log1p(jnp.exp(-jnp.abs(z)))


def _gla_direction(q_ref, k_ref, v_ref, lr_ref, wlr_ref, blr_ref, s_scr, o_ref, backward):
    c = q_ref.shape[0]
    row = lax.broadcasted_iota(jnp.int32, (c, c), 0)
    col = lax.broadcasted_iota(jnp.int32, (c, c), 1)
    keep = (row <= col) if backward else (row >= col)
    tri = jnp.where(keep, 1.0, 0.0).astype(BF16)
    last = 0 if backward else c - 1
    mid = c // 2 if backward else c // 2 - 1

    z = jnp.dot(lr_ref[...].astype(BF16), wlr_ref[...], preferred_element_type=F32) + blr_ref[...]
    la = _log_sigmoid(z) * (1.0 / TAU)
    la_hi = la.astype(BF16)
    la_lo = (la - la_hi.astype(F32)).astype(BF16)
    cum2 = jnp.dot(tri, jnp.concatenate([la_hi, la_lo], axis=1), preferred_element_type=F32)
    cum_all = cum2[:, :KEY_W] + cum2[:, KEY_W:]

    for h in range(HEADS):
        cum = cum_all[:, h * DK:(h + 1) * DK]
        tot = cum[last:last + 1, :]
        ref_pt = cum[mid:mid + 1, :]
        q = q_ref[:, h * DK:(h + 1) * DK].astype(F32) * (DK ** -0.5)
        k = k_ref[:, h * DK:(h + 1) * DK].astype(F32)
        v = v_ref[:, h * DV:(h + 1) * DV]
        q_mid = (q * jnp.exp(cum - ref_pt)).astype(BF16)
        k_mid = (k * jnp.exp(ref_pt - cum)).astype(BF16)
        q_dec = (q * jnp.exp(cum)).astype(BF16)
        k_end = k * jnp.exp(tot - cum)
        scores = lax.dot_general(q_mid, k_mid, (((1,), (1,)), ((), ())), preferred_element_type=F32)
        scores = jnp.where(keep, scores, 0.0).astype(BF16)
        s_prev = s_scr[h]
        lhs = jnp.concatenate([scores, q_dec], axis=1)
        rhs = jnp.concatenate([v, s_prev.astype(BF16)], axis=0)
        o = jnp.dot(lhs, rhs, preferred_element_type=F32)
        if o_ref is not None:
            o_ref[:, h * DV:(h + 1) * DV] = o.astype(o_ref.dtype)
        kv = jnp.dot(k_end.T.astype(BF16), v, preferred_element_type=F32)
        dec_col = jnp.broadcast_to(jnp.exp(tot), (DK, DK)).T
        dec = jnp.concatenate([dec_col] * (DV // DK), axis=1)
        s_scr[h] = s_prev * dec + kv


def _gla_kernel(*refs, emit_o):
    (qf, kf, vf, lrf, qb, kb, vb, lrb, wf, bf, wb, bb, s0f, s0b) = refs[:14]
    if emit_o:
        of, ob, sf_out, sb_out, s_scr = refs[14:]
    else:
        sf_out, sb_out, s_scr = refs[14:]
        of = ob = None
    i = pl.program_id(1)

    @pl.when(i == 0)
    def _():
        s_scr[0] = s0f[...]
        s_scr[1] = s0b[...]

    _gla_direction(qf, kf, vf, lrf, wf, bf, s_scr.at[0], of, backward=False)
    _gla_direction(qb, kb, vb, lrb, wb, bb, s_scr.at[1], ob, backward=True)

    @pl.when(i == pl.num_programs(1) - 1)
    def _():
        sf_out[...] = s_scr[0]
        sb_out[...] = s_scr[1]


def _gla(p3, lr3, wlr_f, blr_f, wlr_b, blr_b, s0f, s0b, emit_o):
    b, t, _ = p3.shape
    c = GLA_CHUNK
    n = t // c
    fwd = lambda blk: (lambda bi, i: (bi, i, blk))
    bwd = lambda blk: (lambda bi, i: (bi, n - 1 - i, blk))

    def seq_specs(mk):
        return [pl.BlockSpec((None, c, KEY_W), mk(0)),
                pl.BlockSpec((None, c, KEY_W), mk(1)),
                pl.BlockSpec((None, c, VAL_W), mk(1)),
                pl.BlockSpec((None, c, LR_PAD), mk(0))]
    full2 = lambda shape: pl.BlockSpec(shape, lambda bi, i: (0, 0))
    st_spec = pl.BlockSpec((None, HEADS, DK, DV), lambda bi, i: (bi, 0, 0, 0))
    in_specs = (seq_specs(fwd) + seq_specs(bwd)
                + [full2(wlr_f.shape), full2(blr_f.shape), full2(wlr_b.shape), full2(blr_b.shape),
                   st_spec, st_spec])
    st_shape = jax.ShapeDtypeStruct((b, HEADS, DK, DV), F32)
    out_shape = [st_shape, st_shape]
    out_specs = [st_spec, st_spec]
    if emit_o:
        o_shape = jax.ShapeDtypeStruct((b, t, VAL_W), BF16)
        out_shape = [o_shape, o_shape] + out_shape
        out_specs = [pl.BlockSpec((None, c, VAL_W), fwd(0)), pl.BlockSpec((None, c, VAL_W), bwd(0))] + out_specs
    return pl.pallas_call(
        functools.partial(_gla_kernel, emit_o=emit_o),
        out_shape=tuple(out_shape),
        grid=(b, n),
        in_specs=in_specs,
        out_specs=tuple(out_specs),
        scratch_shapes=[pltpu.VMEM((2, HEADS, DK, DV), F32)],
        compiler_params=_cparams(("arbitrary", "arbitrary")),
        name="gla_seq" if emit_o else "gla_ctx",
    )(p3, p3, p3, lr3, p3, p3, p3, lr3, wlr_f, blr_f, wlr_b, blr_b, s0f, s0b)


def _cadd(a, b):
    return a[0] + b[0], a[1] + b[1]


def _csub(a, b):
    return a[0] - b[0], a[1] - b[1]


def _cmul_neg_i(a):
    return a[1], -a[0]


def _dft4(y):
    t0, t1 = _cadd(y[0], y[2]), _csub(y[0], y[2])
    t2, t3 = _cadd(y[1], y[3]), _cmul_neg_i(_csub(y[1], y[3]))
    return [_cadd(t0, t2), _cadd(t1, t3), _csub(t0, t2), _csub(t1, t3)]


def _dft8(z):
    r = math.sqrt(0.5)
    s = [_cadd(z[a], z[a + 4]) for a in range(4)]
    d = [_csub(z[a], z[a + 4]) for a in range(4)]
    d1 = ((d[1][0] + d[1][1]) * r, (d[1][1] - d[1][0]) * r)
    d2 = _cmul_neg_i(d[2])
    d3 = ((d[3][1] - d[3][0]) * r, (-d[3][1] - d[3][0]) * r)
    ev = _dft4(s)
    od = _dft4([d[0], d1, d2, d3])
    out = [None] * 8
    for j in range(4):
        out[2 * j] = ev[j]
        out[2 * j + 1] = od[j]
    return out


def _fft_kernel(*refs, n_slabs):
    u_refs = refs[:n_slabs]
    f_ref, twc_ref, tws_ref, xr_ref, xi_ref, ub_scr = refs[n_slabs:]
    m = ub_scr.shape[1]

    @pl.when(pl.program_id(2) == 0)
    def _():
        for a in range(RADIX):
            rows = [u[pl.ds(a, m, stride=RADIX), :].astype(BF16) for u in u_refs]
            ub_scr[a] = jnp.concatenate(rows, axis=1)

    t = f_ref.shape[0] // 2
    f = f_ref[...]
    z = []
    for a in range(RADIX):
        za = jnp.dot(f, ub_scr[a], preferred_element_type=F32)
        zr, zi = za[:t], za[t:]
        cc, ss = twc_ref[a], tws_ref[a]
        z.append((zr * cc + zi * ss, zi * cc - zr * ss))
    x = _dft8(z)
    for k1 in range(RADIX):
        xr_ref[k1] = x[k1][0].astype(xr_ref.dtype)
        xi_ref[k1] = x[k1][1].astype(xi_ref.dtype)


LANES = 128


def _fft(u3, fmat, twc, tws):
    b, seq, w = u3.shape
    m = seq // RADIX
    nt, t2, _ = fmat.shape
    t = t2 // 2
    chb = 256
    n_slabs = chb // LANES
    out = jax.ShapeDtypeStruct((b, RADIX, m, w), BF16)
    o_spec = pl.BlockSpec((None, RADIX, t, chb), lambda bi, cj, kt: (bi, 0, kt, cj))
    slab = lambda s: pl.BlockSpec((None, seq, LANES), lambda bi, cj, kt: (bi, 0, cj * n_slabs + s))
    return pl.pallas_call(
        functools.partial(_fft_kernel, n_slabs=n_slabs),
        out_shape=(out, out),
        grid=(b, w // chb, nt),
        scratch_shapes=[pltpu.VMEM((RADIX, m, chb), BF16)],
        in_specs=[slab(s) for s in range(n_slabs)] + [
                  pl.BlockSpec((None, t2, m), lambda bi, cj, kt: (kt, 0, 0)),
                  pl.BlockSpec((RADIX, t, 1), lambda bi, cj, kt: (0, kt, 0)),
                  pl.BlockSpec((RADIX, t, 1), lambda bi, cj, kt: (0, kt, 0))],
        out_specs=(o_spec, o_spec),
        compiler_params=_cparams(("arbitrary", "arbitrary", "arbitrary")),
        name="fft",
    )(*([u3] * n_slabs), fmat, twc, tws)


def _merge_kernel(of_ref, ob_ref, g_ref, gates_ref, xr_ref, xi_ref, x_ref, rt_ref, ct_ref, gt_ref,
                  gng_ref, wgo_ref, wfo_ref, wo_ref, csg_ref, o_ref):
    o = of_ref[...].astype(F32) + ob_ref[...].astype(F32)
    g = g_ref[...].astype(F32)
    gng = gng_ref[...]
    heads = []
    for h in range(HEADS):
        sl = slice(h * DV, (h + 1) * DV)
        heads.append((_rms(o[:, sl], gng) * _silu(g[:, sl])).astype(BF16))
    y_gla = jnp.dot(jnp.concatenate(heads, axis=1), wgo_ref[...], preferred_element_type=F32)

    csg = csg_ref[...]
    groups = []
    for gi in range(FGROUPS):
        sl = slice(gi * FGDIM, (gi + 1) * FGDIM)
        xg = jnp.concatenate([xr_ref[:, sl], xi_ref[:, sl]], axis=1)
        groups.append(jnp.dot(xg, csg, preferred_element_type=F32).astype(BF16))
    y_fft = jnp.dot(jnp.concatenate(groups, axis=1), wfo_ref[...], preferred_element_type=F32)

    gates = jax.nn.sigmoid(gates_ref[...].astype(F32))
    d = y_fft.shape[1]
    zmix = (gates[:, :d] * y_fft + gates[:, d:] * y_gla).astype(BF16)
    y = jnp.dot(zmix, wo_ref[...], preferred_element_type=F32)
    o_ref[...] = _add_pos(x_ref[...], rt_ref, ct_ref) + gt_ref[...] * y


def _const_spec(shape, nidx):
    zeros = (0,) * len(shape)
    return pl.BlockSpec(shape, lambda *idx: zeros, pipeline_mode=pl.Buffered(1))


def _merge(o_f, o_b, p2, gates, xr, xi, x2, rt3, ct, mod3, gng, wgo, wfo, wo, csg, t):
    ntok, d = x2.shape
    tm = 256
    tiles_per_seq = t // tm
    rpt = tm // GRID_W
    row = lambda blk: (lambda i: (i, blk))
    in_specs = [pl.BlockSpec((tm, VAL_W), row(0)),
                pl.BlockSpec((tm, VAL_W), row(0)),
                pl.BlockSpec((tm, VAL_W), row(2)),
                pl.BlockSpec((tm, 2 * d), row(0)),
                pl.BlockSpec((tm, FWIDTH), row(0)),
                pl.BlockSpec((tm, FWIDTH), row(0)),
                pl.BlockSpec((tm, d), row(0)),
                pl.BlockSpec((rpt, 1, d // 2), lambda i: (i % tiles_per_seq, 0, 0)),
                _const_spec((GRID_W, d // 2), 1),
                pl.BlockSpec((None, 1, d), lambda i: (i // tiles_per_seq, 0, 2)),
                _const_spec(gng.shape, 1), _const_spec(wgo.shape, 1), _const_spec(wfo.shape, 1),
                _const_spec(wo.shape, 1), _const_spec(csg.shape, 1)]
    return pl.pallas_call(
        _merge_kernel,
        out_shape=jax.ShapeDtypeStruct((ntok, d), F32),
        grid=(ntok // tm,),
        in_specs=in_specs,
        out_specs=pl.BlockSpec((tm, d), row(0)),
        compiler_params=_cparams(("arbitrary",)),
        name="merge",
    )(o_f, o_b, p2, gates, xr, xi, x2, rt3, ct, mod3, gng, wgo, wfo, wo, csg)


def _mlp_kernel(x_ref, sh_ref, sc_ref, gt_ref, g2_ref, w1_ref, w2_ref, fg_ref, o_ref, h_scr, acc_scr):
    j = pl.program_id(1)

    @pl.when(j == 0)
    def _():
        h = _rms(x_ref[...], g2_ref[...]) * (1.0 + sc_ref[...]) + sh_ref[...]
        h_scr[...] = h.astype(BF16)
        acc_scr[...] = jnp.zeros_like(acc_scr)

    hid = jnp.dot(h_scr[...], w1_ref[...], preferred_element_type=F32)
    hid = jnp.square(jnp.maximum(hid, 0.0)).astype(BF16)
    acc_scr[...] += jnp.dot(hid, w2_ref[...], preferred_element_type=F32)

    @pl.when(j == pl.num_programs(1) - 1)
    def _():
        xo = x_ref[...] + gt_ref[...] * acc_scr[...]
        o_ref[...] = _rms(xo, fg_ref[...])


def _mlp(x2, mod3, g2, w1, w2, fg, t):
    ntok, d = x2.shape
    dff = w1.shape[1]
    tm, tf = 512, 1024
    tiles_per_seq = t // tm
    modspec = lambda blk: pl.BlockSpec((None, 1, d), lambda i, j: (i // tiles_per_seq, 0, blk))
    return pl.pallas_call(
        _mlp_kernel,
        out_shape=jax.ShapeDtypeStruct((ntok, d), F32),
        grid=(ntok // tm, dff // tf),
        in_specs=[pl.BlockSpec((tm, d), lambda i, j: (i, 0)),
                  modspec(3), modspec(4), modspec(5),
                  pl.BlockSpec((1, d), lambda i, j: (0, 0)),
                  pl.BlockSpec((d, tf), lambda i, j: (0, j)),
                  pl.BlockSpec((tf, d), lambda i, j: (j, 0)),
                  pl.BlockSpec((1, d), lambda i, j: (0, 0))],
        out_specs=pl.BlockSpec((tm, d), lambda i, j: (i, 0)),
        scratch_shapes=[pltpu.VMEM((tm, d), BF16), pltpu.VMEM((tm, d), F32)],
        compiler_params=_cparams(("arbitrary", "arbitrary")),
        name="mlp",
    )(x2, mod3, mod3, mod3, g2, w1, w2, fg)


def _pos_tables(t, d):
    quarter = d // 4
    omega = 1.0 / (POS_TEMP ** (jnp.arange(quarter, dtype=F32) / quarter))
    er = jnp.arange(t // GRID_W, dtype=F32)[:, None] * omega[None, :]
    ec = jnp.arange(GRID_W, dtype=F32)[:, None] * omega[None, :]
    rt = jnp.concatenate([jnp.sin(er), jnp.cos(er)], axis=-1)
    ct = jnp.concatenate([jnp.sin(ec), jnp.cos(ec)], axis=-1)
    return rt[:, None, :], ct


def _dft_tables(t):
    m = t // RADIX
    tile = min(256, m)
    k = np.arange(m)
    ang = 2.0 * np.pi * ((k[:, None] * k[None, :]) % m) / m
    cos_t = np.cos(ang).reshape(m // tile, tile, m)
    sin_t = np.sin(ang).reshape(m // tile, tile, m)
    fmat = np.concatenate([cos_t, -sin_t], axis=1)
    a = np.arange(RADIX)
    tw = 2.0 * np.pi * (a[:, None] * k[None, :]) / t
    scale = 1.0 / math.sqrt(t * FGDIM)
    twc = (np.cos(tw) * scale)[:, :, None]
    tws = (np.sin(tw) * scale)[:, :, None]
    c = np.arange(FGDIM)
    cang = 2.0 * np.pi * ((c[:, None] * c[None, :]) % FGDIM) / FGDIM
    csg = np.concatenate([np.cos(cang), np.sin(cang)], axis=0)
    as_f32 = lambda a: jnp.asarray(a.astype(np.float32))
    return as_f32(fmat).astype(BF16), as_f32(twc), as_f32(tws), as_f32(csg).astype(BF16)


def _pad_lr_weight(w_lr, row0):
    out = jnp.zeros((LR_PAD, KEY_W), F32)
    return out.at[row0:row0 + RANK].set(w_lr).astype(BF16)


def kernel(x, c, ctx, c_ctx, w_mod, b_mod, norm1_g, norm2_g, w_in, w_lr_f, b_lr_f, w_lr_b, b_lr_b,
           gla_norm_g, w_fourier_out, w_gla_out, w_out, w_mlp_in, w_mlp_out, final_norm_g):
    b, t, d = x.shape
    tc = ctx.shape[1]
    depth = w_mod.shape[0]
    assert depth == 1 and d == D_MODEL and t % (RADIX * GRID_W) == 0 and tc % GLA_CHUNK == 0
    li = 0

    lr0 = QKVG_W
    u0 = lr0 + 2 * RANK
    wi = w_in[li]
    wa = wi[:, :lr0].astype(BF16)
    wb = wi[:, u0:].astype(BF16)
    w_lr = jnp.pad(wi[:, lr0:u0], ((0, 0), (0, LR_PAD - 2 * RANK))).astype(BF16)
    wlr_f, wlr_b = _pad_lr_weight(w_lr_f[li], 0), _pad_lr_weight(w_lr_b[li], RANK)
    blr_f, blr_b = b_lr_f[li][None, :], b_lr_b[li][None, :]
    wgo, wfo, wo = w_gla_out[li].astype(BF16), w_fourier_out[li].astype(BF16), w_out[li].astype(BF16)
    w1, w2 = w_mlp_in[li].astype(BF16), w_mlp_out[li].astype(BF16)

    rt3, ct = _pos_tables(t, d)
    fmat, twc, tws, csg = _dft_tables(t)

    rows = 8
    cpad = jnp.concatenate([c, c_ctx[None, :], jnp.zeros((rows - b - 1, d), F32)], axis=0)
    mod3 = _mod(cpad, w_mod[li], b_mod[li][None, :]).reshape(rows, 1, N_MOD * d)

    tm_ctx = min(512, b * tc)
    p_ctx, lr_ctx = _inproj(ctx.reshape(b * tc, d), mod3, lambda i: b, norm1_g[li][None, :],
                            wa, None, w_lr, None, tm_ctx, full=False)
    s_zero = jnp.zeros((b, HEADS, DK, DV), F32)
    s_f, s_b = _gla(p_ctx.reshape(b, tc, p_ctx.shape[1]), lr_ctx.reshape(b, tc, LR_PAD),
                    wlr_f, blr_f, wlr_b, blr_b, s_zero, s_zero, emit_o=False)

    tm = min(1024, t)
    tiles = t // tm
    x2 = x.reshape(b * t, d)
    p, u, gates, lr = _inproj(x2, mod3, lambda i: i // tiles, norm1_g[li][None, :], wa, wb, w_lr,
                              (rt3, ct, tiles), tm, full=True)
    o_f, o_b, _, _ = _gla(p.reshape(b, t, QKVG_W), lr.reshape(b, t, LR_PAD),
                          wlr_f, blr_f, wlr_b, blr_b, s_f, s_b, emit_o=True)

    xr, xi = _fft(u.reshape(b, t, FWIDTH), fmat, twc, tws)

    x1 = _merge(o_f.reshape(b * t, VAL_W), o_b.reshape(b * t, VAL_W), p, gates,
                xr.reshape(b * t, FWIDTH), xi.reshape(b * t, FWIDTH), x2, rt3, ct, mod3,
                gla_norm_g[li][None, :], wgo, wfo, wo, csg, t)

    out = _mlp(x1, mod3, norm2_g[li][None, :], w1, w2, final_norm_g[None, :], t)
    return out.reshape(b, t, d)
```

```python
import functools
import math

import jax
import jax.numpy as jnp
import numpy as np
from jax import lax
from jax.experimental import pallas as pl
from jax.experimental.pallas import tpu as pltpu

F32 = jnp.float32
BF16 = jnp.bfloat16

D_MODEL = 2048
GRID_W = 64
HEADS = 4
DK = 128
DV = 256
KEY_W = HEADS * DK
VAL_W = HEADS * DV
RANK = 16
TAU = 16.0
FGROUPS = 4
FGDIM = 256
FWIDTH = FGROUPS * FGDIM
D_FF = 4 * D_MODEL
N_MOD = 6
EPS = 1e-6
POS_TEMP = 10000.0

QKVG_W = 2 * KEY_W + 2 * VAL_W
P_WIDTH = QKVG_W + FWIDTH + 2 * D_MODEL
LR_PAD = 128
GLA_CHUNK = 128
RADIX = 8
LANES = 128

V7X_VMEM_LIMIT = 56 * 1024 * 1024


def _cparams(sem, vmem=V7X_VMEM_LIMIT):
    return pltpu.CompilerParams(dimension_semantics=sem, vmem_limit_bytes=vmem)


def _silu(x):
    return x * jax.nn.sigmoid(x)


def _mod_kernel(c_ref, w_ref, b_ref, o_ref):
    s = _silu(c_ref[...])
    o_ref[...] = jnp.dot(s, w_ref[...], preferred_element_type=F32,
                         precision=lax.Precision.HIGHEST) + b_ref[...]


def _mod(cpad, w_mod, b_mod):
    rows, d = cpad.shape
    n = w_mod.shape[1]
    tn = 1024
    return pl.pallas_call(
        _mod_kernel,
        out_shape=jax.ShapeDtypeStruct((rows, n), F32),
        grid=(n // tn,),
        in_specs=[pl.BlockSpec((rows, d), lambda j: (0, 0)),
                  pl.BlockSpec((d, tn), lambda j: (0, j)),
                  pl.BlockSpec((1, tn), lambda j: (0, j))],
        out_specs=pl.BlockSpec((rows, tn), lambda j: (0, j)),
        compiler_params=_cparams(("arbitrary",)),
        name="mod",
    )(cpad, w_mod, b_mod)


def _add_pos(x, rt_ref, ct_ref):
    tm, d = x.shape
    x3 = x.reshape(tm // GRID_W, GRID_W, d)
    half = d // 2
    lo = x3[:, :, :half] + rt_ref[...]
    hi = x3[:, :, half:] + ct_ref[...][None]
    return jnp.concatenate([lo, hi], axis=-1).reshape(tm, d)


def _rms(x, g):
    return x * lax.rsqrt(jnp.mean(x * x, axis=-1, keepdims=True) + EPS) * g


def _inproj_kernel(*refs, add_pos):
    if add_pos:
        x_ref, rt_ref, ct_ref, sh_ref, sc_ref, g_ref, w_ref, wlr_ref, p_ref, lr_ref, h_scr = refs
    else:
        x_ref, sh_ref, sc_ref, g_ref, w_ref, wlr_ref, p_ref, lr_ref, h_scr = refs

    @pl.when(pl.program_id(1) == 0)
    def _():
        x = x_ref[...]
        if add_pos:
            x = _add_pos(x, rt_ref, ct_ref)
        h = _rms(x, g_ref[...]) * (1.0 + sc_ref[...]) + sh_ref[...]
        hb = h.astype(BF16)
        h_scr[...] = hb
        lr_ref[...] = jnp.dot(hb, wlr_ref[...], preferred_element_type=F32)

    p_ref[...] = jnp.dot(h_scr[...], w_ref[...], preferred_element_type=F32).astype(BF16)


def _inproj(x2, mod3, mod_row_of_tile, norm_g, w_main, w_lr, pos_tabs, tm, width):
    ntok, d = x2.shape
    tn = 1024
    add_pos = pos_tabs is not None
    in_specs = [pl.BlockSpec((tm, d), lambda i, j: (i, 0))]
    args = [x2]
    if add_pos:
        rt3, ct, tiles_per_seq = pos_tabs
        rpt = tm // GRID_W
        in_specs += [pl.BlockSpec((rpt, 1, d // 2), lambda i, j: (i % tiles_per_seq, 0, 0)),
                     pl.BlockSpec((GRID_W, d // 2), lambda i, j: (0, 0))]
        args += [rt3, ct]
    in_specs += [pl.BlockSpec((None, 1, d), lambda i, j: (mod_row_of_tile(i), 0, 0)),
                 pl.BlockSpec((None, 1, d), lambda i, j: (mod_row_of_tile(i), 0, 1)),
                 pl.BlockSpec((1, d), lambda i, j: (0, 0)),
                 pl.BlockSpec((d, tn), lambda i, j: (0, j)),
                 pl.BlockSpec((d, LR_PAD), lambda i, j: (0, 0))]
    args += [mod3, mod3, norm_g, w_main, w_lr]
    return pl.pallas_call(
        functools.partial(_inproj_kernel, add_pos=add_pos),
        out_shape=(jax.ShapeDtypeStruct((ntok, width), BF16),
                   jax.ShapeDtypeStruct((ntok, LR_PAD), F32)),
        grid=(ntok // tm, width // tn),
        in_specs=in_specs,
        out_specs=(pl.BlockSpec((tm, tn), lambda i, j: (i, j)),
                   pl.BlockSpec((tm, LR_PAD), lambda i, j: (i, 0))),
        scratch_shapes=[pltpu.VMEM((tm, d), BF16)],
        compiler_params=_cparams(("arbitrary", "arbitrary")),
        name="inproj_pos" if add_pos else "inproj_ctx",
    )(*args)


def _log_sigmoid(z):
    return jnp.minimum(z, 0.0) - jnp.log1p(jnp.exp(-jnp.abs(z)))


def _gla_direction(q_ref, k_ref, v_ref, lr_ref, wlr_ref, blr_ref, s_scr, o_ref, backward):
    c = q_ref.shape[0]
    row = lax.broadcasted_iota(jnp.int32, (c, c), 0)
    col = lax.broadcasted_iota(jnp.int32, (c, c), 1)
    keep = (row <= col) if backward else (row >= col)
    tri = jnp.where(keep, 1.0, 0.0).astype(BF16)
    last = 0 if backward else c - 1
    mid = c // 2 if backward else c // 2 - 1

    z = jnp.dot(lr_ref[...].astype(BF16), wlr_ref[...], preferred_element_type=F32) + blr_ref[...]
    la = _log_sigmoid(z) * (1.0 / TAU)
    la_hi = la.astype(BF16)
    la_lo = (la - la_hi.astype(F32)).astype(BF16)
    cum2 = jnp.dot(tri, jnp.concatenate([la_hi, la_lo], axis=1), preferred_element_type=F32)
    cum_all = cum2[:, :KEY_W] + cum2[:, KEY_W:]

    for h in range(HEADS):
        cum = cum_all[:, h * DK:(h + 1) * DK]
        tot = cum[last:last + 1, :]
        ref_pt = cum[mid:mid + 1, :]
        q = q_ref[:, h * DK:(h + 1) * DK].astype(F32) * (DK ** -0.5)
        k = k_ref[:, h * DK:(h + 1) * DK].astype(F32)
        v = v_ref[:, h * DV:(h + 1) * DV]
        q_mid = (q * jnp.exp(cum - ref_pt)).astype(BF16)
        k_mid = (k * jnp.exp(ref_pt - cum)).astype(BF16)
        q_dec = (q * jnp.exp(cum)).astype(BF16)
        k_end = k * jnp.exp(tot - cum)
        scores = lax.dot_general(q_mid, k_mid, (((1,), (1,)), ((), ())), preferred_element_type=F32)
        scores = jnp.where(keep, scores, 0.0).astype(BF16)
        s_prev = s_scr[h]
        lhs = jnp.concatenate([scores, q_dec], axis=1)
        rhs = jnp.concatenate([v, s_prev.astype(BF16)], axis=0)
        o = jnp.dot(lhs, rhs, preferred_element_type=F32)
        if o_ref is not None:
            o_ref[:, h * DV:(h + 1) * DV] = o.astype(o_ref.dtype)
        kv = jnp.dot(k_end.T.astype(BF16), v, preferred_element_type=F32)
        dec_col = jnp.broadcast_to(jnp.exp(tot), (DK, DK)).T
        dec = jnp.concatenate([dec_col] * (DV // DK), axis=1)
        s_scr[h] = s_prev * dec + kv


def _gla_kernel(*refs, emit_o):
    (qf, kf, vf, lrf, qb, kb, vb, lrb, wf, bf, wb, bb, s0f, s0b) = refs[:14]
    if emit_o:
        of, ob, sf_out, sb_out, s_scr = refs[14:]
    else:
        sf_out, sb_out, s_scr = refs[14:]
        of = ob = None
    i = pl.program_id(1)

    @pl.when(i == 0)
    def _():
        s_scr[0] = s0f[...]
        s_scr[1] = s0b[...]

    _gla_direction(qf, kf, vf, lrf, wf, bf, s_scr.at[0], of, backward=False)
    _gla_direction(qb, kb, vb, lrb, wb, bb, s_scr.at[1], ob, backward=True)

    @pl.when(i == pl.num_programs(1) - 1)
    def _():
        sf_out[...] = s_scr[0]
        sb_out[...] = s_scr[1]


def _gla(p3, lr3, wlr_f, blr_f, wlr_b, blr_b, s0f, s0b, emit_o):
    b, t, _ = p3.shape
    c = GLA_CHUNK
    n = t // c
    fwd = lambda blk: (lambda bi, i: (bi, i, blk))
    bwd = lambda blk: (lambda bi, i: (bi, n - 1 - i, blk))

    def seq_specs(mk):
        return [pl.BlockSpec((None, c, KEY_W), mk(0)),
                pl.BlockSpec((None, c, KEY_W), mk(1)),
                pl.BlockSpec((None, c, VAL_W), mk(1)),
                pl.BlockSpec((None, c, LR_PAD), mk(0))]
    full2 = lambda shape: pl.BlockSpec(shape, lambda bi, i: (0, 0))
    st_spec = pl.BlockSpec((None, HEADS, DK, DV), lambda bi, i: (bi, 0, 0, 0))
    in_specs = (seq_specs(fwd) + seq_specs(bwd)
                + [full2(wlr_f.shape), full2(blr_f.shape), full2(wlr_b.shape), full2(blr_b.shape),
                   st_spec, st_spec])
    st_shape = jax.ShapeDtypeStruct((b, HEADS, DK, DV), F32)
    out_shape = [st_shape, st_shape]
    out_specs = [st_spec, st_spec]
    if emit_o:
        o_shape = jax.ShapeDtypeStruct((b, t, VAL_W), BF16)
        out_shape = [o_shape, o_shape] + out_shape
        out_specs = [pl.BlockSpec((None, c, VAL_W), fwd(0)), pl.BlockSpec((None, c, VAL_W), bwd(0))] + out_specs
    return pl.pallas_call(
        functools.partial(_gla_kernel, emit_o=emit_o),
        out_shape=tuple(out_shape),
        grid=(b, n),
        in_specs=in_specs,
        out_specs=tuple(out_specs),
        scratch_shapes=[pltpu.VMEM((2, HEADS, DK, DV), F32)],
        compiler_params=_cparams(("arbitrary", "arbitrary")),
        name="gla_seq" if emit_o else "gla_ctx",
    )(p3, p3, p3, lr3, p3, p3, p3, lr3, wlr_f, blr_f, wlr_b, blr_b, s0f, s0b)


def _cadd(a, b):
    return a[0] + b[0], a[1] + b[1]


def _csub(a, b):
    return a[0] - b[0], a[1] - b[1]


def _cmul_neg_i(a):
    return a[1], -a[0]


def _dft4(y):
    t0, t1 = _cadd(y[0], y[2]), _csub(y[0], y[2])
    t2, t3 = _cadd(y[1], y[3]), _cmul_neg_i(_csub(y[1], y[3]))
    return [_cadd(t0, t2), _cadd(t1, t3), _csub(t0, t2), _csub(t1, t3)]


def _dft8(z):
    r = math.sqrt(0.5)
    s = [_cadd(z[a], z[a + 4]) for a in range(4)]
    d = [_csub(z[a], z[a + 4]) for a in range(4)]
    d1 = ((d[1][0] + d[1][1]) * r, (d[1][1] - d[1][0]) * r)
    d2 = _cmul_neg_i(d[2])
    d3 = ((d[3][1] - d[3][0]) * r, (-d[3][1] - d[3][0]) * r)
    ev = _dft4(s)
    od = _dft4([d[0], d1, d2, d3])
    out = [None] * 8
    for j in range(4):
        out[2 * j] = ev[j]
        out[2 * j + 1] = od[j]
    return out


def _fft_kernel(*refs, n_slabs):
    u_refs = refs[:n_slabs]
    f_ref, twc_ref, tws_ref, xr_ref, xi_ref, wide_scr, ub_scr = refs[n_slabs:]
    m = ub_scr.shape[1]

    @pl.when(pl.program_id(2) == 0)
    def _():
        for s in range(n_slabs):
            wide_scr[s] = u_refs[s][...].astype(F32)
        for a in range(RADIX):
            rows = [wide_scr[s, pl.ds(a, m, stride=RADIX), :].astype(BF16) for s in range(n_slabs)]
            ub_scr[a] = jnp.concatenate(rows, axis=1)

    t = f_ref.shape[0] // 2
    f = f_ref[...]
    z = []
    for a in range(RADIX):
        za = jnp.dot(f, ub_scr[a], preferred_element_type=F32)
        zr, zi = za[:t], za[t:]
        cc, ss = twc_ref[a], tws_ref[a]
        z.append((zr * cc + zi * ss, zi * cc - zr * ss))
    x = _dft8(z)
    for k1 in range(RADIX):
        xr_ref[k1] = x[k1][0].astype(xr_ref.dtype)
        xi_ref[k1] = x[k1][1].astype(xi_ref.dtype)


def _fft(p3, col0, fmat, twc, tws):
    b, seq, _ = p3.shape
    w = FWIDTH
    m = seq // RADIX
    nt, t2, _ = fmat.shape
    t = t2 // 2
    chb = 256
    n_slabs = chb // LANES
    slab0 = col0 // LANES
    out = jax.ShapeDtypeStruct((b, RADIX, m, w), BF16)
    o_spec = pl.BlockSpec((None, RADIX, t, chb), lambda bi, cj, kt: (bi, 0, kt, cj))
    slab = lambda s: pl.BlockSpec((None, seq, LANES), lambda bi, cj, kt: (bi, 0, slab0 + cj * n_slabs + s))
    return pl.pallas_call(
        functools.partial(_fft_kernel, n_slabs=n_slabs),
        out_shape=(out, out),
        grid=(b, w // chb, nt),
        in_specs=[slab(s) for s in range(n_slabs)] + [
                  pl.BlockSpec((None, t2, m), lambda bi, cj, kt: (kt, 0, 0)),
                  pl.BlockSpec((RADIX, t, 1), lambda bi, cj, kt: (0, kt, 0)),
                  pl.BlockSpec((RADIX, t, 1), lambda bi, cj, kt: (0, kt, 0))],
        out_specs=(o_spec, o_spec),
        scratch_shapes=[pltpu.VMEM((n_slabs, seq, LANES), F32), pltpu.VMEM((RADIX, m, chb), BF16)],
        compiler_params=_cparams(("arbitrary", "arbitrary", "arbitrary")),
        name="fft",
    )(*([p3] * n_slabs), fmat, twc, tws)


def _merge_kernel(of_ref, ob_ref, g_ref, gates_ref, xr_ref, xi_ref, x_ref, rt_ref, ct_ref, gt_ref,
                  gng_ref, wgo_ref, wfo_ref, wo_ref, csg_ref, o_ref):
    o = of_ref[...].astype(F32) + ob_ref[...].astype(F32)
    g = g_ref[...].astype(F32)
    gng = gng_ref[...]
    heads = []
    for h in range(HEADS):
        sl = slice(h * DV, (h + 1) * DV)
        heads.append((_rms(o[:, sl], gng) * _silu(g[:, sl])).astype(BF16))
    y_gla = jnp.dot(jnp.concatenate(heads, axis=1), wgo_ref[...], preferred_element_type=F32)

    csg = csg_ref[...]
    groups = []
    for gi in range(FGROUPS):
        sl = slice(gi * FGDIM, (gi + 1) * FGDIM)
        xg = jnp.concatenate([xr_ref[:, sl], xi_ref[:, sl]], axis=1)
        groups.append(jnp.dot(xg, csg, preferred_element_type=F32).astype(BF16))
    y_fft = jnp.dot(jnp.concatenate(groups, axis=1), wfo_ref[...], preferred_element_type=F32)

    gates = jax.nn.sigmoid(gates_ref[...].astype(F32))
    d = y_fft.shape[1]
    zmix = (gates[:, :d] * y_fft + gates[:, d:] * y_gla).astype(BF16)
    y = jnp.dot(zmix, wo_ref[...], preferred_element_type=F32)
    o_ref[...] = _add_pos(x_ref[...], rt_ref, ct_ref) + gt_ref[...] * y


def _const_spec(shape):
    zeros = (0,) * len(shape)
    return pl.BlockSpec(shape, lambda *idx: zeros, pipeline_mode=pl.Buffered(1))


def _merge(o_f, o_b, p2, xr, xi, x2, rt3, ct, mod3, gng, wgo, wfo, wo, csg, t):
    ntok, d = x2.shape
    tm = 256
    tiles_per_seq = t // tm
    rpt = tm // GRID_W
    row = lambda blk: (lambda i: (i, blk))
    in_specs = [pl.BlockSpec((tm, VAL_W), row(0)),
                pl.BlockSpec((tm, VAL_W), row(0)),
                pl.BlockSpec((tm, VAL_W), row(2)),
                pl.BlockSpec((tm, 2 * d), row(1)),
                pl.BlockSpec((tm, FWIDTH), row(0)),
                pl.BlockSpec((tm, FWIDTH), row(0)),
                pl.BlockSpec((tm, d), row(0)),
                pl.BlockSpec((rpt, 1, d // 2), lambda i: (i % tiles_per_seq, 0, 0)),
                _const_spec((GRID_W, d // 2)),
                pl.BlockSpec((None, 1, d), lambda i: (i // tiles_per_seq, 0, 2)),
                _const_spec(gng.shape), _const_spec(wgo.shape), _const_spec(wfo.shape),
                _const_spec(wo.shape), _const_spec(csg.shape)]
    return pl.pallas_call(
        _merge_kernel,
        out_shape=jax.ShapeDtypeStruct((ntok, d), F32),
        grid=(ntok // tm,),
        in_specs=in_specs,
        out_specs=pl.BlockSpec((tm, d), row(0)),
        compiler_params=_cparams(("arbitrary",)),
        name="merge",
    )(o_f, o_b, p2, p2, xr, xi, x2, rt3, ct, mod3, gng, wgo, wfo, wo, csg)


def _mlp_kernel(x_ref, xn_ref, sh_ref, sc_ref, shn_ref, scn_ref, gt_ref, g2_ref, w1_ref, w2_ref, fg_ref,
                o_ref, h_even, h_odd):
    i, j = pl.program_id(0), pl.program_id(1)
    nf = pl.num_programs(1)
    tm = x_ref.shape[0]
    slab = tm // nf
    slot = i % 2

    def normed(x, sh, sc):
        return (_rms(x, g2_ref[...]) * (1.0 + sc) + sh).astype(BF16)

    @pl.when((i == 0) & (j == 0))
    def _():
        h_even[...] = normed(x_ref[...], sh_ref[...], sc_ref[...])

    @pl.when(j == 0)
    def _():
        o_ref[...] = jnp.zeros_like(o_ref)

    def step(h_cur, h_next):
        r0 = pl.multiple_of(j * slab, slab)
        h_next[pl.ds(r0, slab), :] = normed(xn_ref[pl.ds(r0, slab), :], shn_ref[...], scn_ref[...])
        hid = jnp.dot(h_cur[...], w1_ref[...], preferred_element_type=F32)
        hid = jnp.square(jnp.maximum(hid, 0.0)).astype(BF16)
        o_ref[...] += jnp.dot(hid, w2_ref[...], preferred_element_type=F32)

    @pl.when(slot == 0)
    def _():
        step(h_even, h_odd)

    @pl.when(slot == 1)
    def _():
        step(h_odd, h_even)

    @pl.when(j == nf - 1)
    def _():
        xo = x_ref[...] + gt_ref[...] * o_ref[...]
        o_ref[...] = _rms(xo, fg_ref[...])


def _mlp(x2, mod3, g2, w1, w2, fg, t):
    ntok, d = x2.shape
    dff = w1.shape[1]
    tm, tf = 512, 1024
    assert (tm // (dff // tf)) % 8 == 0
    nt = ntok // tm
    tiles_per_seq = t // tm
    nxt = lambda i: jnp.minimum(i + 1, nt - 1)
    modspec = lambda blk: pl.BlockSpec((None, 1, d), lambda i, j: (i // tiles_per_seq, 0, blk))
    modspec_next = lambda blk: pl.BlockSpec((None, 1, d), lambda i, j: (nxt(i) // tiles_per_seq, 0, blk))
    return pl.pallas_call(
        _mlp_kernel,
        out_shape=jax.ShapeDtypeStruct((ntok, d), F32),
        grid=(nt, dff // tf),
        in_specs=[pl.BlockSpec((tm, d), lambda i, j: (i, 0)),
                  pl.BlockSpec((tm, d), lambda i, j: (nxt(i), 0)),
                  modspec(3), modspec(4), modspec_next(3), modspec_next(4), modspec(5),
                  pl.BlockSpec((1, d), lambda i, j: (0, 0)),
                  pl.BlockSpec((d, tf), lambda i, j: (0, j)),
                  pl.BlockSpec((tf, d), lambda i, j: (j, 0)),
                  pl.BlockSpec((1, d), lambda i, j: (0, 0))],
        out_specs=pl.BlockSpec((tm, d), lambda i, j: (i, 0)),
        scratch_shapes=[pltpu.VMEM((tm, d), BF16), pltpu.VMEM((tm, d), BF16)],
        compiler_params=_cparams(("arbitrary", "arbitrary")),
        name="mlp",
    )(x2, x2, mod3, mod3, mod3, mod3, mod3, g2, w1, w2, fg)


def _pos_tables(t, d):
    quarter = d // 4
    omega = 1.0 / (POS_TEMP ** (jnp.arange(quarter, dtype=F32) / quarter))
    er = jnp.arange(t // GRID_W, dtype=F32)[:, None] * omega[None, :]
    ec = jnp.arange(GRID_W, dtype=F32)[:, None] * omega[None, :]
    rt = jnp.concatenate([jnp.sin(er), jnp.cos(er)], axis=-1)
    ct = jnp.concatenate([jnp.sin(ec), jnp.cos(ec)], axis=-1)
    return rt[:, None, :], ct


def _dft_tables(t):
    m = t // RADIX
    tile = min(256, m)
    k = np.arange(m)
    ang = 2.0 * np.pi * ((k[:, None] * k[None, :]) % m) / m
    cos_t = np.cos(ang).reshape(m // tile, tile, m)
    sin_t = np.sin(ang).reshape(m // tile, tile, m)
    fmat = np.concatenate([cos_t, -sin_t], axis=1)
    a = np.arange(RADIX)
    tw = 2.0 * np.pi * (a[:, None] * k[None, :]) / t
    scale = 1.0 / math.sqrt(t * FGDIM)
    twc = (np.cos(tw) * scale)[:, :, None]
    tws = (np.sin(tw) * scale)[:, :, None]
    c = np.arange(FGDIM)
    cang = 2.0 * np.pi * ((c[:, None] * c[None, :]) % FGDIM) / FGDIM
    csg = np.concatenate([np.cos(cang), np.sin(cang)], axis=0)
    as_f32 = lambda a: jnp.asarray(a.astype(np.float32))
    return as_f32(fmat).astype(BF16), as_f32(twc), as_f32(tws), as_f32(csg).astype(BF16)


def _pad_lr_weight(w_lr, row0):
    out = jnp.zeros((LR_PAD, KEY_W), F32)
    return out.at[row0:row0 + RANK].set(w_lr).astype(BF16)


def kernel(x, c, ctx, c_ctx, w_mod, b_mod, norm1_g, norm2_g, w_in, w_lr_f, b_lr_f, w_lr_b, b_lr_b,
           gla_norm_g, w_fourier_out, w_gla_out, w_out, w_mlp_in, w_mlp_out, final_norm_g):
    b, t, d = x.shape
    tc = ctx.shape[1]
    depth = w_mod.shape[0]
    assert depth == 1 and d == D_MODEL and t % (RADIX * GRID_W) == 0 and tc % GLA_CHUNK == 0
    li = 0

    lr0 = QKVG_W
    u0 = lr0 + 2 * RANK
    wi = w_in[li]
    w_main = jnp.concatenate([wi[:, :lr0], wi[:, u0:]], axis=1).astype(BF16)
    w_lr = jnp.pad(wi[:, lr0:u0], ((0, 0), (0, LR_PAD - 2 * RANK))).astype(BF16)
    wlr_f, wlr_b = _pad_lr_weight(w_lr_f[li], 0), _pad_lr_weight(w_lr_b[li], RANK)
    blr_f, blr_b = b_lr_f[li][None, :], b_lr_b[li][None, :]
    wgo, wfo, wo = w_gla_out[li].astype(BF16), w_fourier_out[li].astype(BF16), w_out[li].astype(BF16)
    w1, w2 = w_mlp_in[li].astype(BF16), w_mlp_out[li].astype(BF16)

    rt3, ct = _pos_tables(t, d)
    fmat, twc, tws, csg = _dft_tables(t)

    rows = 8
    cpad = jnp.concatenate([c, c_ctx[None, :], jnp.zeros((rows - b - 1, d), F32)], axis=0)
    mod3 = _mod(cpad, w_mod[li], b_mod[li][None, :]).reshape(rows, 1, N_MOD * d)

    tm_ctx = min(512, b * tc)
    qkv_w = 2 * KEY_W + VAL_W
    p_ctx, lr_ctx = _inproj(ctx.reshape(b * tc, d), mod3, lambda i: b, norm1_g[li][None, :],
                            w_main, w_lr, None, tm_ctx, qkv_w)
    s_zero = jnp.zeros((b, HEADS, DK, DV), F32)
    s_f, s_b = _gla(p_ctx.reshape(b, tc, qkv_w), lr_ctx.reshape(b, tc, LR_PAD),
                    wlr_f, blr_f, wlr_b, blr_b, s_zero, s_zero, emit_o=False)

    tm = min(1024, t)
    tiles = t // tm
    x2 = x.reshape(b * t, d)
    p, lr = _inproj(x2, mod3, lambda i: i // tiles, norm1_g[li][None, :], w_main, w_lr,
                    (rt3, ct, tiles), tm, P_WIDTH)
    p3 = p.reshape(b, t, P_WIDTH)
    o_f, o_b, _, _ = _gla(p3, lr.reshape(b, t, LR_PAD), wlr_f, blr_f, wlr_b, blr_b, s_f, s_b, emit_o=True)

    xr, xi = _fft(p3, QKVG_W, fmat, twc, tws)

    x1 = _merge(o_f.reshape(b * t, VAL_W), o_b.reshape(b * t, VAL_W), p,
                xr.reshape(b * t, FWIDTH), xi.reshape(b * t, FWIDTH), x2, rt3, ct, mod3,
                gla_norm_g[li][None, :], wgo, wfo, wo, csg, t)

    out = _mlp(x1, mod3, norm2_g[li][None, :], w1, w2, final_norm_g[None, :], t)
    return out.reshape(b, t, d)
```

```python
import functools
import math

import jax
import jax.numpy as jnp
import numpy as np
from jax import lax
from jax.experimental import pallas as pl
from jax.experimental.pallas import tpu as pltpu

F32 = jnp.float32
BF16 = jnp.bfloat16

D_MODEL = 2048
GRID_W = 64
HEADS = 4
DK = 128
DV = 256
KEY_W = HEADS * DK
VAL_W = HEADS * DV
RANK = 16
TAU = 16.0
FGROUPS = 4
FGDIM = 256
FWIDTH = FGROUPS * FGDIM
D_FF = 4 * D_MODEL
N_MOD = 6
EPS = 1e-6
POS_TEMP = 10000.0

QKVG_W = 2 * KEY_W + 2 * VAL_W
P_WIDTH = QKVG_W + FWIDTH + 2 * D_MODEL
LR_PAD = 128
GLA_CHUNK = 128
GLA_CHUNKS_PER_STEP = 4
RADIX = 8
LANES = 128

V7X_VMEM_LIMIT = 56 * 1024 * 1024


def _cparams(sem, vmem=V7X_VMEM_LIMIT):
    return pltpu.CompilerParams(dimension_semantics=sem, vmem_limit_bytes=vmem)


def _silu(x):
    return x * jax.nn.sigmoid(x)


def _mod_kernel(c_ref, w_ref, b_ref, o_ref):
    s = _silu(c_ref[...])
    o_ref[...] = jnp.dot(s, w_ref[...], preferred_element_type=F32,
                         precision=lax.Precision.HIGHEST) + b_ref[...]


def _mod(cpad, w_mod, b_mod):
    rows, d = cpad.shape
    n = w_mod.shape[1]
    tn = 1024
    return pl.pallas_call(
        _mod_kernel,
        out_shape=jax.ShapeDtypeStruct((rows, n), F32),
        grid=(n // tn,),
        in_specs=[pl.BlockSpec((rows, d), lambda j: (0, 0)),
                  pl.BlockSpec((d, tn), lambda j: (0, j)),
                  pl.BlockSpec((1, tn), lambda j: (0, j))],
        out_specs=pl.BlockSpec((rows, tn), lambda j: (0, j)),
        compiler_params=_cparams(("arbitrary",)),
        name="mod",
    )(cpad, w_mod, b_mod)


def _add_pos(x, rt_ref, ct_ref):
    tm, d = x.shape
    x3 = x.reshape(tm // GRID_W, GRID_W, d)
    half = d // 2
    lo = x3[:, :, :half] + rt_ref[...]
    hi = x3[:, :, half:] + ct_ref[...][None]
    return jnp.concatenate([lo, hi], axis=-1).reshape(tm, d)


def _rms(x, g):
    return x * lax.rsqrt(jnp.mean(x * x, axis=-1, keepdims=True) + EPS) * g


def _inproj_kernel(*refs, add_pos):
    if add_pos:
        x_ref, rt_ref, ct_ref, sh_ref, sc_ref, g_ref, w_ref, wlr_ref, p_ref, lr_ref, h_scr = refs
    else:
        x_ref, sh_ref, sc_ref, g_ref, w_ref, wlr_ref, p_ref, lr_ref, h_scr = refs

    @pl.when(pl.program_id(1) == 0)
    def _():
        x = x_ref[...]
        if add_pos:
            x = _add_pos(x, rt_ref, ct_ref)
        h = _rms(x, g_ref[...]) * (1.0 + sc_ref[...]) + sh_ref[...]
        hb = h.astype(BF16)
        h_scr[...] = hb
        lr_ref[...] = jnp.dot(hb, wlr_ref[...], preferred_element_type=F32)

    p_ref[...] = jnp.dot(h_scr[...], w_ref[...], preferred_element_type=F32).astype(BF16)


def _inproj(x2, mod3, mod_row_of_tile, norm_g, w_main, w_lr, pos_tabs, tm, width):
    ntok, d = x2.shape
    tn = 1024
    add_pos = pos_tabs is not None
    in_specs = [pl.BlockSpec((tm, d), lambda i, j: (i, 0))]
    args = [x2]
    if add_pos:
        rt3, ct, tiles_per_seq = pos_tabs
        rpt = tm // GRID_W
        in_specs += [pl.BlockSpec((rpt, 1, d // 2), lambda i, j: (i % tiles_per_seq, 0, 0)),
                     pl.BlockSpec((GRID_W, d // 2), lambda i, j: (0, 0))]
        args += [rt3, ct]
    in_specs += [pl.BlockSpec((None, 1, d), lambda i, j: (mod_row_of_tile(i), 0, 0)),
                 pl.BlockSpec((None, 1, d), lambda i, j: (mod_row_of_tile(i), 0, 1)),
                 pl.BlockSpec((1, d), lambda i, j: (0, 0)),
                 pl.BlockSpec((d, tn), lambda i, j: (0, j)),
                 pl.BlockSpec((d, LR_PAD), lambda i, j: (0, 0))]
    args += [mod3, mod3, norm_g, w_main, w_lr]
    return pl.pallas_call(
        functools.partial(_inproj_kernel, add_pos=add_pos),
        out_shape=(jax.ShapeDtypeStruct((ntok, width), BF16),
                   jax.ShapeDtypeStruct((ntok, LR_PAD), F32)),
        grid=(ntok // tm, width // tn),
        in_specs=in_specs,
        out_specs=(pl.BlockSpec((tm, tn), lambda i, j: (i, j)),
                   pl.BlockSpec((tm, LR_PAD), lambda i, j: (i, 0))),
        scratch_shapes=[pltpu.VMEM((tm, d), BF16)],
        compiler_params=_cparams(("arbitrary", "arbitrary")),
        name="inproj_pos" if add_pos else "inproj_ctx",
    )(*args)


def _log_sigmoid(z):
    return jnp.minimum(z, 0.0) - jnp.log1p(jnp.exp(-jnp.abs(z)))


def _gla_chunks(dirs, c):
    n_sub = dirs[0][0].shape[0] // c
    row = lax.broadcasted_iota(jnp.int32, (c, c), 0)
    col = lax.broadcasted_iota(jnp.int32, (c, c), 1)

    chains = {}
    for di, (qk_ref, v_ref, lr_ref, wlr_ref, blr_ref, s_scr, o_ref, backward) in enumerate(dirs):
        keep = (row <= col) if backward else (row >= col)
        tri = jnp.where(keep, 1.0, 0.0).astype(BF16)
        last = 0 if backward else c - 1
        mid = c // 2 if backward else c // 2 - 1
        order = range(n_sub - 1, -1, -1) if backward else range(n_sub)
        for step, sub in enumerate(order):
            rows = slice(sub * c, (sub + 1) * c)
            z = jnp.dot(lr_ref[rows, :].astype(BF16), wlr_ref[...], preferred_element_type=F32) + blr_ref[...]
            la = _log_sigmoid(z) * (1.0 / TAU)
            la_hi = la.astype(BF16)
            la_lo = (la - la_hi.astype(F32)).astype(BF16)
            cum2 = jnp.dot(tri, jnp.concatenate([la_hi, la_lo], axis=1), preferred_element_type=F32)
            cum_all = cum2[:, :KEY_W] + cum2[:, KEY_W:]
            for h in range(HEADS):
                cum = cum_all[:, h * DK:(h + 1) * DK]
                tot = cum[last:last + 1, :]
                ref_pt = cum[mid:mid + 1, :]
                q = qk_ref[rows, h * DK:(h + 1) * DK].astype(F32) * (DK ** -0.5)
                k = qk_ref[rows, KEY_W + h * DK:KEY_W + (h + 1) * DK].astype(F32)
                chains[(step, di, h)] = dict(
                    keep=keep, rows=rows,
                    v=v_ref[rows, h * DV:(h + 1) * DV],
                    q_mid=(q * jnp.exp(cum - ref_pt)).astype(BF16),
                    k_mid=(k * jnp.exp(ref_pt - cum)).astype(BF16),
                    q_dec=(q * jnp.exp(cum)).astype(BF16),
                    k_end_t=(k * jnp.exp(tot - cum)).T.astype(BF16),
                    dec_col=jnp.broadcast_to(jnp.exp(tot), (DK, DK)).T)

    for ch in chains.values():
        s = lax.dot_general(ch["q_mid"], ch["k_mid"], (((1,), (1,)), ((), ())), preferred_element_type=F32)
        ch["scores"] = jnp.where(ch["keep"], s, 0.0).astype(BF16)
    for ch in chains.values():
        ch["kv"] = jnp.dot(ch["k_end_t"], ch["v"], preferred_element_type=F32)

    state = {(di, h): d[5][h] for di, d in enumerate(dirs) for h in range(HEADS)}
    for step in range(n_sub):
        for di, d in enumerate(dirs):
            o_ref = d[6]
            for h in range(HEADS):
                ch = chains[(step, di, h)]
                s_prev = state[(di, h)]
                lhs = jnp.concatenate([ch["scores"], ch["q_dec"]], axis=1)
                rhs = jnp.concatenate([ch["v"], s_prev.astype(BF16)], axis=0)
                o = jnp.dot(lhs, rhs, preferred_element_type=F32)
                if o_ref is not None:
                    o_ref[ch["rows"], h * DV:(h + 1) * DV] = o.astype(o_ref.dtype)
                dec = jnp.concatenate([ch["dec_col"]] * (DV // DK), axis=1)
                state[(di, h)] = s_prev * dec + ch["kv"]
    for (di, h), s in state.items():
        dirs[di][5][h] = s


def _gla_kernel(*refs, emit_o):
    (qkf, vf, lrf, qkb, vb, lrb, wf, bf, wb, bb, s0f, s0b) = refs[:12]
    if emit_o:
        of, ob, sf_out, sb_out, s_scr = refs[12:]
    else:
        sf_out, sb_out, s_scr = refs[12:]
        of = ob = None
    i = pl.program_id(1)

    @pl.when(i == 0)
    def _():
        s_scr[0] = s0f[...]
        s_scr[1] = s0b[...]

    _gla_chunks([(qkf, vf, lrf, wf, bf, s_scr.at[0], of, False),
                 (qkb, vb, lrb, wb, bb, s_scr.at[1], ob, True)], GLA_CHUNK)

    @pl.when(i == pl.num_programs(1) - 1)
    def _():
        sf_out[...] = s_scr[0]
        sb_out[...] = s_scr[1]


def _gla(p3, lr3, wlr_f, blr_f, wlr_b, blr_b, s0f, s0b, emit_o):
    b, t, _ = p3.shape
    per_step = max(s for s in range(1, GLA_CHUNKS_PER_STEP + 1) if t % (GLA_CHUNK * s) == 0)
    c = GLA_CHUNK * per_step
    n = t // c
    fwd = lambda blk: (lambda bi, i: (bi, i, blk))
    bwd = lambda blk: (lambda bi, i: (bi, n - 1 - i, blk))

    def seq_specs(mk):
        return [pl.BlockSpec((None, c, 2 * KEY_W), mk(0)),
                pl.BlockSpec((None, c, VAL_W), mk(1)),
                pl.BlockSpec((None, c, LR_PAD), mk(0))]
    full2 = lambda shape: pl.BlockSpec(shape, lambda bi, i: (0, 0))
    st_spec = pl.BlockSpec((None, HEADS, DK, DV), lambda bi, i: (bi, 0, 0, 0))
    in_specs = (seq_specs(fwd) + seq_specs(bwd)
                + [full2(wlr_f.shape), full2(blr_f.shape), full2(wlr_b.shape), full2(blr_b.shape),
                   st_spec, st_spec])
    st_shape = jax.ShapeDtypeStruct((b, HEADS, DK, DV), F32)
    out_shape = [st_shape, st_shape]
    out_specs = [st_spec, st_spec]
    if emit_o:
        o_shape = jax.ShapeDtypeStruct((b, t, VAL_W), BF16)
        out_shape = [o_shape, o_shape] + out_shape
        out_specs = [pl.BlockSpec((None, c, VAL_W), fwd(0)), pl.BlockSpec((None, c, VAL_W), bwd(0))] + out_specs
    return pl.pallas_call(
        functools.partial(_gla_kernel, emit_o=emit_o),
        out_shape=tuple(out_shape),
        grid=(b, n),
        in_specs=in_specs,
        out_specs=tuple(out_specs),
        scratch_shapes=[pltpu.VMEM((2, HEADS, DK, DV), F32)],
        compiler_params=_cparams(("arbitrary", "arbitrary")),
        name="gla_seq" if emit_o else "gla_ctx",
    )(p3, p3, lr3, p3, p3, lr3, wlr_f, blr_f, wlr_b, blr_b, s0f, s0b)


def _cadd(a, b):
    return a[0] + b[0], a[1] + b[1]


def _csub(a, b):
    return a[0] - b[0], a[1] - b[1]


def _cmul_neg_i(a):
    return a[1], -a[0]


def _dft4(y):
    t0, t1 = _cadd(y[0], y[2]), _csub(y[0], y[2])
    t2, t3 = _cadd(y[1], y[3]), _cmul_neg_i(_csub(y[1], y[3]))
    return [_cadd(t0, t2), _cadd(t1, t3), _csub(t0, t2), _csub(t1, t3)]


def _dft8(z):
    r = math.sqrt(0.5)
    s = [_cadd(z[a], z[a + 4]) for a in range(4)]
    d = [_csub(z[a], z[a + 4]) for a in range(4)]
    d1 = ((d[1][0] + d[1][1]) * r, (d[1][1] - d[1][0]) * r)
    d2 = _cmul_neg_i(d[2])
    d3 = ((d[3][1] - d[3][0]) * r, (-d[3][1] - d[3][0]) * r)
    ev = _dft4(s)
    od = _dft4([d[0], d1, d2, d3])
    out = [None] * 8
    for j in range(4):
        out[2 * j] = ev[j]
        out[2 * j + 1] = od[j]
    return out


def _fft_kernel(*refs, n_slabs):
    u_refs = refs[:n_slabs]
    f_ref, twc_ref, tws_ref, xr_ref, xi_ref, wide_scr, ub_scr = refs[n_slabs:]
    m = ub_scr.shape[1]

    @pl.when(pl.program_id(2) == 0)
    def _():
        for s in range(n_slabs):
            wide_scr[s] = u_refs[s][...].astype(F32)
        for a in range(RADIX):
            rows = [wide_scr[s, pl.ds(a, m, stride=RADIX), :].astype(BF16) for s in range(n_slabs)]
            ub_scr[a] = jnp.concatenate(rows, axis=1)

    t = f_ref.shape[0] // 2
    f = f_ref[...]
    z = []
    for a in range(RADIX):
        za = jnp.dot(f, ub_scr[a], preferred_element_type=F32)
        zr, zi = za[:t], za[t:]
        cc, ss = twc_ref[a], tws_ref[a]
        z.append((zr * cc + zi * ss, zi * cc - zr * ss))
    x = _dft8(z)
    for k1 in range(RADIX):
        xr_ref[k1] = x[k1][0].astype(xr_ref.dtype)
        xi_ref[k1] = x[k1][1].astype(xi_ref.dtype)


def _fft(p3, col0, fmat, twc, tws):
    b, seq, _ = p3.shape
    w = FWIDTH
    m = seq // RADIX
    nt, t2, _ = fmat.shape
    t = t2 // 2
    chb = 256
    n_slabs = chb // LANES
    slab0 = col0 // LANES
    out = jax.ShapeDtypeStruct((b, RADIX, m, w), BF16)
    o_spec = pl.BlockSpec((None, RADIX, t, chb), lambda bi, cj, kt: (bi, 0, kt, cj))
    slab = lambda s: pl.BlockSpec((None, seq, LANES), lambda bi, cj, kt: (bi, 0, slab0 + cj * n_slabs + s))
    return pl.pallas_call(
        functools.partial(_fft_kernel, n_slabs=n_slabs),
        out_shape=(out, out),
        grid=(b, w // chb, nt),
        in_specs=[slab(s) for s in range(n_slabs)] + [
                  pl.BlockSpec((None, t2, m), lambda bi, cj, kt: (kt, 0, 0)),
                  pl.BlockSpec((RADIX, t, 1), lambda bi, cj, kt: (0, kt, 0)),
                  pl.BlockSpec((RADIX, t, 1), lambda bi, cj, kt: (0, kt, 0))],
        out_specs=(o_spec, o_spec),
        scratch_shapes=[pltpu.VMEM((n_slabs, seq, LANES), F32), pltpu.VMEM((RADIX, m, chb), BF16)],
        compiler_params=_cparams(("arbitrary", "arbitrary", "arbitrary")),
        name="fft",
    )(*([p3] * n_slabs), fmat, twc, tws)


def _merge_kernel(of_ref, ob_ref, g_ref, gates_ref, xr_ref, xi_ref, x_ref, rt_ref, ct_ref, gt_ref,
                  gng_ref, wgo_ref, wfo_ref, wo_ref, csg_ref, o_ref):
    o = of_ref[...].astype(F32) + ob_ref[...].astype(F32)
    g = g_ref[...].astype(F32)
    gng = gng_ref[...]
    heads = []
    for h in range(HEADS):
        sl = slice(h * DV, (h + 1) * DV)
        heads.append((_rms(o[:, sl], gng) * _silu(g[:, sl])).astype(BF16))
    y_gla = jnp.dot(jnp.concatenate(heads, axis=1), wgo_ref[...], preferred_element_type=F32)

    csg = csg_ref[...]
    groups = []
    for gi in range(FGROUPS):
        sl = slice(gi * FGDIM, (gi + 1) * FGDIM)
        xg = jnp.concatenate([xr_ref[:, sl], xi_ref[:, sl]], axis=1)
        groups.append(jnp.dot(xg, csg, preferred_element_type=F32).astype(BF16))
    y_fft = jnp.dot(jnp.concatenate(groups, axis=1), wfo_ref[...], preferred_element_type=F32)

    gates = jax.nn.sigmoid(gates_ref[...].astype(F32))
    d = y_fft.shape[1]
    zmix = (gates[:, :d] * y_fft + gates[:, d:] * y_gla).astype(BF16)
    y = jnp.dot(zmix, wo_ref[...], preferred_element_type=F32)
    o_ref[...] = _add_pos(x_ref[...], rt_ref, ct_ref) + gt_ref[...] * y


def _const_spec(shape):
    zeros = (0,) * len(shape)
    return pl.BlockSpec(shape, lambda *idx: zeros, pipeline_mode=pl.Buffered(1))


def _merge(o_f, o_b, p2, xr, xi, x2, rt3, ct, mod3, gng, wgo, wfo, wo, csg, t):
    ntok, d = x2.shape
    tm = 256
    tiles_per_seq = t // tm
    rpt = tm // GRID_W
    row = lambda blk: (lambda i: (i, blk))
    in_specs = [pl.BlockSpec((tm, VAL_W), row(0)),
                pl.BlockSpec((tm, VAL_W), row(0)),
                pl.BlockSpec((tm, VAL_W), row(2)),
                pl.BlockSpec((tm, 2 * d), row(1)),
                pl.BlockSpec((tm, FWIDTH), row(0)),
                pl.BlockSpec((tm, FWIDTH), row(0)),
                pl.BlockSpec((tm, d), row(0)),
                pl.BlockSpec((rpt, 1, d // 2), lambda i: (i % tiles_per_seq, 0, 0)),
                _const_spec((GRID_W, d // 2)),
                pl.BlockSpec((None, 1, d), lambda i: (i // tiles_per_seq, 0, 2)),
                _const_spec(gng.shape), _const_spec(wgo.shape), _const_spec(wfo.shape),
                _const_spec(wo.shape), _const_spec(csg.shape)]
    return pl.pallas_call(
        _merge_kernel,
        out_shape=jax.ShapeDtypeStruct((ntok, d), F32),
        grid=(ntok // tm,),
        in_specs=in_specs,
        out_specs=pl.BlockSpec((tm, d), row(0)),
        compiler_params=_cparams(("arbitrary",)),
        name="merge",
    )(o_f, o_b, p2, p2, xr, xi, x2, rt3, ct, mod3, gng, wgo, wfo, wo, csg)


def _mlp_kernel(x_ref, xn_ref, sh_ref, sc_ref, shn_ref, scn_ref, gt_ref, g2_ref, w1_ref, w2_ref, fg_ref,
                o_ref, h_even, h_odd):
    i, j = pl.program_id(0), pl.program_id(1)
    nf = pl.num_programs(1)
    tm = x_ref.shape[0]
    slab = tm // nf
    slot = i % 2

    def normed(x, sh, sc):
        return (_rms(x, g2_ref[...]) * (1.0 + sc) + sh).astype(BF16)

    @pl.when((i == 0) & (j == 0))
    def _():
        h_even[...] = normed(x_ref[...], sh_ref[...], sc_ref[...])

    @pl.when(j == 0)
    def _():
        o_ref[...] = jnp.zeros_like(o_ref)

    def step(h_cur, h_next):
        r0 = pl.multiple_of(j * slab, slab)
        h_next[pl.ds(r0, slab), :] = normed(xn_ref[pl.ds(r0, slab), :], shn_ref[...], scn_ref[...])
        hid = jnp.dot(h_cur[...], w1_ref[...], preferred_element_type=F32)
        hid = jnp.square(jnp.maximum(hid, 0.0)).astype(BF16)
        o_ref[...] += jnp.dot(hid, w2_ref[...], preferred_element_type=F32)

    @pl.when(slot == 0)
    def _():
        step(h_even, h_odd)

    @pl.when(slot == 1)
    def _():
        step(h_odd, h_even)

    @pl.when(j == nf - 1)
    def _():
        xo = x_ref[...] + gt_ref[...] * o_ref[...]
        o_ref[...] = _rms(xo, fg_ref[...])


def _mlp(x2, mod3, g2, w1, w2, fg, t):
    ntok, d = x2.shape
    dff = w1.shape[1]
    tm, tf = 512, 1024
    assert (tm // (dff // tf)) % 8 == 0
    nt = ntok // tm
    tiles_per_seq = t // tm
    nxt = lambda i: jnp.minimum(i + 1, nt - 1)
    modspec = lambda blk: pl.BlockSpec((None, 1, d), lambda i, j: (i // tiles_per_seq, 0, blk))
    modspec_next = lambda blk: pl.BlockSpec((None, 1, d), lambda i, j: (nxt(i) // tiles_per_seq, 0, blk))
    return pl.pallas_call(
        _mlp_kernel,
        out_shape=jax.ShapeDtypeStruct((ntok, d), F32),
        grid=(nt, dff // tf),
        in_specs=[pl.BlockSpec((tm, d), lambda i, j: (i, 0)),
                  pl.BlockSpec((tm, d), lambda i, j: (nxt(i), 0)),
                  modspec(3), modspec(4), modspec_next(3), modspec_next(4), modspec(5),
                  pl.BlockSpec((1, d), lambda i, j: (0, 0)),
                  pl.BlockSpec((d, tf), lambda i, j: (0, j)),
                  pl.BlockSpec((tf, d), lambda i, j: (j, 0)),
                  pl.BlockSpec((1, d), lambda i, j: (0, 0))],
        out_specs=pl.BlockSpec((tm, d), lambda i, j: (i, 0)),
        scratch_shapes=[pltpu.VMEM((tm, d), BF16), pltpu.VMEM((tm, d), BF16)],
        compiler_params=_cparams(("arbitrary", "arbitrary")),
        name="mlp",
    )(x2, x2, mod3, mod3, mod3, mod3, mod3, g2, w1, w2, fg)


def _pos_tables(t, d):
    quarter = d // 4
    omega = 1.0 / (POS_TEMP ** (jnp.arange(quarter, dtype=F32) / quarter))
    er = jnp.arange(t // GRID_W, dtype=F32)[:, None] * omega[None, :]
    ec = jnp.arange(GRID_W, dtype=F32)[:, None] * omega[None, :]
    rt = jnp.concatenate([jnp.sin(er), jnp.cos(er)], axis=-1)
    ct = jnp.concatenate([jnp.sin(ec), jnp.cos(ec)], axis=-1)
    return rt[:, None, :], ct


def _dft_tables(t):
    m = t // RADIX
    tile = min(256, m)
    k = np.arange(m)
    ang = 2.0 * np.pi * ((k[:, None] * k[None, :]) % m) / m
    cos_t = np.cos(ang).reshape(m // tile, tile, m)
    sin_t = np.sin(ang).reshape(m // tile, tile, m)
    fmat = np.concatenate([cos_t, -sin_t], axis=1)
    a = np.arange(RADIX)
    tw = 2.0 * np.pi * (a[:, None] * k[None, :]) / t
    scale = 1.0 / math.sqrt(t * FGDIM)
    twc = (np.cos(tw) * scale)[:, :, None]
    tws = (np.sin(tw) * scale)[:, :, None]
    c = np.arange(FGDIM)
    cang = 2.0 * np.pi * ((c[:, None] * c[None, :]) % FGDIM) / FGDIM
    csg = np.concatenate([np.cos(cang), np.sin(cang)], axis=0)
    as_f32 = lambda a: jnp.asarray(a.astype(np.float32))
    return as_f32(fmat).astype(BF16), as_f32(twc), as_f32(tws), as_f32(csg).astype(BF16)


def _pad_lr_weight(w_lr, row0):
    out = jnp.zeros((LR_PAD, KEY_W), F32)
    return out.at[row0:row0 + RANK].set(w_lr).astype(BF16)


def kernel(x, c, ctx, c_ctx, w_mod, b_mod, norm1_g, norm2_g, w_in, w_lr_f, b_lr_f, w_lr_b, b_lr_b,
           gla_norm_g, w_fourier_out, w_gla_out, w_out, w_mlp_in, w_mlp_out, final_norm_g):
    b, t, d = x.shape
    tc = ctx.shape[1]
    depth = w_mod.shape[0]
    assert depth == 1 and d == D_MODEL and t % (RADIX * GRID_W) == 0 and tc % GLA_CHUNK == 0
    li = 0

    lr0 = QKVG_W
    u0 = lr0 + 2 * RANK
    wi = w_in[li]
    w_main = jnp.concatenate([wi[:, :lr0], wi[:, u0:]], axis=1).astype(BF16)
    w_lr = jnp.pad(wi[:, lr0:u0], ((0, 0), (0, LR_PAD - 2 * RANK))).astype(BF16)
    wlr_f, wlr_b = _pad_lr_weight(w_lr_f[li], 0), _pad_lr_weight(w_lr_b[li], RANK)
    blr_f, blr_b = b_lr_f[li][None, :], b_lr_b[li][None, :]
    wgo, wfo, wo = w_gla_out[li].astype(BF16), w_fourier_out[li].astype(BF16), w_out[li].astype(BF16)
    w1, w2 = w_mlp_in[li].astype(BF16), w_mlp_out[li].astype(BF16)

    rt3, ct = _pos_tables(t, d)
    fmat, twc, tws, csg = _dft_tables(t)

    rows = 8
    cpad = jnp.concatenate([c, c_ctx[None, :], jnp.zeros((rows - b - 1, d), F32)], axis=0)
    mod3 = _mod(cpad, w_mod[li], b_mod[li][None, :]).reshape(rows, 1, N_MOD * d)

    tm_ctx = min(512, b * tc)
    qkv_w = 2 * KEY_W + VAL_W
    p_ctx, lr_ctx = _inproj(ctx.reshape(b * tc, d), mod3, lambda i: b, norm1_g[li][None, :],
                            w_main, w_lr, None, tm_ctx, qkv_w)
    s_zero = jnp.zeros((b, HEADS, DK, DV), F32)
    s_f, s_b = _gla(p_ctx.reshape(b, tc, qkv_w), lr_ctx.reshape(b, tc, LR_PAD),
                    wlr_f, blr_f, wlr_b, blr_b, s_zero, s_zero, emit_o=False)

    tm = min(1024, t)
    tiles = t // tm
    x2 = x.reshape(b * t, d)
    p, lr = _inproj(x2, mod3, lambda i: i // tiles, norm1_g[li][None, :], w_main, w_lr,
                    (rt3, ct, tiles), tm, P_WIDTH)
    p3 = p.reshape(b, t, P_WIDTH)
    o_f, o_b, _, _ = _gla(p3, lr.reshape(b, t, LR_PAD), wlr_f, blr_f, wlr_b, blr_b, s_f, s_b, emit_o=True)

    xr, xi = _fft(p3, QKVG_W, fmat, twc, tws)

    x1 = _merge(o_f.reshape(b * t, VAL_W), o_b.reshape(b * t, VAL_W), p,
                xr.reshape(b * t, FWIDTH), xi.reshape(b * t, FWIDTH), x2, rt3, ct, mod3,
                gla_norm_g[li][None, :], wgo, wfo, wo, csg, t)

    out = _mlp(x1, mod3, norm2_g[li][None, :], w1, w2, final_norm_g[None, :], t)
    return out.reshape(b, t, d)
```

```python
import functools
import math

import jax
import jax.numpy as jnp
import numpy as np
from jax import lax
from jax.experimental import pallas as pl
from jax.experimental.pallas import tpu as pltpu

F32 = jnp.float32
BF16 = jnp.bfloat16

D_MODEL = 2048
GRID_W = 64
HEADS = 4
DK = 128
DV = 256
KEY_W = HEADS * DK
VAL_W = HEADS * DV
RANK = 16
TAU = 16.0
FGROUPS = 4
FGDIM = 256
FWIDTH = FGROUPS * FGDIM
D_FF = 4 * D_MODEL
N_MOD = 6
EPS = 1e-6
POS_TEMP = 10000.0

QKVG_W = 2 * KEY_W + 2 * VAL_W
P_WIDTH = QKVG_W + FWIDTH + 2 * D_MODEL
LR_PAD = 128
GLA_CHUNK = 128
GLA_CHUNKS_PER_STEP = 4
RADIX = 8
LANES = 128

V7X_VMEM_LIMIT = 56 * 1024 * 1024


def _cparams(sem, vmem=V7X_VMEM_LIMIT):
    return pltpu.CompilerParams(dimension_semantics=sem, vmem_limit_bytes=vmem)


def _silu(x):
    return x * jax.nn.sigmoid(x)


def _mod_kernel(ct_ref, w_ref, b_ref, o_ref, *, n_used):
    st = _silu(ct_ref[...])
    w = w_ref[...]
    bias = b_ref[...]
    rows = [jnp.sum(w * st[:, m:m + 1], axis=0, keepdims=True) + bias for m in range(n_used)]
    rows += [bias] * (o_ref.shape[0] - n_used)
    o_ref[...] = jnp.concatenate(rows, axis=0)


def _mod(cpad_t, n_used, w_mod, b_mod):
    d, rows = cpad_t.shape
    n = w_mod.shape[1]
    tn = 1024
    return pl.pallas_call(
        functools.partial(_mod_kernel, n_used=n_used),
        out_shape=jax.ShapeDtypeStruct((rows, n), F32),
        grid=(n // tn,),
        in_specs=[pl.BlockSpec((d, rows), lambda j: (0, 0)),
                  pl.BlockSpec((d, tn), lambda j: (0, j)),
                  pl.BlockSpec((1, tn), lambda j: (0, j))],
        out_specs=pl.BlockSpec((rows, tn), lambda j: (0, j)),
        compiler_params=_cparams(("arbitrary",)),
        name="mod",
    )(cpad_t, w_mod, b_mod)


def _wprep_kernel(w_ref, wm_ref, wlr_ref):
    x = w_ref[...]
    wm_ref[:, :QKVG_W] = x[:, :QKVG_W].astype(BF16)
    wm_ref[:, QKVG_W:] = x[:, QKVG_W + 2 * RANK:].astype(BF16)
    lane = lax.broadcasted_iota(jnp.int32, (x.shape[0], LR_PAD), 1)
    wlr_ref[...] = jnp.where(lane < 2 * RANK, x[:, QKVG_W:QKVG_W + LR_PAD], 0.0).astype(BF16)


def _wprep(w_in):
    d, n = w_in.shape
    tr = 256
    return pl.pallas_call(
        _wprep_kernel,
        out_shape=(jax.ShapeDtypeStruct((d, P_WIDTH), BF16), jax.ShapeDtypeStruct((d, LR_PAD), BF16)),
        grid=(d // tr,),
        in_specs=[pl.BlockSpec((tr, n), lambda i: (i, 0))],
        out_specs=(pl.BlockSpec((tr, P_WIDTH), lambda i: (i, 0)), pl.BlockSpec((tr, LR_PAD), lambda i: (i, 0))),
        compiler_params=_cparams(("arbitrary",)),
        name="wprep",
    )(w_in)


def _add_pos(x, rt_ref, ct_ref):
    tm, d = x.shape
    x3 = x.reshape(tm // GRID_W, GRID_W, d)
    half = d // 2
    lo = x3[:, :, :half] + rt_ref[...]
    hi = x3[:, :, half:] + ct_ref[...][None]
    return jnp.concatenate([lo, hi], axis=-1).reshape(tm, d)


def _rms(x, g):
    return x * lax.rsqrt(jnp.mean(x * x, axis=-1, keepdims=True) + EPS) * g


def _inproj_kernel(*refs, add_pos):
    if add_pos:
        x_ref, rt_ref, ct_ref, sh_ref, sc_ref, g_ref, w_ref, wlr_ref, p_ref, lr_ref, h_scr = refs
    else:
        x_ref, sh_ref, sc_ref, g_ref, w_ref, wlr_ref, p_ref, lr_ref, h_scr = refs

    @pl.when(pl.program_id(1) == 0)
    def _():
        x = x_ref[...]
        if add_pos:
            x = _add_pos(x, rt_ref, ct_ref)
        h = _rms(x, g_ref[...]) * (1.0 + sc_ref[...]) + sh_ref[...]
        hb = h.astype(BF16)
        h_scr[...] = hb
        lr_ref[...] = jnp.dot(hb, wlr_ref[...], preferred_element_type=F32)

    p_ref[...] = jnp.dot(h_scr[...], w_ref[...], preferred_element_type=F32).astype(BF16)


def _inproj(x2, mod3, mod_row_of_tile, norm_g, w_main, w_lr, pos_tabs, tm, width):
    ntok, d = x2.shape
    tn = 1024
    add_pos = pos_tabs is not None
    in_specs = [pl.BlockSpec((tm, d), lambda i, j: (i, 0))]
    args = [x2]
    if add_pos:
        rt3, ct, tiles_per_seq = pos_tabs
        rpt = tm // GRID_W
        in_specs += [pl.BlockSpec((rpt, 1, d // 2), lambda i, j: (i % tiles_per_seq, 0, 0)),
                     pl.BlockSpec((GRID_W, d // 2), lambda i, j: (0, 0))]
        args += [rt3, ct]
    in_specs += [pl.BlockSpec((None, 1, d), lambda i, j: (mod_row_of_tile(i), 0, 0)),
                 pl.BlockSpec((None, 1, d), lambda i, j: (mod_row_of_tile(i), 0, 1)),
                 pl.BlockSpec((1, d), lambda i, j: (0, 0)),
                 pl.BlockSpec((d, tn), lambda i, j: (0, j)),
                 pl.BlockSpec((d, LR_PAD), lambda i, j: (0, 0))]
    args += [mod3, mod3, norm_g, w_main, w_lr]
    return pl.pallas_call(
        functools.partial(_inproj_kernel, add_pos=add_pos),
        out_shape=(jax.ShapeDtypeStruct((ntok, width), BF16),
                   jax.ShapeDtypeStruct((ntok, LR_PAD), F32)),
        grid=(ntok // tm, width // tn),
        in_specs=in_specs,
        out_specs=(pl.BlockSpec((tm, tn), lambda i, j: (i, j)),
                   pl.BlockSpec((tm, LR_PAD), lambda i, j: (i, 0))),
        scratch_shapes=[pltpu.VMEM((tm, d), BF16)],
        compiler_params=_cparams(("arbitrary", "arbitrary")),
        name="inproj_pos" if add_pos else "inproj_ctx",
    )(*args)


def _log_sigmoid(z):
    return jnp.minimum(z, 0.0) - jnp.log1p(jnp.exp(-jnp.abs(z)))


def _gla_chunks(dirs, c):
    n_sub = dirs[0][0].shape[0] // c
    row = lax.broadcasted_iota(jnp.int32, (c, c), 0)
    col = lax.broadcasted_iota(jnp.int32, (c, c), 1)

    chains = {}
    for di, (qk_ref, v_ref, lr_ref, wlr_ref, blr_ref, s_scr, o_ref, backward) in enumerate(dirs):
        keep = (row <= col) if backward else (row >= col)
        tri = jnp.where(keep, 1.0, 0.0).astype(BF16)
        last = 0 if backward else c - 1
        mid = c // 2 if backward else c // 2 - 1
        order = range(n_sub - 1, -1, -1) if backward else range(n_sub)
        for step, sub in enumerate(order):
            rows = slice(sub * c, (sub + 1) * c)
            z = jnp.dot(lr_ref[rows, :].astype(BF16), wlr_ref[...], preferred_element_type=F32) + blr_ref[...]
            la = _log_sigmoid(z) * (1.0 / TAU)
            la_hi = la.astype(BF16)
            la_lo = (la - la_hi.astype(F32)).astype(BF16)
            cum2 = jnp.dot(tri, jnp.concatenate([la_hi, la_lo], axis=1), preferred_element_type=F32)
            cum_all = cum2[:, :KEY_W] + cum2[:, KEY_W:]
            for h in range(HEADS):
                cum = cum_all[:, h * DK:(h + 1) * DK]
                tot = cum[last:last + 1, :]
                ref_pt = cum[mid:mid + 1, :]
                q = qk_ref[rows, h * DK:(h + 1) * DK].astype(F32) * (DK ** -0.5)
                k = qk_ref[rows, KEY_W + h * DK:KEY_W + (h + 1) * DK].astype(F32)
                chains[(step, di, h)] = dict(
                    keep=keep, rows=rows,
                    v=v_ref[rows, h * DV:(h + 1) * DV],
                    q_mid=(q * jnp.exp(cum - ref_pt)).astype(BF16),
                    k_mid=(k * jnp.exp(ref_pt - cum)).astype(BF16),
                    q_dec=(q * jnp.exp(cum)).astype(BF16),
                    k_end_t=(k * jnp.exp(tot - cum)).T.astype(BF16),
                    dec_col=jnp.broadcast_to(jnp.exp(tot), (DK, DK)).T)

    for ch in chains.values():
        s = lax.dot_general(ch["q_mid"], ch["k_mid"], (((1,), (1,)), ((), ())), preferred_element_type=F32)
        ch["scores"] = jnp.where(ch["keep"], s, 0.0).astype(BF16)
    for ch in chains.values():
        ch["kv"] = jnp.dot(ch["k_end_t"], ch["v"], preferred_element_type=F32)

    state = {(di, h): d[5][h] for di, d in enumerate(dirs) for h in range(HEADS)}
    for step in range(n_sub):
        for di, d in enumerate(dirs):
            o_ref = d[6]
            for h in range(HEADS):
                ch = chains[(step, di, h)]
                s_prev = state[(di, h)]
                lhs = jnp.concatenate([ch["scores"], ch["q_dec"]], axis=1)
                rhs = jnp.concatenate([ch["v"], s_prev.astype(BF16)], axis=0)
                o = jnp.dot(lhs, rhs, preferred_element_type=F32)
                if o_ref is not None:
                    o_ref[ch["rows"], h * DV:(h + 1) * DV] = o.astype(o_ref.dtype)
                dec = jnp.concatenate([ch["dec_col"]] * (DV // DK), axis=1)
                state[(di, h)] = s_prev * dec + ch["kv"]
    for (di, h), s in state.items():
        dirs[di][5][h] = s


def _gla_kernel(*refs, emit_o):
    (qkf, vf, lrf, qkb, vb, lrb, wf, bf, wb, bb, s0f, s0b) = refs[:12]
    if emit_o:
        of, ob, sf_out, sb_out, s_scr = refs[12:]
    else:
        sf_out, sb_out, s_scr = refs[12:]
        of = ob = None
    i = pl.program_id(1)

    @pl.when(i == 0)
    def _():
        s_scr[0] = s0f[...]
        s_scr[1] = s0b[...]

    _gla_chunks([(qkf, vf, lrf, wf, bf, s_scr.at[0], of, False),
                 (qkb, vb, lrb, wb, bb, s_scr.at[1], ob, True)], GLA_CHUNK)

    @pl.when(i == pl.num_programs(1) - 1)
    def _():
        sf_out[...] = s_scr[0]
        sb_out[...] = s_scr[1]


def _gla(p3, lr3, wlr_f, blr_f, wlr_b, blr_b, s0f, s0b, emit_o):
    b, t, _ = p3.shape
    per_step = max(s for s in range(1, GLA_CHUNKS_PER_STEP + 1) if t % (GLA_CHUNK * s) == 0)
    c = GLA_CHUNK * per_step
    n = t // c
    fwd = lambda blk: (lambda bi, i: (bi, i, blk))
    bwd = lambda blk: (lambda bi, i: (bi, n - 1 - i, blk))

    def seq_specs(mk):
        return [pl.BlockSpec((None, c, 2 * KEY_W), mk(0)),
                pl.BlockSpec((None, c, VAL_W), mk(1)),
                pl.BlockSpec((None, c, LR_PAD), mk(0))]
    full2 = lambda shape: pl.BlockSpec(shape, lambda bi, i: (0, 0))
    st_spec = pl.BlockSpec((None, HEADS, DK, DV), lambda bi, i: (bi, 0, 0, 0))
    in_specs = (seq_specs(fwd) + seq_specs(bwd)
                + [full2(wlr_f.shape), full2(blr_f.shape), full2(wlr_b.shape), full2(blr_b.shape),
                   st_spec, st_spec])
    st_shape = jax.ShapeDtypeStruct((b, HEADS, DK, DV), F32)
    out_shape = [st_shape, st_shape]
    out_specs = [st_spec, st_spec]
    if emit_o:
        o_shape = jax.ShapeDtypeStruct((b, t, VAL_W), BF16)
        out_shape = [o_shape, o_shape] + out_shape
        out_specs = [pl.BlockSpec((None, c, VAL_W), fwd(0)), pl.BlockSpec((None, c, VAL_W), bwd(0))] + out_specs
    return pl.pallas_call(
        functools.partial(_gla_kernel, emit_o=emit_o),
        out_shape=tuple(out_shape),
        grid=(b, n),
        in_specs=in_specs,
        out_specs=tuple(out_specs),
        scratch_shapes=[pltpu.VMEM((2, HEADS, DK, DV), F32)],
        compiler_params=_cparams(("arbitrary", "arbitrary")),
        name="gla_seq" if emit_o else "gla_ctx",
    )(p3, p3, lr3, p3, p3, lr3, wlr_f, blr_f, wlr_b, blr_b, s0f, s0b)


def _cadd(a, b):
    return a[0] + b[0], a[1] + b[1]


def _csub(a, b):
    return a[0] - b[0], a[1] - b[1]


def _cmul_neg_i(a):
    return a[1], -a[0]


def _dft4(y):
    t0, t1 = _cadd(y[0], y[2]), _csub(y[0], y[2])
    t2, t3 = _cadd(y[1], y[3]), _cmul_neg_i(_csub(y[1], y[3]))
    return [_cadd(t0, t2), _cadd(t1, t3), _csub(t0, t2), _csub(t1, t3)]


def _dft8(z):
    r = math.sqrt(0.5)
    s = [_cadd(z[a], z[a + 4]) for a in range(4)]
    d = [_csub(z[a], z[a + 4]) for a in range(4)]
    d1 = ((d[1][0] + d[1][1]) * r, (d[1][1] - d[1][0]) * r)
    d2 = _cmul_neg_i(d[2])
    d3 = ((d[3][1] - d[3][0]) * r, (-d[3][1] - d[3][0]) * r)
    ev = _dft4(s)
    od = _dft4([d[0], d1, d2, d3])
    out = [None] * 8
    for j in range(4):
        out[2 * j] = ev[j]
        out[2 * j + 1] = od[j]
    return out


def _fft_kernel(*refs, n_slabs):
    u_refs = refs[:n_slabs]
    f_ref, twc_ref, tws_ref, xr_ref, xi_ref, wide_scr, ub_scr = refs[n_slabs:]
    m = ub_scr.shape[1]

    @pl.when(pl.program_id(2) == 0)
    def _():
        for s in range(n_slabs):
            wide_scr[s] = u_refs[s][...].astype(F32)
        for a in range(RADIX):
            rows = [wide_scr[s, pl.ds(a, m, stride=RADIX), :].astype(BF16) for s in range(n_slabs)]
            ub_scr[a] = jnp.concatenate(rows, axis=1)

    t = f_ref.shape[0] // 2
    f = f_ref[...]
    z = []
    for a in range(RADIX):
        za = jnp.dot(f, ub_scr[a], preferred_element_type=F32)
        zr, zi = za[:t], za[t:]
        cc, ss = twc_ref[a], tws_ref[a]
        z.append((zr * cc + zi * ss, zi * cc - zr * ss))
    x = _dft8(z)
    for k1 in range(RADIX):
        xr_ref[k1] = x[k1][0].astype(xr_ref.dtype)
        xi_ref[k1] = x[k1][1].astype(xi_ref.dtype)


def _fft(p3, col0, fmat, twc, tws):
    b, seq, _ = p3.shape
    w = FWIDTH
    m = seq // RADIX
    nt, t2, _ = fmat.shape
    t = t2 // 2
    chb = 256
    n_slabs = chb // LANES
    slab0 = col0 // LANES
    out = jax.ShapeDtypeStruct((b, RADIX, m, w), BF16)
    o_spec = pl.BlockSpec((None, RADIX, t, chb), lambda bi, cj, kt: (bi, 0, kt, cj))
    slab = lambda s: pl.BlockSpec((None, seq, LANES), lambda bi, cj, kt: (bi, 0, slab0 + cj * n_slabs + s))
    return pl.pallas_call(
        functools.partial(_fft_kernel, n_slabs=n_slabs),
        out_shape=(out, out),
        grid=(b, w // chb, nt),
        in_specs=[slab(s) for s in range(n_slabs)] + [
                  pl.BlockSpec((None, t2, m), lambda bi, cj, kt: (kt, 0, 0)),
                  pl.BlockSpec((RADIX, t, 1), lambda bi, cj, kt: (0, kt, 0)),
                  pl.BlockSpec((RADIX, t, 1), lambda bi, cj, kt: (0, kt, 0))],
        out_specs=(o_spec, o_spec),
        scratch_shapes=[pltpu.VMEM((n_slabs, seq, LANES), F32), pltpu.VMEM((RADIX, m, chb), BF16)],
        compiler_params=_cparams(("arbitrary", "arbitrary", "arbitrary")),
        name="fft",
    )(*([p3] * n_slabs), fmat, twc, tws)


def _merge_kernel(of_ref, ob_ref, g_ref, gates_ref, xr_ref, xi_ref, x_ref, rt_ref, ct_ref, gt_ref,
                  gng_ref, wgo_ref, wfo_ref, wo_ref, csg_ref, o_ref):
    o = of_ref[...].astype(F32) + ob_ref[...].astype(F32)
    g = g_ref[...].astype(F32)
    gng = gng_ref[...]
    heads = []
    for h in range(HEADS):
        sl = slice(h * DV, (h + 1) * DV)
        heads.append((_rms(o[:, sl], gng) * _silu(g[:, sl])).astype(BF16))
    y_gla = jnp.dot(jnp.concatenate(heads, axis=1), wgo_ref[...], preferred_element_type=F32)

    csg = csg_ref[...]
    groups = []
    for gi in range(FGROUPS):
        sl = slice(gi * FGDIM, (gi + 1) * FGDIM)
        xg = jnp.concatenate([xr_ref[:, sl], xi_ref[:, sl]], axis=1)
        groups.append(jnp.dot(xg, csg, preferred_element_type=F32).astype(BF16))
    y_fft = jnp.dot(jnp.concatenate(groups, axis=1), wfo_ref[...], preferred_element_type=F32)

    gates = jax.nn.sigmoid(gates_ref[...].astype(F32))
    d = y_fft.shape[1]
    zmix = (gates[:, :d] * y_fft + gates[:, d:] * y_gla).astype(BF16)
    y = jnp.dot(zmix, wo_ref[...], preferred_element_type=F32)
    o_ref[...] = _add_pos(x_ref[...], rt_ref, ct_ref) + gt_ref[...] * y


def _const_spec(shape):
    zeros = (0,) * len(shape)
    return pl.BlockSpec(shape, lambda *idx: zeros, pipeline_mode=pl.Buffered(1))


def _merge(o_f, o_b, p2, xr, xi, x2, rt3, ct, mod3, gng, wgo, wfo, wo, csg, t):
    ntok, d = x2.shape
    tm = 256
    tiles_per_seq = t // tm
    rpt = tm // GRID_W
    row = lambda blk: (lambda i: (i, blk))
    in_specs = [pl.BlockSpec((tm, VAL_W), row(0)),
                pl.BlockSpec((tm, VAL_W), row(0)),
                pl.BlockSpec((tm, VAL_W), row(2)),
                pl.BlockSpec((tm, 2 * d), row(1)),
                pl.BlockSpec((tm, FWIDTH), row(0)),
                pl.BlockSpec((tm, FWIDTH), row(0)),
                pl.BlockSpec((tm, d), row(0)),
                pl.BlockSpec((rpt, 1, d // 2), lambda i: (i % tiles_per_seq, 0, 0)),
                _const_spec((GRID_W, d // 2)),
                pl.BlockSpec((None, 1, d), lambda i: (i // tiles_per_seq, 0, 2)),
                _const_spec(gng.shape), _const_spec(wgo.shape), _const_spec(wfo.shape),
                _const_spec(wo.shape), _const_spec(csg.shape)]
    return pl.pallas_call(
        _merge_kernel,
        out_shape=jax.ShapeDtypeStruct((ntok, d), F32),
        grid=(ntok // tm,),
        in_specs=in_specs,
        out_specs=pl.BlockSpec((tm, d), row(0)),
        compiler_params=_cparams(("arbitrary",)),
        name="merge",
    )(o_f, o_b, p2, p2, xr, xi, x2, rt3, ct, mod3, gng, wgo, wfo, wo, csg)


def _mlp_kernel(x_ref, xn_ref, sh_ref, sc_ref, shn_ref, scn_ref, gt_ref, g2_ref, w1_ref, w2_ref, fg_ref,
                o_ref, h_even, h_odd):
    i, j = pl.program_id(0), pl.program_id(1)
    nf = pl.num_programs(1)
    tm = x_ref.shape[0]
    slab = tm // nf
    slot = i % 2

    def normed(x, sh, sc):
        return (_rms(x, g2_ref[...]) * (1.0 + sc) + sh).astype(BF16)

    @pl.when((i == 0) & (j == 0))
    def _():
        h_even[...] = normed(x_ref[...], sh_ref[...], sc_ref[...])

    @pl.when(j == 0)
    def _():
        o_ref[...] = jnp.zeros_like(o_ref)

    def step(h_cur, h_next):
        r0 = pl.multiple_of(j * slab, slab)
        h_next[pl.ds(r0, slab), :] = normed(xn_ref[pl.ds(r0, slab), :], shn_ref[...], scn_ref[...])
        hid = jnp.dot(h_cur[...], w1_ref[...], preferred_element_type=F32)
        hid = jnp.square(jnp.maximum(hid, 0.0)).astype(BF16)
        o_ref[...] += jnp.dot(hid, w2_ref[...], preferred_element_type=F32)

    @pl.when(slot == 0)
    def _():
        step(h_even, h_odd)

    @pl.when(slot == 1)
    def _():
        step(h_odd, h_even)

    @pl.when(j == nf - 1)
    def _():
        xo = x_ref[...] + gt_ref[...] * o_ref[...]
        o_ref[...] = _rms(xo, fg_ref[...])


def _mlp(x2, mod3, g2, w1, w2, fg, t):
    ntok, d = x2.shape
    dff = w1.shape[1]
    tm, tf = 512, 1024
    assert (tm // (dff // tf)) % 8 == 0
    nt = ntok // tm
    tiles_per_seq = t // tm
    nxt = lambda i: jnp.minimum(i + 1, nt - 1)
    modspec = lambda blk: pl.BlockSpec((None, 1, d), lambda i, j: (i // tiles_per_seq, 0, blk))
    modspec_next = lambda blk: pl.BlockSpec((None, 1, d), lambda i, j: (nxt(i) // tiles_per_seq, 0, blk))
    return pl.pallas_call(
        _mlp_kernel,
        out_shape=jax.ShapeDtypeStruct((ntok, d), F32),
        grid=(nt, dff // tf),
        in_specs=[pl.BlockSpec((tm, d), lambda i, j: (i, 0)),
                  pl.BlockSpec((tm, d), lambda i, j: (nxt(i), 0)),
                  modspec(3), modspec(4), modspec_next(3), modspec_next(4), modspec(5),
                  pl.BlockSpec((1, d), lambda i, j: (0, 0)),
                  pl.BlockSpec((d, tf), lambda i, j: (0, j)),
                  pl.BlockSpec((tf, d), lambda i, j: (j, 0)),
                  pl.BlockSpec((1, d), lambda i, j: (0, 0))],
        out_specs=pl.BlockSpec((tm, d), lambda i, j: (i, 0)),
        scratch_shapes=[pltpu.VMEM((tm, d), BF16), pltpu.VMEM((tm, d), BF16)],
        compiler_params=_cparams(("arbitrary", "arbitrary")),
        name="mlp",
    )(x2, x2, mod3, mod3, mod3, mod3, mod3, g2, w1, w2, fg)


def _pos_tables(t, d):
    quarter = d // 4
    omega = 1.0 / (POS_TEMP ** (jnp.arange(quarter, dtype=F32) / quarter))
    er = jnp.arange(t // GRID_W, dtype=F32)[:, None] * omega[None, :]
    ec = jnp.arange(GRID_W, dtype=F32)[:, None] * omega[None, :]
    rt = jnp.concatenate([jnp.sin(er), jnp.cos(er)], axis=-1)
    ct = jnp.concatenate([jnp.sin(ec), jnp.cos(ec)], axis=-1)
    return rt[:, None, :], ct


def _dft_tables(t):
    m = t // RADIX
    tile = min(256, m)
    k = np.arange(m)
    ang = 2.0 * np.pi * ((k[:, None] * k[None, :]) % m) / m
    cos_t = np.cos(ang).reshape(m // tile, tile, m)
    sin_t = np.sin(ang).reshape(m // tile, tile, m)
    fmat = np.concatenate([cos_t, -sin_t], axis=1)
    a = np.arange(RADIX)
    tw = 2.0 * np.pi * (a[:, None] * k[None, :]) / t
    scale = 1.0 / math.sqrt(t * FGDIM)
    twc = (np.cos(tw) * scale)[:, :, None]
    tws = (np.sin(tw) * scale)[:, :, None]
    c = np.arange(FGDIM)
    cang = 2.0 * np.pi * ((c[:, None] * c[None, :]) % FGDIM) / FGDIM
    csg = np.concatenate([np.cos(cang), np.sin(cang)], axis=0)
    as_f32 = lambda a: jnp.asarray(a.astype(np.float32))
    return as_f32(fmat).astype(BF16), as_f32(twc), as_f32(tws), as_f32(csg).astype(BF16)


def _pad_lr_weight(w_lr, row0):
    out = jnp.zeros((LR_PAD, KEY_W), F32)
    return out.at[row0:row0 + RANK].set(w_lr).astype(BF16)


def kernel(x, c, ctx, c_ctx, w_mod, b_mod, norm1_g, norm2_g, w_in, w_lr_f, b_lr_f, w_lr_b, b_lr_b,
           gla_norm_g, w_fourier_out, w_gla_out, w_out, w_mlp_in, w_mlp_out, final_norm_g):
    b, t, d = x.shape
    tc = ctx.shape[1]
    depth = w_mod.shape[0]
    assert depth == 1 and d == D_MODEL and t % (RADIX * GRID_W) == 0 and tc % GLA_CHUNK == 0
    li = 0

    w_main, w_lr = _wprep(w_in[li])
    wlr_f, wlr_b = _pad_lr_weight(w_lr_f[li], 0), _pad_lr_weight(w_lr_b[li], RANK)
    blr_f, blr_b = b_lr_f[li][None, :], b_lr_b[li][None, :]
    wgo, wfo, wo = w_gla_out[li].astype(BF16), w_fourier_out[li].astype(BF16), w_out[li].astype(BF16)
    w1, w2 = w_mlp_in[li].astype(BF16), w_mlp_out[li].astype(BF16)

    rt3, ct = _pos_tables(t, d)
    fmat, twc, tws, csg = _dft_tables(t)

    rows = 8
    cpad = jnp.concatenate([c, c_ctx[None, :], jnp.zeros((rows - b - 1, d), F32)], axis=0)
    mod3 = _mod(cpad.T, b + 1, w_mod[li], b_mod[li][None, :]).reshape(rows, 1, N_MOD * d)

    tm_ctx = min(512, b * tc)
    qkv_w = 2 * KEY_W + VAL_W
    p_ctx, lr_ctx = _inproj(ctx.reshape(b * tc, d), mod3, lambda i: b, norm1_g[li][None, :],
                            w_main, w_lr, None, tm_ctx, qkv_w)
    s_zero = jnp.zeros((b, HEADS, DK, DV), F32)
    s_f, s_b = _gla(p_ctx.reshape(b, tc, qkv_w), lr_ctx.reshape(b, tc, LR_PAD),
                    wlr_f, blr_f, wlr_b, blr_b, s_zero, s_zero, emit_o=False)

    tm = min(1024, t)
    tiles = t // tm
    x2 = x.reshape(b * t, d)
    p, lr = _inproj(x2, mod3, lambda i: i // tiles, norm1_g[li][None, :], w_main, w_lr,
                    (rt3, ct, tiles), tm, P_WIDTH)
    p3 = p.reshape(b, t, P_WIDTH)
    o_f, o_b, _, _ = _gla(p3, lr.reshape(b, t, LR_PAD), wlr_f, blr_f, wlr_b, blr_b, s_f, s_b, emit_o=True)

    xr, xi = _fft(p3, QKVG_W, fmat, twc, tws)

    x1 = _merge(o_f.reshape(b * t, VAL_W), o_b.reshape(b * t, VAL_W), p,
                xr.reshape(b * t, FWIDTH), xi.reshape(b * t, FWIDTH), x2, rt3, ct, mod3,
                gla_norm_g[li][None, :], wgo, wfo, wo, csg, t)

    out = _mlp(x1, mod3, norm2_g[li][None, :], w1, w2, final_norm_g[None, :], t)
    return out.reshape(b, t, d)
```

```python
import functools
import math

import jax
import jax.numpy as jnp
import numpy as np
from jax import lax
from jax.experimental import pallas as pl
from jax.experimental.pallas import tpu as pltpu

F32 = jnp.float32
BF16 = jnp.bfloat16

D_MODEL = 2048
GRID_W = 64
HEADS = 4
DK = 128
DV = 256
KEY_W = HEADS * DK
VAL_W = HEADS * DV
RANK = 16
TAU = 16.0
FGROUPS = 4
FGDIM = 256
FWIDTH = FGROUPS * FGDIM
D_FF = 4 * D_MODEL
N_MOD = 6
EPS = 1e-6
POS_TEMP = 10000.0

QKVG_W = 2 * KEY_W + 2 * VAL_W
P_WIDTH = QKVG_W + FWIDTH + 2 * D_MODEL
LR_PAD = 128
GLA_CHUNK = 128
GLA_CHUNKS_PER_STEP = 4
RADIX = 8
LANES = 128

V7X_VMEM_LIMIT = 56 * 1024 * 1024


def _cparams(sem, vmem=V7X_VMEM_LIMIT):
    return pltpu.CompilerParams(dimension_semantics=sem, vmem_limit_bytes=vmem)


def _silu(x):
    return x * jax.nn.sigmoid(x)


def _mod_kernel(ct_ref, w_ref, b_ref, o_ref, *, n_used):
    st = _silu(ct_ref[...])
    w = w_ref[...]
    bias = b_ref[...]
    rows = [jnp.sum(w * st[:, m:m + 1], axis=0, keepdims=True) + bias for m in range(n_used)]
    rows += [bias] * (o_ref.shape[0] - n_used)
    o_ref[...] = jnp.concatenate(rows, axis=0)


def _mod(cpad_t, n_used, w_mod, b_mod):
    d, rows = cpad_t.shape
    n = w_mod.shape[1]
    tn = 1024
    return pl.pallas_call(
        functools.partial(_mod_kernel, n_used=n_used),
        out_shape=jax.ShapeDtypeStruct((rows, n), F32),
        grid=(n // tn,),
        in_specs=[pl.BlockSpec((d, rows), lambda j: (0, 0)),
                  pl.BlockSpec((d, tn), lambda j: (0, j)),
                  pl.BlockSpec((1, tn), lambda j: (0, j))],
        out_specs=pl.BlockSpec((rows, tn), lambda j: (0, j)),
        compiler_params=_cparams(("arbitrary",)),
        name="mod",
    )(cpad_t, w_mod, b_mod)


def _wprep_kernel(wt_ref, lrt_ref, wm_ref, wlr_ref):
    wm_ref[...] = wt_ref[...].T.astype(BF16)

    @pl.when(pl.program_id(0) == 0)
    def _():
        lrt = lrt_ref[...]
        r = lax.broadcasted_iota(jnp.int32, lrt.shape, 0)
        wlr_ref[...] = jnp.where(r < 2 * RANK, lrt, 0.0).T.astype(BF16)


def _wprep(w_in_t):
    n, d = w_in_t.shape
    tc = 512
    n_before = QKVG_W // tc

    def src_row(j):
        return pl.multiple_of(jnp.where(j >= n_before, j * tc + 2 * RANK, j * tc), 2 * RANK)

    return pl.pallas_call(
        _wprep_kernel,
        out_shape=(jax.ShapeDtypeStruct((d, P_WIDTH), BF16), jax.ShapeDtypeStruct((d, LR_PAD), BF16)),
        grid=(P_WIDTH // tc,),
        in_specs=[pl.BlockSpec((pl.Element(tc), pl.Element(d)), lambda j: (src_row(j), 0)),
                  pl.BlockSpec((LR_PAD, d), lambda j: (QKVG_W // LR_PAD, 0))],
        out_specs=(pl.BlockSpec((d, tc), lambda j: (0, j)), pl.BlockSpec((d, LR_PAD), lambda j: (0, 0))),
        compiler_params=_cparams(("arbitrary",)),
        name="wprep",
    )(w_in_t, w_in_t)


def _add_pos(x, rt_ref, ct_ref):
    tm, d = x.shape
    x3 = x.reshape(tm // GRID_W, GRID_W, d)
    half = d // 2
    lo = x3[:, :, :half] + rt_ref[...]
    hi = x3[:, :, half:] + ct_ref[...][None]
    return jnp.concatenate([lo, hi], axis=-1).reshape(tm, d)


def _rms(x, g):
    return x * lax.rsqrt(jnp.mean(x * x, axis=-1, keepdims=True) + EPS) * g


def _inproj_kernel(*refs, add_pos):
    if add_pos:
        x_ref, rt_ref, ct_ref, sh_ref, sc_ref, g_ref, w_ref, wlr_ref, p_ref, lr_ref, h_scr = refs
    else:
        x_ref, sh_ref, sc_ref, g_ref, w_ref, wlr_ref, p_ref, lr_ref, h_scr = refs

    @pl.when(pl.program_id(1) == 0)
    def _():
        x = x_ref[...]
        if add_pos:
            x = _add_pos(x, rt_ref, ct_ref)
        h = _rms(x, g_ref[...]) * (1.0 + sc_ref[...]) + sh_ref[...]
        hb = h.astype(BF16)
        h_scr[...] = hb
        lr_ref[...] = jnp.dot(hb, wlr_ref[...], preferred_element_type=F32)

    p_ref[...] = jnp.dot(h_scr[...], w_ref[...], preferred_element_type=F32).astype(BF16)


def _inproj(x2, mod3, mod_row_of_tile, norm_g, w_main, w_lr, pos_tabs, tm, width):
    ntok, d = x2.shape
    tn = 1024
    add_pos = pos_tabs is not None
    in_specs = [pl.BlockSpec((tm, d), lambda i, j: (i, 0))]
    args = [x2]
    if add_pos:
        rt3, ct, tiles_per_seq = pos_tabs
        rpt = tm // GRID_W
        in_specs += [pl.BlockSpec((rpt, 1, d // 2), lambda i, j: (i % tiles_per_seq, 0, 0)),
                     pl.BlockSpec((GRID_W, d // 2), lambda i, j: (0, 0))]
        args += [rt3, ct]
    in_specs += [pl.BlockSpec((None, 1, d), lambda i, j: (mod_row_of_tile(i), 0, 0)),
                 pl.BlockSpec((None, 1, d), lambda i, j: (mod_row_of_tile(i), 0, 1)),
                 pl.BlockSpec((1, d), lambda i, j: (0, 0)),
                 pl.BlockSpec((d, tn), lambda i, j: (0, j)),
                 pl.BlockSpec((d, LR_PAD), lambda i, j: (0, 0))]
    args += [mod3, mod3, norm_g, w_main, w_lr]
    return pl.pallas_call(
        functools.partial(_inproj_kernel, add_pos=add_pos),
        out_shape=(jax.ShapeDtypeStruct((ntok, width), BF16),
                   jax.ShapeDtypeStruct((ntok, LR_PAD), F32)),
        grid=(ntok // tm, width // tn),
        in_specs=in_specs,
        out_specs=(pl.BlockSpec((tm, tn), lambda i, j: (i, j)),
                   pl.BlockSpec((tm, LR_PAD), lambda i, j: (i, 0))),
        scratch_shapes=[pltpu.VMEM((tm, d), BF16)],
        compiler_params=_cparams(("arbitrary", "arbitrary")),
        name="inproj_pos" if add_pos else "inproj_ctx",
    )(*args)


def _log_sigmoid(z):
    return jnp.minimum(z, 0.0) - jnp.log1p(jnp.exp(-jnp.abs(z)))


def _gla_chunks(dirs, c):
    n_sub = dirs[0][0].shape[0] // c
    row = lax.broadcasted_iota(jnp.int32, (c, c), 0)
    col = lax.broadcasted_iota(jnp.int32, (c, c), 1)

    chains = {}
    for di, (qk_ref, v_ref, lr_ref, wlr_ref, blr_ref, s_scr, o_ref, backward) in enumerate(dirs):
        keep = (row <= col) if backward else (row >= col)
        tri = jnp.where(keep, 1.0, 0.0).astype(BF16)
        last = 0 if backward else c - 1
        mid = c // 2 if backward else c // 2 - 1
        order = range(n_sub - 1, -1, -1) if backward else range(n_sub)
        for step, sub in enumerate(order):
            rows = slice(sub * c, (sub + 1) * c)
            z = jnp.dot(lr_ref[rows, :].astype(BF16), wlr_ref[...], preferred_element_type=F32) + blr_ref[...]
            la = _log_sigmoid(z) * (1.0 / TAU)
            la_hi = la.astype(BF16)
            la_lo = (la - la_hi.astype(F32)).astype(BF16)
            cum2 = jnp.dot(tri, jnp.concatenate([la_hi, la_lo], axis=1), preferred_element_type=F32)
            cum_all = cum2[:, :KEY_W] + cum2[:, KEY_W:]
            for h in range(HEADS):
                cum = cum_all[:, h * DK:(h + 1) * DK]
                tot = cum[last:last + 1, :]
                ref_pt = cum[mid:mid + 1, :]
                q = qk_ref[rows, h * DK:(h + 1) * DK].astype(F32) * (DK ** -0.5)
                k = qk_ref[rows, KEY_W + h * DK:KEY_W + (h + 1) * DK].astype(F32)
                chains[(step, di, h)] = dict(
                    keep=keep, rows=rows,
                    v=v_ref[rows, h * DV:(h + 1) * DV],
                    q_mid=(q * jnp.exp(cum - ref_pt)).astype(BF16),
                    k_mid=(k * jnp.exp(ref_pt - cum)).astype(BF16),
                    q_dec=(q * jnp.exp(cum)).astype(BF16),
                    k_end_t=(k * jnp.exp(tot - cum)).T.astype(BF16),
                    dec_col=jnp.broadcast_to(jnp.exp(tot), (DK, DK)).T)

    for ch in chains.values():
        s = lax.dot_general(ch["q_mid"], ch["k_mid"], (((1,), (1,)), ((), ())), preferred_element_type=F32)
        ch["scores"] = jnp.where(ch["keep"], s, 0.0).astype(BF16)
    for ch in chains.values():
        ch["kv"] = jnp.dot(ch["k_end_t"], ch["v"], preferred_element_type=F32)

    state = {(di, h): d[5][h] for di, d in enumerate(dirs) for h in range(HEADS)}
    for step in range(n_sub):
        for di, d in enumerate(dirs):
            o_ref = d[6]
            for h in range(HEADS):
                ch = chains[(step, di, h)]
                s_prev = state[(di, h)]
                lhs = jnp.concatenate([ch["scores"], ch["q_dec"]], axis=1)
                rhs = jnp.concatenate([ch["v"], s_prev.astype(BF16)], axis=0)
                o = jnp.dot(lhs, rhs, preferred_element_type=F32)
                if o_ref is not None:
                    o_ref[ch["rows"], h * DV:(h + 1) * DV] = o.astype(o_ref.dtype)
                dec = jnp.concatenate([ch["dec_col"]] * (DV // DK), axis=1)
                state[(di, h)] = s_prev * dec + ch["kv"]
    for (di, h), s in state.items():
        dirs[di][5][h] = s


def _gla_kernel(*refs, emit_o):
    (qkf, vf, lrf, qkb, vb, lrb, wf, bf, wb, bb, s0f, s0b) = refs[:12]
    if emit_o:
        of, ob, sf_out, sb_out, s_scr = refs[12:]
    else:
        sf_out, sb_out, s_scr = refs[12:]
        of = ob = None
    i = pl.program_id(1)

    @pl.when(i == 0)
    def _():
        s_scr[0] = s0f[...]
        s_scr[1] = s0b[...]

    _gla_chunks([(qkf, vf, lrf, wf, bf, s_scr.at[0], of, False),
                 (qkb, vb, lrb, wb, bb, s_scr.at[1], ob, True)], GLA_CHUNK)

    @pl.when(i == pl.num_programs(1) - 1)
    def _():
        sf_out[...] = s_scr[0]
        sb_out[...] = s_scr[1]


def _gla(p3, lr3, wlr_f, blr_f, wlr_b, blr_b, s0f, s0b, emit_o):
    b, t, _ = p3.shape
    per_step = max(s for s in range(1, GLA_CHUNKS_PER_STEP + 1) if t % (GLA_CHUNK * s) == 0)
    c = GLA_CHUNK * per_step
    n = t // c
    fwd = lambda blk: (lambda bi, i: (bi, i, blk))
    bwd = lambda blk: (lambda bi, i: (bi, n - 1 - i, blk))

    def seq_specs(mk):
        return [pl.BlockSpec((None, c, 2 * KEY_W), mk(0)),
                pl.BlockSpec((None, c, VAL_W), mk(1)),
                pl.BlockSpec((None, c, LR_PAD), mk(0))]
    full2 = lambda shape: pl.BlockSpec(shape, lambda bi, i: (0, 0))
    st_spec = pl.BlockSpec((None, HEADS, DK, DV), lambda bi, i: (bi, 0, 0, 0))
    in_specs = (seq_specs(fwd) + seq_specs(bwd)
                + [full2(wlr_f.shape), full2(blr_f.shape), full2(wlr_b.shape), full2(blr_b.shape),
                   st_spec, st_spec])
    st_shape = jax.ShapeDtypeStruct((b, HEADS, DK, DV), F32)
    out_shape = [st_shape, st_shape]
    out_specs = [st_spec, st_spec]
    if emit_o:
        o_shape = jax.ShapeDtypeStruct((b, t, VAL_W), BF16)
        out_shape = [o_shape, o_shape] + out_shape
        out_specs = [pl.BlockSpec((None, c, VAL_W), fwd(0)), pl.BlockSpec((None, c, VAL_W), bwd(0))] + out_specs
    return pl.pallas_call(
        functools.partial(_gla_kernel, emit_o=emit_o),
        out_shape=tuple(out_shape),
        grid=(b, n),
        in_specs=in_specs,
        out_specs=tuple(out_specs),
        scratch_shapes=[pltpu.VMEM((2, HEADS, DK, DV), F32)],
        compiler_params=_cparams(("arbitrary", "arbitrary")),
        name="gla_seq" if emit_o else "gla_ctx",
    )(p3, p3, lr3, p3, p3, lr3, wlr_f, blr_f, wlr_b, blr_b, s0f, s0b)


def _cadd(a, b):
    return a[0] + b[0], a[1] + b[1]


def _csub(a, b):
    return a[0] - b[0], a[1] - b[1]


def _cmul_neg_i(a):
    return a[1], -a[0]


def _dft4(y):
    t0, t1 = _cadd(y[0], y[2]), _csub(y[0], y[2])
    t2, t3 = _cadd(y[1], y[3]), _cmul_neg_i(_csub(y[1], y[3]))
    return [_cadd(t0, t2), _cadd(t1, t3), _csub(t0, t2), _csub(t1, t3)]


def _dft8(z):
    r = math.sqrt(0.5)
    s = [_cadd(z[a], z[a + 4]) for a in range(4)]
    d = [_csub(z[a], z[a + 4]) for a in range(4)]
    d1 = ((d[1][0] + d[1][1]) * r, (d[1][1] - d[1][0]) * r)
    d2 = _cmul_neg_i(d[2])
    d3 = ((d[3][1] - d[3][0]) * r, (-d[3][1] - d[3][0]) * r)
    ev = _dft4(s)
    od = _dft4([d[0], d1, d2, d3])
    out = [None] * 8
    for j in range(4):
        out[2 * j] = ev[j]
        out[2 * j + 1] = od[j]
    return out


def _fft_kernel(*refs, n_slabs):
    u_refs = refs[:n_slabs]
    f_ref, twc_ref, tws_ref, xr_ref, xi_ref, wide_scr, ub_scr = refs[n_slabs:]
    m = ub_scr.shape[1]

    @pl.when(pl.program_id(2) == 0)
    def _():
        for s in range(n_slabs):
            wide_scr[s] = u_refs[s][...].astype(F32)
        for a in range(RADIX):
            rows = [wide_scr[s, pl.ds(a, m, stride=RADIX), :].astype(BF16) for s in range(n_slabs)]
            ub_scr[a] = jnp.concatenate(rows, axis=1)

    t = f_ref.shape[0] // 2
    f = f_ref[...]
    z = []
    for a in range(RADIX):
        za = jnp.dot(f, ub_scr[a], preferred_element_type=F32)
        zr, zi = za[:t], za[t:]
        cc, ss = twc_ref[a], tws_ref[a]
        z.append((zr * cc + zi * ss, zi * cc - zr * ss))
    x = _dft8(z)
    for k1 in range(RADIX):
        xr_ref[k1] = x[k1][0].astype(xr_ref.dtype)
        xi_ref[k1] = x[k1][1].astype(xi_ref.dtype)


def _fft(p3, col0, fmat, twc, tws):
    b, seq, _ = p3.shape
    w = FWIDTH
    m = seq // RADIX
    nt, t2, _ = fmat.shape
    t = t2 // 2
    chb = 256
    n_slabs = chb // LANES
    slab0 = col0 // LANES
    out = jax.ShapeDtypeStruct((b, RADIX, m, w), BF16)
    o_spec = pl.BlockSpec((None, RADIX, t, chb), lambda bi, cj, kt: (bi, 0, kt, cj))
    slab = lambda s: pl.BlockSpec((None, seq, LANES), lambda bi, cj, kt: (bi, 0, slab0 + cj * n_slabs + s))
    return pl.pallas_call(
        functools.partial(_fft_kernel, n_slabs=n_slabs),
        out_shape=(out, out),
        grid=(b, w // chb, nt),
        in_specs=[slab(s) for s in range(n_slabs)] + [
                  pl.BlockSpec((None, t2, m), lambda bi, cj, kt: (kt, 0, 0)),
                  pl.BlockSpec((RADIX, t, 1), lambda bi, cj, kt: (0, kt, 0)),
                  pl.BlockSpec((RADIX, t, 1), lambda bi, cj, kt: (0, kt, 0))],
        out_specs=(o_spec, o_spec),
        scratch_shapes=[pltpu.VMEM((n_slabs, seq, LANES), F32), pltpu.VMEM((RADIX, m, chb), BF16)],
        compiler_params=_cparams(("arbitrary", "arbitrary", "arbitrary")),
        name="fft",
    )(*([p3] * n_slabs), fmat, twc, tws)


def _merge_kernel(of_ref, ob_ref, g_ref, gates_ref, xr_ref, xi_ref, x_ref, rt_ref, ct_ref, gt_ref,
                  gng_ref, wgo_ref, wfo_ref, wo_ref, csg_ref, o_ref):
    o = of_ref[...].astype(F32) + ob_ref[...].astype(F32)
    g = g_ref[...].astype(F32)
    gng = gng_ref[...]
    heads = []
    for h in range(HEADS):
        sl = slice(h * DV, (h + 1) * DV)
        heads.append((_rms(o[:, sl], gng) * _silu(g[:, sl])).astype(BF16))
    y_gla = jnp.dot(jnp.concatenate(heads, axis=1), wgo_ref[...], preferred_element_type=F32)

    csg = csg_ref[...]
    groups = []
    for gi in range(FGROUPS):
        sl = slice(gi * FGDIM, (gi + 1) * FGDIM)
        xg = jnp.concatenate([xr_ref[:, sl], xi_ref[:, sl]], axis=1)
        groups.append(jnp.dot(xg, csg, preferred_element_type=F32).astype(BF16))
    y_fft = jnp.dot(jnp.concatenate(groups, axis=1), wfo_ref[...], preferred_element_type=F32)

    gates = jax.nn.sigmoid(gates_ref[...].astype(F32))
    d = y_fft.shape[1]
    zmix = (gates[:, :d] * y_fft + gates[:, d:] * y_gla).astype(BF16)
    y = jnp.dot(zmix, wo_ref[...], preferred_element_type=F32)
    o_ref[...] = _add_pos(x_ref[...], rt_ref, ct_ref) + gt_ref[...] * y


def _const_spec(shape):
    zeros = (0,) * len(shape)
    return pl.BlockSpec(shape, lambda *idx: zeros, pipeline_mode=pl.Buffered(1))


def _merge(o_f, o_b, p2, xr, xi, x2, rt3, ct, mod3, gng, wgo, wfo, wo, csg, t):
    ntok, d = x2.shape
    tm = 256
    tiles_per_seq = t // tm
    rpt = tm // GRID_W
    row = lambda blk: (lambda i: (i, blk))
    in_specs = [pl.BlockSpec((tm, VAL_W), row(0)),
                pl.BlockSpec((tm, VAL_W), row(0)),
                pl.BlockSpec((tm, VAL_W), row(2)),
                pl.BlockSpec((tm, 2 * d), row(1)),
                pl.BlockSpec((tm, FWIDTH), row(0)),
                pl.BlockSpec((tm, FWIDTH), row(0)),
                pl.BlockSpec((tm, d), row(0)),
                pl.BlockSpec((rpt, 1, d // 2), lambda i: (i % tiles_per_seq, 0, 0)),
                _const_spec((GRID_W, d // 2)),
                pl.BlockSpec((None, 1, d), lambda i: (i // tiles_per_seq, 0, 2)),
                _const_spec(gng.shape), _const_spec(wgo.shape), _const_spec(wfo.shape),
                _const_spec(wo.shape), _const_spec(csg.shape)]
    return pl.pallas_call(
        _merge_kernel,
        out_shape=jax.ShapeDtypeStruct((ntok, d), F32),
        grid=(ntok // tm,),
        in_specs=in_specs,
        out_specs=pl.BlockSpec((tm, d), row(0)),
        compiler_params=_cparams(("arbitrary",)),
        name="merge",
    )(o_f, o_b, p2, p2, xr, xi, x2, rt3, ct, mod3, gng, wgo, wfo, wo, csg)


def _mlp_kernel(x_ref, xn_ref, sh_ref, sc_ref, shn_ref, scn_ref, gt_ref, g2_ref, w1_ref, w2_ref, fg_ref,
                o_ref, h_even, h_odd):
    i, j = pl.program_id(0), pl.program_id(1)
    nf = pl.num_programs(1)
    tm = x_ref.shape[0]
    slab = tm // nf
    slot = i % 2

    def normed(x, sh, sc):
        return (_rms(x, g2_ref[...]) * (1.0 + sc) + sh).astype(BF16)

    @pl.when((i == 0) & (j == 0))
    def _():
        h_even[...] = normed(x_ref[...], sh_ref[...], sc_ref[...])

    @pl.when(j == 0)
    def _():
        o_ref[...] = jnp.zeros_like(o_ref)

    def step(h_cur, h_next):
        r0 = pl.multiple_of(j * slab, slab)
        h_next[pl.ds(r0, slab), :] = normed(xn_ref[pl.ds(r0, slab), :], shn_ref[...], scn_ref[...])
        hid = jnp.dot(h_cur[...], w1_ref[...], preferred_element_type=F32)
        hid = jnp.square(jnp.maximum(hid, 0.0)).astype(BF16)
        o_ref[...] += jnp.dot(hid, w2_ref[...], preferred_element_type=F32)

    @pl.when(slot == 0)
    def _():
        step(h_even, h_odd)

    @pl.when(slot == 1)
    def _():
        step(h_odd, h_even)

    @pl.when(j == nf - 1)
    def _():
        xo = x_ref[...] + gt_ref[...] * o_ref[...]
        o_ref[...] = _rms(xo, fg_ref[...])


def _mlp(x2, mod3, g2, w1, w2, fg, t):
    ntok, d = x2.shape
    dff = w1.shape[1]
    tm, tf = 512, 1024
    assert (tm // (dff // tf)) % 8 == 0
    nt = ntok // tm
    tiles_per_seq = t // tm
    nxt = lambda i: jnp.minimum(i + 1, nt - 1)
    modspec = lambda blk: pl.BlockSpec((None, 1, d), lambda i, j: (i // tiles_per_seq, 0, blk))
    modspec_next = lambda blk: pl.BlockSpec((None, 1, d), lambda i, j: (nxt(i) // tiles_per_seq, 0, blk))
    return pl.pallas_call(
        _mlp_kernel,
        out_shape=jax.ShapeDtypeStruct((ntok, d), F32),
        grid=(nt, dff // tf),
        in_specs=[pl.BlockSpec((tm, d), lambda i, j: (i, 0)),
                  pl.BlockSpec((tm, d), lambda i, j: (nxt(i), 0)),
                  modspec(3), modspec(4), modspec_next(3), modspec_next(4), modspec(5),
                  pl.BlockSpec((1, d), lambda i, j: (0, 0)),
                  pl.BlockSpec((d, tf), lambda i, j: (0, j)),
                  pl.BlockSpec((tf, d), lambda i, j: (j, 0)),
                  pl.BlockSpec((1, d), lambda i, j: (0, 0))],
        out_specs=pl.BlockSpec((tm, d), lambda i, j: (i, 0)),
        scratch_shapes=[pltpu.VMEM((tm, d), BF16), pltpu.VMEM((tm, d), BF16)],
        compiler_params=_cparams(("arbitrary", "arbitrary")),
        name="mlp",
    )(x2, x2, mod3, mod3, mod3, mod3, mod3, g2, w1, w2, fg)


def _pos_tables(t, d):
    quarter = d // 4
    omega = 1.0 / (POS_TEMP ** (jnp.arange(quarter, dtype=F32) / quarter))
    er = jnp.arange(t // GRID_W, dtype=F32)[:, None] * omega[None, :]
    ec = jnp.arange(GRID_W, dtype=F32)[:, None] * omega[None, :]
    rt = jnp.concatenate([jnp.sin(er), jnp.cos(er)], axis=-1)
    ct = jnp.concatenate([jnp.sin(ec), jnp.cos(ec)], axis=-1)
    return rt[:, None, :], ct


def _dft_tables(t):
    m = t // RADIX
    tile = min(256, m)
    k = np.arange(m)
    ang = 2.0 * np.pi * ((k[:, None] * k[None, :]) % m) / m
    cos_t = np.cos(ang).reshape(m // tile, tile, m)
    sin_t = np.sin(ang).reshape(m // tile, tile, m)
    fmat = np.concatenate([cos_t, -sin_t], axis=1)
    a = np.arange(RADIX)
    tw = 2.0 * np.pi * (a[:, None] * k[None, :]) / t
    scale = 1.0 / math.sqrt(t * FGDIM)
    twc = (np.cos(tw) * scale)[:, :, None]
    tws = (np.sin(tw) * scale)[:, :, None]
    c = np.arange(FGDIM)
    cang = 2.0 * np.pi * ((c[:, None] * c[None, :]) % FGDIM) / FGDIM
    csg = np.concatenate([np.cos(cang), np.sin(cang)], axis=0)
    as_f32 = lambda a: jnp.asarray(a.astype(np.float32))
    return as_f32(fmat).astype(BF16), as_f32(twc), as_f32(tws), as_f32(csg).astype(BF16)


def _pad_lr_weight(w_lr, row0):
    out = jnp.zeros((LR_PAD, KEY_W), F32)
    return out.at[row0:row0 + RANK].set(w_lr).astype(BF16)


def kernel(x, c, ctx, c_ctx, w_mod, b_mod, norm1_g, norm2_g, w_in, w_lr_f, b_lr_f, w_lr_b, b_lr_b,
           gla_norm_g, w_fourier_out, w_gla_out, w_out, w_mlp_in, w_mlp_out, final_norm_g):
    b, t, d = x.shape
    tc = ctx.shape[1]
    depth = w_mod.shape[0]
    assert depth == 1 and d == D_MODEL and t % (RADIX * GRID_W) == 0 and tc % GLA_CHUNK == 0
    li = 0

    w_main, w_lr = _wprep(w_in[li].T)
    wlr_f, wlr_b = _pad_lr_weight(w_lr_f[li], 0), _pad_lr_weight(w_lr_b[li], RANK)
    blr_f, blr_b = b_lr_f[li][None, :], b_lr_b[li][None, :]
    wgo, wfo, wo = w_gla_out[li].astype(BF16), w_fourier_out[li].astype(BF16), w_out[li].astype(BF16)
    w1, w2 = w_mlp_in[li].astype(BF16), w_mlp_out[li].astype(BF16)

    rt3, ct = _pos_tables(t, d)
    fmat, twc, tws, csg = _dft_tables(t)

    rows = 8
    cpad = jnp.concatenate([c, c_ctx[None, :], jnp.zeros((rows - b - 1, d), F32)], axis=0)
    mod3 = _mod(cpad.T, b + 1, w_mod[li], b_mod[li][None, :]).reshape(rows, 1, N_MOD * d)

    tm_ctx = min(512, b * tc)
    qkv_w = 2 * KEY_W + VAL_W
    p_ctx, lr_ctx = _inproj(ctx.reshape(b * tc, d), mod3, lambda i: b, norm1_g[li][None, :],
                            w_main, w_lr, None, tm_ctx, qkv_w)
    s_zero = jnp.zeros((b, HEADS, DK, DV), F32)
    s_f, s_b = _gla(p_ctx.reshape(b, tc, qkv_w), lr_ctx.reshape(b, tc, LR_PAD),
                    wlr_f, blr_f, wlr_b, blr_b, s_zero, s_zero, emit_o=False)

    tm = min(1024, t)
    tiles = t // tm
    x2 = x.reshape(b * t, d)
    p, lr = _inproj(x2, mod3, lambda i: i // tiles, norm1_g[li][None, :], w_main, w_lr,
                    (rt3, ct, tiles), tm, P_WIDTH)
    p3 = p.reshape(b, t, P_WIDTH)
    o_f, o_b, _, _ = _gla(p3, lr.reshape(b, t, LR_PAD), wlr_f, blr_f, wlr_b, blr_b, s_f, s_b, emit_o=True)

    xr, xi = _fft(p3, QKVG_W, fmat, twc, tws)

    x1 = _merge(o_f.reshape(b * t, VAL_W), o_b.reshape(b * t, VAL_W), p,
                xr.reshape(b * t, FWIDTH), xi.reshape(b * t, FWIDTH), x2, rt3, ct, mod3,
                gla_norm_g[li][None, :], wgo, wfo, wo, csg, t)

    out = _mlp(x1, mod3, norm2_g[li][None, :], w1, w2, final_norm_g[None, :], t)
    return out.reshape(b, t, d)
```

```python
import functools
import math

import jax
import jax.numpy as jnp
import numpy as np
from jax import lax
from jax.experimental import pallas as pl
from jax.experimental.pallas import tpu as pltpu

F32 = jnp.float32
BF16 = jnp.bfloat16

D_MODEL = 2048
GRID_W = 64
HEADS = 4
DK = 128
DV = 256
KEY_W = HEADS * DK
VAL_W = HEADS * DV
RANK = 16
TAU = 16.0
FGROUPS = 4
FGDIM = 256
FWIDTH = FGROUPS * FGDIM
D_FF = 4 * D_MODEL
N_MOD = 6
EPS = 1e-6
POS_TEMP = 10000.0

QKVG_W = 2 * KEY_W + 2 * VAL_W
P_WIDTH = QKVG_W + FWIDTH + 2 * D_MODEL
LR_PAD = 128
GLA_CHUNK = 128
GLA_CHUNKS_PER_STEP = 4
RADIX = 8
LANES = 128

V7X_VMEM_LIMIT = 56 * 1024 * 1024


def _cparams(sem, vmem=V7X_VMEM_LIMIT):
    return pltpu.CompilerParams(dimension_semantics=sem, vmem_limit_bytes=vmem)


def _silu(x):
    return x * jax.nn.sigmoid(x)


def _mod_kernel(ct_ref, w_ref, b_ref, o_ref, *, n_used):
    st = _silu(ct_ref[...])
    w = w_ref[...]
    bias = b_ref[...]
    rows = [jnp.sum(w * st[:, m:m + 1], axis=0, keepdims=True) + bias for m in range(n_used)]
    rows += [bias] * (o_ref.shape[0] - n_used)
    o_ref[...] = jnp.concatenate(rows, axis=0)


def _mod(cpad_t, n_used, w_mod, b_mod):
    d, rows = cpad_t.shape
    n = w_mod.shape[1]
    tn = 1024
    return pl.pallas_call(
        functools.partial(_mod_kernel, n_used=n_used),
        out_shape=jax.ShapeDtypeStruct((rows, n), F32),
        grid=(n // tn,),
        in_specs=[pl.BlockSpec((d, rows), lambda j: (0, 0)),
                  pl.BlockSpec((d, tn), lambda j: (0, j)),
                  pl.BlockSpec((1, tn), lambda j: (0, j))],
        out_specs=pl.BlockSpec((rows, tn), lambda j: (0, j)),
        compiler_params=_cparams(("arbitrary",)),
        name="mod",
    )(cpad_t, w_mod, b_mod)


def _wprep_kernel(wt_ref, lrt_ref, wm_ref, wlr_ref):
    wm_ref[...] = wt_ref[...].T.astype(BF16)

    @pl.when(pl.program_id(0) == 0)
    def _():
        lrt = lrt_ref[...]
        r = lax.broadcasted_iota(jnp.int32, lrt.shape, 0)
        wlr_ref[...] = jnp.where(r < 2 * RANK, lrt, 0.0).T.astype(BF16)


def _wprep(w_in_t):
    n, d = w_in_t.shape
    tc = 512
    n_before = QKVG_W // tc

    def src_row(j):
        return pl.multiple_of(jnp.where(j >= n_before, j * tc + 2 * RANK, j * tc), 2 * RANK)

    return pl.pallas_call(
        _wprep_kernel,
        out_shape=(jax.ShapeDtypeStruct((d, P_WIDTH), BF16), jax.ShapeDtypeStruct((d, LR_PAD), BF16)),
        grid=(P_WIDTH // tc,),
        in_specs=[pl.BlockSpec((pl.Element(tc), pl.Element(d)), lambda j: (src_row(j), 0)),
                  pl.BlockSpec((LR_PAD, d), lambda j: (QKVG_W // LR_PAD, 0))],
        out_specs=(pl.BlockSpec((d, tc), lambda j: (0, j)), pl.BlockSpec((d, LR_PAD), lambda j: (0, 0))),
        compiler_params=_cparams(("arbitrary",)),
        name="wprep",
    )(w_in_t, w_in_t)


def _add_pos(x, rt_ref, ct_ref):
    tm, d = x.shape
    x3 = x.reshape(tm // GRID_W, GRID_W, d)
    half = d // 2
    lo = x3[:, :, :half] + rt_ref[...]
    hi = x3[:, :, half:] + ct_ref[...][None]
    return jnp.concatenate([lo, hi], axis=-1).reshape(tm, d)


def _rms(x, g):
    return x * lax.rsqrt(jnp.mean(x * x, axis=-1, keepdims=True) + EPS) * g


def _inproj_kernel(*refs, add_pos):
    if add_pos:
        x_ref, rt_ref, ct_ref, sh_ref, sc_ref, g_ref, w_ref, wlr_ref, p_ref, lr_ref, h_even, h_odd = refs
    else:
        x_ref, sh_ref, sc_ref, g_ref, w_ref, wlr_ref, p_ref, lr_ref, h_even, h_odd = refs
    r, j = pl.program_id(0), pl.program_id(1)
    tm = x_ref.shape[0]
    slab = tm // pl.num_programs(1)
    r0 = pl.multiple_of(j * slab, slab)

    def norm_slab(h_dst):
        x = x_ref[pl.ds(r0, slab), :]
        if add_pos:
            gr = slab // GRID_W
            x3 = x.reshape(gr, GRID_W, x.shape[1])
            half = x.shape[1] // 2
            lo = x3[:, :, :half] + rt_ref[pl.ds(j * gr, gr)]
            hi = x3[:, :, half:] + ct_ref[...][None]
            x = jnp.concatenate([lo, hi], axis=-1).reshape(x.shape)
        h = _rms(x, g_ref[...]) * (1.0 + sc_ref[...]) + sh_ref[...]
        h_dst[pl.ds(r0, slab), :] = h.astype(BF16)

    def step(h_cur, h_next):
        @pl.when(j == 0)
        def _():
            lr_ref[...] = jnp.dot(h_cur[...], wlr_ref[...], preferred_element_type=F32)

        norm_slab(h_next)
        p_ref[...] = jnp.dot(h_cur[...], w_ref[...], preferred_element_type=F32).astype(BF16)

    @pl.when(r == 0)
    def _():
        norm_slab(h_even)

    @pl.when((r > 0) & (r % 2 == 1))
    def _():
        step(h_even, h_odd)

    @pl.when((r > 0) & (r % 2 == 0))
    def _():
        step(h_odd, h_even)


def _inproj(x2, mod3, mod_row_of_tile, norm_g, w_main, w_lr, pos_tabs, tm, width):
    ntok, d = x2.shape
    tn = 1024
    nt, ncol = ntok // tm, width // tn
    add_pos = pos_tabs is not None
    assert (tm // ncol) % (GRID_W if add_pos else 16) == 0
    cur = lambda r: jnp.minimum(r, nt - 1)
    lag = lambda r: jnp.maximum(r - 1, 0)
    in_specs = [pl.BlockSpec((tm, d), lambda r, j: (cur(r), 0))]
    args = [x2]
    if add_pos:
        rt3, ct, tiles_per_seq = pos_tabs
        rpt = tm // GRID_W
        in_specs += [pl.BlockSpec((rpt, 1, d // 2), lambda r, j: (cur(r) % tiles_per_seq, 0, 0)),
                     pl.BlockSpec((GRID_W, d // 2), lambda r, j: (0, 0))]
        args += [rt3, ct]
    in_specs += [pl.BlockSpec((None, 1, d), lambda r, j: (mod_row_of_tile(cur(r)), 0, 0)),
                 pl.BlockSpec((None, 1, d), lambda r, j: (mod_row_of_tile(cur(r)), 0, 1)),
                 pl.BlockSpec((1, d), lambda r, j: (0, 0)),
                 pl.BlockSpec((d, tn), lambda r, j: (0, j)),
                 pl.BlockSpec((d, LR_PAD), lambda r, j: (0, 0))]
    args += [mod3, mod3, norm_g, w_main, w_lr]
    return pl.pallas_call(
        functools.partial(_inproj_kernel, add_pos=add_pos),
        out_shape=(jax.ShapeDtypeStruct((ntok, width), BF16),
                   jax.ShapeDtypeStruct((ntok, LR_PAD), F32)),
        grid=(nt + 1, ncol),
        in_specs=in_specs,
        out_specs=(pl.BlockSpec((tm, tn), lambda r, j: (lag(r), jnp.where(r == 0, 0, j))),
                   pl.BlockSpec((tm, LR_PAD), lambda r, j: (lag(r), 0))),
        scratch_shapes=[pltpu.VMEM((tm, d), BF16), pltpu.VMEM((tm, d), BF16)],
        compiler_params=_cparams(("arbitrary", "arbitrary")),
        name="inproj_pos" if add_pos else "inproj_ctx",
    )(*args)


def _log_sigmoid(z):
    return jnp.minimum(z, 0.0) - jnp.log1p(jnp.exp(-jnp.abs(z)))


def _gla_chunks(dirs, c):
    n_sub = dirs[0][0].shape[0] // c
    row = lax.broadcasted_iota(jnp.int32, (c, c), 0)
    col = lax.broadcasted_iota(jnp.int32, (c, c), 1)

    chains = {}
    log2_qscale = math.log2(DK ** -0.5)

    def front(step):
        for di, (qk_ref, v_ref, lr_ref, wlr_ref, blr_ref, s_scr, o_ref, backward) in enumerate(dirs):
            keep = (row <= col) if backward else (row >= col)
            tri = jnp.where(keep, 1.0, 0.0).astype(BF16)
            last = 0 if backward else c - 1
            mid = c // 2 if backward else c // 2 - 1
            sub = n_sub - 1 - step if backward else step
            rows = slice(sub * c, (sub + 1) * c)
            z = jnp.dot(lr_ref[rows, :].astype(BF16), wlr_ref[...], preferred_element_type=F32) + blr_ref[...]
            la = _log_sigmoid(z) * (math.log2(math.e) / TAU)
            la_hi = la.astype(BF16)
            la_lo = (la - la_hi.astype(F32)).astype(BF16)
            cum2 = jnp.dot(tri, jnp.concatenate([la_hi, la_lo], axis=1), preferred_element_type=F32)
            cum_all = cum2[:, :KEY_W] + cum2[:, KEY_W:]
            for h in range(HEADS):
                cum = cum_all[:, h * DK:(h + 1) * DK]
                tot = cum[last:last + 1, :]
                ref_pt = cum[mid:mid + 1, :]
                q = qk_ref[rows, h * DK:(h + 1) * DK].astype(F32)
                k = qk_ref[rows, KEY_W + h * DK:KEY_W + (h + 1) * DK].astype(F32)
                chains[(step, di, h)] = dict(
                    keep=keep, rows=rows,
                    v=v_ref[rows, h * DV:(h + 1) * DV],
                    q_mid=(q * jnp.exp2(cum - (ref_pt - log2_qscale))).astype(BF16),
                    k_mid=(k * jnp.exp2(ref_pt - cum)).astype(BF16),
                    q_dec=(q * jnp.exp2(cum + log2_qscale)).astype(BF16),
                    k_end_t=(k * jnp.exp2(tot - cum)).T.astype(BF16),
                    dec_col=jnp.broadcast_to(jnp.exp2(tot), (DK, DK)).T)

    def middle(step):
        for di in range(len(dirs)):
            for h in range(HEADS):
                ch = chains[(step, di, h)]
                s = lax.dot_general(ch["q_mid"], ch["k_mid"], (((1,), (1,)), ((), ())),
                                    preferred_element_type=F32)
                ch["scores"] = jnp.where(ch["keep"], s, 0.0).astype(BF16)
                ch["kv"] = jnp.dot(ch["k_end_t"], ch["v"], preferred_element_type=F32)

    state = {(di, h): d[5][h] for di, d in enumerate(dirs) for h in range(HEADS)}

    def tail(step):
        for di, d in enumerate(dirs):
            o_ref = d[6]
            for h in range(HEADS):
                ch = chains.pop((step, di, h))
                s_prev = state[(di, h)]
                lhs = jnp.concatenate([ch["scores"], ch["q_dec"]], axis=1)
                rhs = jnp.concatenate([ch["v"], s_prev.astype(BF16)], axis=0)
                o = jnp.dot(lhs, rhs, preferred_element_type=F32)
                if o_ref is not None:
                    o_ref[ch["rows"], h * DV:(h + 1) * DV] = o.astype(o_ref.dtype)
                dec = jnp.concatenate([ch["dec_col"]] * (DV // DK), axis=1)
                state[(di, h)] = s_prev * dec + ch["kv"]

    front(0)
    for step in range(n_sub):
        if step + 1 < n_sub:
            front(step + 1)
        middle(step)
        tail(step)
    for (di, h), s in state.items():
        dirs[di][5][h] = s


def _gla_kernel(*refs, emit_o):
    (qkf, vf, lrf, qkb, vb, lrb, wf, bf, wb, bb, s0f, s0b) = refs[:12]
    if emit_o:
        of, ob, sf_out, sb_out, s_scr = refs[12:]
    else:
        sf_out, sb_out, s_scr = refs[12:]
        of = ob = None
    i = pl.program_id(1)

    @pl.when(i == 0)
    def _():
        s_scr[0] = s0f[...]
        s_scr[1] = s0b[...]

    _gla_chunks([(qkf, vf, lrf, wf, bf, s_scr.at[0], of, False),
                 (qkb, vb, lrb, wb, bb, s_scr.at[1], ob, True)], GLA_CHUNK)

    @pl.when(i == pl.num_programs(1) - 1)
    def _():
        sf_out[...] = s_scr[0]
        sb_out[...] = s_scr[1]


def _gla(p3, lr3, wlr_f, blr_f, wlr_b, blr_b, s0f, s0b, emit_o):
    b, t, _ = p3.shape
    per_step = max(s for s in range(1, GLA_CHUNKS_PER_STEP + 1) if t % (GLA_CHUNK * s) == 0)
    c = GLA_CHUNK * per_step
    n = t // c
    fwd = lambda blk: (lambda bi, i: (bi, i, blk))
    bwd = lambda blk: (lambda bi, i: (bi, n - 1 - i, blk))

    def seq_specs(mk):
        return [pl.BlockSpec((None, c, 2 * KEY_W), mk(0)),
                pl.BlockSpec((None, c, VAL_W), mk(1)),
                pl.BlockSpec((None, c, LR_PAD), mk(0))]
    full2 = lambda shape: pl.BlockSpec(shape, lambda bi, i: (0, 0))
    st_spec = pl.BlockSpec((None, HEADS, DK, DV), lambda bi, i: (bi, 0, 0, 0))
    in_specs = (seq_specs(fwd) + seq_specs(bwd)
                + [full2(wlr_f.shape), full2(blr_f.shape), full2(wlr_b.shape), full2(blr_b.shape),
                   st_spec, st_spec])
    st_shape = jax.ShapeDtypeStruct((b, HEADS, DK, DV), F32)
    out_shape = [st_shape, st_shape]
    out_specs = [st_spec, st_spec]
    if emit_o:
        o_shape = jax.ShapeDtypeStruct((b, t, VAL_W), BF16)
        out_shape = [o_shape, o_shape] + out_shape
        out_specs = [pl.BlockSpec((None, c, VAL_W), fwd(0)), pl.BlockSpec((None, c, VAL_W), bwd(0))] + out_specs
    return pl.pallas_call(
        functools.partial(_gla_kernel, emit_o=emit_o),
        out_shape=tuple(out_shape),
        grid=(b, n),
        in_specs=in_specs,
        out_specs=tuple(out_specs),
        scratch_shapes=[pltpu.VMEM((2, HEADS, DK, DV), F32)],
        compiler_params=_cparams(("arbitrary", "arbitrary")),
        name="gla_seq" if emit_o else "gla_ctx",
    )(p3, p3, lr3, p3, p3, lr3, wlr_f, blr_f, wlr_b, blr_b, s0f, s0b)


def _cadd(a, b):
    return a[0] + b[0], a[1] + b[1]


def _csub(a, b):
    return a[0] - b[0], a[1] - b[1]


def _cmul_neg_i(a):
    return a[1], -a[0]


def _dft4(y):
    t0, t1 = _cadd(y[0], y[2]), _csub(y[0], y[2])
    t2, t3 = _cadd(y[1], y[3]), _cmul_neg_i(_csub(y[1], y[3]))
    return [_cadd(t0, t2), _cadd(t1, t3), _csub(t0, t2), _csub(t1, t3)]


def _dft8(z):
    r = math.sqrt(0.5)
    s = [_cadd(z[a], z[a + 4]) for a in range(4)]
    d = [_csub(z[a], z[a + 4]) for a in range(4)]
    d1 = ((d[1][0] + d[1][1]) * r, (d[1][1] - d[1][0]) * r)
    d2 = _cmul_neg_i(d[2])
    d3 = ((d[3][1] - d[3][0]) * r, (-d[3][1] - d[3][0]) * r)
    ev = _dft4(s)
    od = _dft4([d[0], d1, d2, d3])
    out = [None] * 8
    for j in range(4):
        out[2 * j] = ev[j]
        out[2 * j + 1] = od[j]
    return out


def _fft_kernel(*refs, n_slabs):
    u_refs = refs[:n_slabs]
    f_ref, twc_ref, tws_ref, xr_ref, xi_ref, wide_scr, ub_scr = refs[n_slabs:]
    m = ub_scr.shape[1]

    @pl.when(pl.program_id(2) == 0)
    def _():
        for s in range(n_slabs):
            wide_scr[s] = u_refs[s][...].astype(F32)
        for a in range(RADIX):
            rows = [wide_scr[s, pl.ds(a, m, stride=RADIX), :].astype(BF16) for s in range(n_slabs)]
            ub_scr[a] = jnp.concatenate(rows, axis=1)

    t = f_ref.shape[0] // 2
    f = f_ref[...]
    z = []
    for a in range(RADIX):
        za = jnp.dot(f, ub_scr[a], preferred_element_type=F32)
        zr, zi = za[:t], za[t:]
        cc, ss = twc_ref[a], tws_ref[a]
        z.append((zr * cc + zi * ss, zi * cc - zr * ss))
    x = _dft8(z)
    for k1 in range(RADIX):
        xr_ref[k1] = x[k1][0].astype(xr_ref.dtype)
        xi_ref[k1] = x[k1][1].astype(xi_ref.dtype)


def _fft(p3, col0, fmat, twc, tws):
    b, seq, _ = p3.shape
    w = FWIDTH
    m = seq // RADIX
    nt, t2, _ = fmat.shape
    t = t2 // 2
    chb = 256
    n_slabs = chb // LANES
    slab0 = col0 // LANES
    out = jax.ShapeDtypeStruct((b, RADIX, m, w), BF16)
    o_spec = pl.BlockSpec((None, RADIX, t, chb), lambda bi, cj, kt: (bi, 0, kt, cj))
    slab = lambda s: pl.BlockSpec((None, seq, LANES), lambda bi, cj, kt: (bi, 0, slab0 + cj * n_slabs + s))
    return pl.pallas_call(
        functools.partial(_fft_kernel, n_slabs=n_slabs),
        out_shape=(out, out),
        grid=(b, w // chb, nt),
        in_specs=[slab(s) for s in range(n_slabs)] + [
                  pl.BlockSpec((None, t2, m), lambda bi, cj, kt: (kt, 0, 0)),
                  pl.BlockSpec((RADIX, t, 1), lambda bi, cj, kt: (0, kt, 0)),
                  pl.BlockSpec((RADIX, t, 1), lambda bi, cj, kt: (0, kt, 0))],
        out_specs=(o_spec, o_spec),
        scratch_shapes=[pltpu.VMEM((n_slabs, seq, LANES), F32), pltpu.VMEM((RADIX, m, chb), BF16)],
        compiler_params=_cparams(("arbitrary", "arbitrary", "arbitrary")),
        name="fft",
    )(*([p3] * n_slabs), fmat, twc, tws)


def _merge_kernel(of_ref, ob_ref, g_ref, gates_ref, xr_ref, xi_ref, x_ref, rt_ref, ct_ref, gt_ref,
                  gng_ref, wgo_ref, wfo_ref, wo_ref, csg_ref, o_ref):
    o = of_ref[...].astype(F32) + ob_ref[...].astype(F32)
    g = g_ref[...].astype(F32)
    gng = gng_ref[...]
    heads = []
    for h in range(HEADS):
        sl = slice(h * DV, (h + 1) * DV)
        heads.append((_rms(o[:, sl], gng) * _silu(g[:, sl])).astype(BF16))
    y_gla = jnp.dot(jnp.concatenate(heads, axis=1), wgo_ref[...], preferred_element_type=F32)

    csg = csg_ref[...]
    groups = []
    for gi in range(FGROUPS):
        sl = slice(gi * FGDIM, (gi + 1) * FGDIM)
        xg = jnp.concatenate([xr_ref[:, sl], xi_ref[:, sl]], axis=1)
        groups.append(jnp.dot(xg, csg, preferred_element_type=F32).astype(BF16))
    y_fft = jnp.dot(jnp.concatenate(groups, axis=1), wfo_ref[...], preferred_element_type=F32)

    gates = jax.nn.sigmoid(gates_ref[...].astype(F32))
    d = y_fft.shape[1]
    zmix = (gates[:, :d] * y_fft + gates[:, d:] * y_gla).astype(BF16)
    y = jnp.dot(zmix, wo_ref[...], preferred_element_type=F32)
    o_ref[...] = _add_pos(x_ref[...], rt_ref, ct_ref) + gt_ref[...] * y


def _const_spec(shape):
    zeros = (0,) * len(shape)
    return pl.BlockSpec(shape, lambda *idx: zeros, pipeline_mode=pl.Buffered(1))


def _merge(o_f, o_b, p2, xr, xi, x2, rt3, ct, mod3, gng, wgo, wfo, wo, csg, t):
    ntok, d = x2.shape
    tm = 256
    tiles_per_seq = t // tm
    rpt = tm // GRID_W
    row = lambda blk: (lambda i: (i, blk))
    in_specs = [pl.BlockSpec((tm, VAL_W), row(0)),
                pl.BlockSpec((tm, VAL_W), row(0)),
                pl.BlockSpec((tm, VAL_W), row(2)),
                pl.BlockSpec((tm, 2 * d), row(1)),
                pl.BlockSpec((tm, FWIDTH), row(0)),
                pl.BlockSpec((tm, FWIDTH), row(0)),
                pl.BlockSpec((tm, d), row(0)),
                pl.BlockSpec((rpt, 1, d // 2), lambda i: (i % tiles_per_seq, 0, 0)),
                _const_spec((GRID_W, d // 2)),
                pl.BlockSpec((None, 1, d), lambda i: (i // tiles_per_seq, 0, 2)),
                _const_spec(gng.shape), _const_spec(wgo.shape), _const_spec(wfo.shape),
                _const_spec(wo.shape), _const_spec(csg.shape)]
    return pl.pallas_call(
        _merge_kernel,
        out_shape=jax.ShapeDtypeStruct((ntok, d), F32),
        grid=(ntok // tm,),
        in_specs=in_specs,
        out_specs=pl.BlockSpec((tm, d), row(0)),
        compiler_params=_cparams(("arbitrary",)),
        name="merge",
    )(o_f, o_b, p2, p2, xr, xi, x2, rt3, ct, mod3, gng, wgo, wfo, wo, csg)


def _mlp_kernel(x_ref, xn_ref, sh_ref, sc_ref, shn_ref, scn_ref, gt_ref, g2_ref, w1_ref, w2_ref, fg_ref,
                o_ref, h_even, h_odd):
    i, j = pl.program_id(0), pl.program_id(1)
    nf = pl.num_programs(1)
    tm = x_ref.shape[0]
    slab = tm // nf
    slot = i % 2

    def normed(x, sh, sc):
        return (_rms(x, g2_ref[...]) * (1.0 + sc) + sh).astype(BF16)

    @pl.when((i == 0) & (j == 0))
    def _():
        h_even[...] = normed(x_ref[...], sh_ref[...], sc_ref[...])

    @pl.when(j == 0)
    def _():
        o_ref[...] = jnp.zeros_like(o_ref)

    def step(h_cur, h_next):
        r0 = pl.multiple_of(j * slab, slab)
        h_next[pl.ds(r0, slab), :] = normed(xn_ref[pl.ds(r0, slab), :], shn_ref[...], scn_ref[...])
        hid = jnp.dot(h_cur[...], w1_ref[...], preferred_element_type=F32)
        hid = jnp.square(jnp.maximum(hid, 0.0)).astype(BF16)
        o_ref[...] += jnp.dot(hid, w2_ref[...], preferred_element_type=F32)

    @pl.when(slot == 0)
    def _():
        step(h_even, h_odd)

    @pl.when(slot == 1)
    def _():
        step(h_odd, h_even)

    @pl.when(j == nf - 1)
    def _():
        xo = x_ref[...] + gt_ref[...] * o_ref[...]
        o_ref[...] = _rms(xo, fg_ref[...])


def _mlp(x2, mod3, g2, w1, w2, fg, t):
    ntok, d = x2.shape
    dff = w1.shape[1]
    tm, tf = 512, 1024
    assert (tm // (dff // tf)) % 8 == 0
    nt = ntok // tm
    tiles_per_seq = t // tm
    nxt = lambda i: jnp.minimum(i + 1, nt - 1)
    modspec = lambda blk: pl.BlockSpec((None, 1, d), lambda i, j: (i // tiles_per_seq, 0, blk))
    modspec_next = lambda blk: pl.BlockSpec((None, 1, d), lambda i, j: (nxt(i) // tiles_per_seq, 0, blk))
    return pl.pallas_call(
        _mlp_kernel,
        out_shape=jax.ShapeDtypeStruct((ntok, d), F32),
        grid=(nt, dff // tf),
        in_specs=[pl.BlockSpec((tm, d), lambda i, j: (i, 0)),
                  pl.BlockSpec((tm, d), lambda i, j: (nxt(i), 0)),
                  modspec(3), modspec(4), modspec_next(3), modspec_next(4), modspec(5),
                  pl.BlockSpec((1, d), lambda i, j: (0, 0)),
                  pl.BlockSpec((d, tf), lambda i, j: (0, j)),
                  pl.BlockSpec((tf, d), lambda i, j: (j, 0)),
                  pl.BlockSpec((1, d), lambda i, j: (0, 0))],
        out_specs=pl.BlockSpec((tm, d), lambda i, j: (i, 0)),
        scratch_shapes=[pltpu.VMEM((tm, d), BF16), pltpu.VMEM((tm, d), BF16)],
        compiler_params=_cparams(("arbitrary", "arbitrary")),
        name="mlp",
    )(x2, x2, mod3, mod3, mod3, mod3, mod3, g2, w1, w2, fg)


def _pos_tables(t, d):
    quarter = d // 4
    omega = 1.0 / (POS_TEMP ** (jnp.arange(quarter, dtype=F32) / quarter))
    er = jnp.arange(t // GRID_W, dtype=F32)[:, None] * omega[None, :]
    ec = jnp.arange(GRID_W, dtype=F32)[:, None] * omega[None, :]
    rt = jnp.concatenate([jnp.sin(er), jnp.cos(er)], axis=-1)
    ct = jnp.concatenate([jnp.sin(ec), jnp.cos(ec)], axis=-1)
    return rt[:, None, :], ct


def _dft_tables(t):
    m = t // RADIX
    tile = min(256, m)
    k = np.arange(m)
    ang = 2.0 * np.pi * ((k[:, None] * k[None, :]) % m) / m
    cos_t = np.cos(ang).reshape(m // tile, tile, m)
    sin_t = np.sin(ang).reshape(m // tile, tile, m)
    fmat = np.concatenate([cos_t, -sin_t], axis=1)
    a = np.arange(RADIX)
    tw = 2.0 * np.pi * (a[:, None] * k[None, :]) / t
    scale = 1.0 / math.sqrt(t * FGDIM)
    twc = (np.cos(tw) * scale)[:, :, None]
    tws = (np.sin(tw) * scale)[:, :, None]
    c = np.arange(FGDIM)
    cang = 2.0 * np.pi * ((c[:, None] * c[None, :]) % FGDIM) / FGDIM
    csg = np.concatenate([np.cos(cang), np.sin(cang)], axis=0)
    as_f32 = lambda a: jnp.asarray(a.astype(np.float32))
    return as_f32(fmat).astype(BF16), as_f32(twc), as_f32(tws), as_f32(csg).astype(BF16)


def _pad_lr_weight(w_lr, row0):
    out = jnp.zeros((LR_PAD, KEY_W), F32)
    return out.at[row0:row0 + RANK].set(w_lr).astype(BF16)


def kernel(x, c, ctx, c_ctx, w_mod, b_mod, norm1_g, norm2_g, w_in, w_lr_f, b_lr_f, w_lr_b, b_lr_b,
           gla_norm_g, w_fourier_out, w_gla_out, w_out, w_mlp_in, w_mlp_out, final_norm_g):
    b, t, d = x.shape
    tc = ctx.shape[1]
    depth = w_mod.shape[0]
    assert depth == 1 and d == D_MODEL and t % (RADIX * GRID_W) == 0 and tc % GLA_CHUNK == 0
    li = 0

    w_main, w_lr = _wprep(w_in[li].T)
    wlr_f, wlr_b = _pad_lr_weight(w_lr_f[li], 0), _pad_lr_weight(w_lr_b[li], RANK)
    blr_f, blr_b = b_lr_f[li][None, :], b_lr_b[li][None, :]
    wgo, wfo, wo = w_gla_out[li].astype(BF16), w_fourier_out[li].astype(BF16), w_out[li].astype(BF16)
    w1, w2 = w_mlp_in[li].astype(BF16), w_mlp_out[li].astype(BF16)

    rt3, ct = _pos_tables(t, d)
    fmat, twc, tws, csg = _dft_tables(t)

    rows = 8
    cpad = jnp.concatenate([c, c_ctx[None, :], jnp.zeros((rows - b - 1, d), F32)], axis=0)
    mod3 = _mod(cpad.T, b + 1, w_mod[li], b_mod[li][None, :]).reshape(rows, 1, N_MOD * d)

    tm_ctx = min(512, b * tc)
    qkv_w = 2 * KEY_W + VAL_W
    p_ctx, lr_ctx = _inproj(ctx.reshape(b * tc, d), mod3, lambda i: b, norm1_g[li][None, :],
                            w_main, w_lr, None, tm_ctx, qkv_w)
    s_zero = jnp.zeros((b, HEADS, DK, DV), F32)
    s_f, s_b = _gla(p_ctx.reshape(b, tc, qkv_w), lr_ctx.reshape(b, tc, LR_PAD),
                    wlr_f, blr_f, wlr_b, blr_b, s_zero, s_zero, emit_o=False)

    tm = min(1024, t)
    tiles = t // tm
    x2 = x.reshape(b * t, d)
    p, lr = _inproj(x2, mod3, lambda i: i // tiles, norm1_g[li][None, :], w_main, w_lr,
                    (rt3, ct, tiles), tm, P_WIDTH)
    p3 = p.reshape(b, t, P_WIDTH)
    o_f, o_b, _, _ = _gla(p3, lr.reshape(b, t, LR_PAD), wlr_f, blr_f, wlr_b, blr_b, s_f, s_b, emit_o=True)

    xr, xi = _fft(p3, QKVG_W, fmat, twc, tws)

    x1 = _merge(o_f.reshape(b * t, VAL_W), o_b.reshape(b * t, VAL_W), p,
                xr.reshape(b * t, FWIDTH), xi.reshape(b * t, FWIDTH), x2, rt3, ct, mod3,
                gla_norm_g[li][None, :], wgo, wfo, wo, csg, t)

    out = _mlp(x1, mod3, norm2_g[li][None, :], w1, w2, final_norm_g[None, :], t)
    return out.reshape(b, t, d)
```

```python
import functools
import math

import jax
import jax.numpy as jnp
import numpy as np
from jax import lax
from jax.experimental import pallas as pl
from jax.experimental.pallas import tpu as pltpu

F32 = jnp.float32
BF16 = jnp.bfloat16

D_MODEL = 2048
GRID_W = 64
HEADS = 4
DK = 128
DV = 256
KEY_W = HEADS * DK
VAL_W = HEADS * DV
RANK = 16
TAU = 16.0
FGROUPS = 4
FGDIM = 256
FWIDTH = FGROUPS * FGDIM
D_FF = 4 * D_MODEL
N_MOD = 6
EPS = 1e-6
POS_TEMP = 10000.0

QKVG_W = 2 * KEY_W + 2 * VAL_W
P_WIDTH = QKVG_W + FWIDTH + 2 * D_MODEL
LR_PAD = 128
GLA_CHUNK = 128
GLA_CHUNKS_PER_STEP = 4
RADIX = 8
LANES = 128

V7X_VMEM_LIMIT = 56 * 1024 * 1024


def _cparams(sem, vmem=V7X_VMEM_LIMIT):
    return pltpu.CompilerParams(dimension_semantics=sem, vmem_limit_bytes=vmem)


def _silu(x):
    return x * jax.nn.sigmoid(x)


def _mod_kernel(ct_ref, w_ref, b_ref, o_ref, *, n_used):
    st = _silu(ct_ref[...])
    w = w_ref[...]
    bias = b_ref[...]
    rows = [jnp.sum(w * st[:, m:m + 1], axis=0, keepdims=True) + bias for m in range(n_used)]
    rows += [bias] * (o_ref.shape[0] - n_used)
    o_ref[...] = jnp.concatenate(rows, axis=0)


def _mod(cpad_t, n_used, w_mod, b_mod):
    d, rows = cpad_t.shape
    n = w_mod.shape[1]
    tn = 1024
    return pl.pallas_call(
        functools.partial(_mod_kernel, n_used=n_used),
        out_shape=jax.ShapeDtypeStruct((rows, n), F32),
        grid=(n // tn,),
        in_specs=[pl.BlockSpec((d, rows), lambda j: (0, 0)),
                  pl.BlockSpec((d, tn), lambda j: (0, j)),
                  pl.BlockSpec((1, tn), lambda j: (0, j))],
        out_specs=pl.BlockSpec((rows, tn), lambda j: (0, j)),
        compiler_params=_cparams(("arbitrary",)),
        name="mod",
    )(cpad_t, w_mod, b_mod)


def _wprep_kernel(wt_ref, lrt_ref, wm_ref, wlr_ref):
    wm_ref[...] = wt_ref[...].T.astype(BF16)

    @pl.when(pl.program_id(0) == 0)
    def _():
        lrt = lrt_ref[...]
        r = lax.broadcasted_iota(jnp.int32, lrt.shape, 0)
        wlr_ref[...] = jnp.where(r < 2 * RANK, lrt, 0.0).T.astype(BF16)


def _wprep(w_in_t):
    n, d = w_in_t.shape
    tc = 512
    n_before = QKVG_W // tc

    def src_row(j):
        return pl.multiple_of(jnp.where(j >= n_before, j * tc + 2 * RANK, j * tc), 2 * RANK)

    return pl.pallas_call(
        _wprep_kernel,
        out_shape=(jax.ShapeDtypeStruct((d, P_WIDTH), BF16), jax.ShapeDtypeStruct((d, LR_PAD), BF16)),
        grid=(P_WIDTH // tc,),
        in_specs=[pl.BlockSpec((pl.Element(tc), pl.Element(d)), lambda j: (src_row(j), 0)),
                  pl.BlockSpec((LR_PAD, d), lambda j: (QKVG_W // LR_PAD, 0))],
        out_specs=(pl.BlockSpec((d, tc), lambda j: (0, j)), pl.BlockSpec((d, LR_PAD), lambda j: (0, 0))),
        compiler_params=_cparams(("arbitrary",)),
        name="wprep",
    )(w_in_t, w_in_t)


def _add_pos(x, rt_ref, ct_ref):
    tm, d = x.shape
    x3 = x.reshape(tm // GRID_W, GRID_W, d)
    half = d // 2
    lo = x3[:, :, :half] + rt_ref[...]
    hi = x3[:, :, half:] + ct_ref[...][None]
    return jnp.concatenate([lo, hi], axis=-1).reshape(tm, d)


def _rms(x, g):
    return x * lax.rsqrt(jnp.mean(x * x, axis=-1, keepdims=True) + EPS) * g


def _inproj_kernel(*refs, add_pos):
    if add_pos:
        x_ref, rt_ref, ct_ref, sh_ref, sc_ref, g_ref, w_ref, wlr_ref, p_ref, lr_ref, h_scr = refs
    else:
        x_ref, sh_ref, sc_ref, g_ref, w_ref, wlr_ref, p_ref, lr_ref, h_scr = refs

    @pl.when(pl.program_id(1) == 0)
    def _():
        x = x_ref[...]
        if add_pos:
            x = _add_pos(x, rt_ref, ct_ref)
        h = _rms(x, g_ref[...]) * (1.0 + sc_ref[...]) + sh_ref[...]
        hb = h.astype(BF16)
        h_scr[...] = hb
        lr_ref[...] = jnp.dot(hb, wlr_ref[...], preferred_element_type=F32)

    p_ref[...] = jnp.dot(h_scr[...], w_ref[...], preferred_element_type=F32).astype(BF16)


def _inproj(x2, mod3, mod_row_of_tile, norm_g, w_main, w_lr, pos_tabs, tm, width):
    ntok, d = x2.shape
    tn = 1024
    add_pos = pos_tabs is not None
    in_specs = [pl.BlockSpec((tm, d), lambda i, j: (i, 0))]
    args = [x2]
    if add_pos:
        rt3, ct, tiles_per_seq = pos_tabs
        rpt = tm // GRID_W
        in_specs += [pl.BlockSpec((rpt, 1, d // 2), lambda i, j: (i % tiles_per_seq, 0, 0)),
                     pl.BlockSpec((GRID_W, d // 2), lambda i, j: (0, 0))]
        args += [rt3, ct]
    in_specs += [pl.BlockSpec((None, 1, d), lambda i, j: (mod_row_of_tile(i), 0, 0)),
                 pl.BlockSpec((None, 1, d), lambda i, j: (mod_row_of_tile(i), 0, 1)),
                 pl.BlockSpec((1, d), lambda i, j: (0, 0)),
                 pl.BlockSpec((d, tn), lambda i, j: (0, j)),
                 pl.BlockSpec((d, LR_PAD), lambda i, j: (0, 0))]
    args += [mod3, mod3, norm_g, w_main, w_lr]
    return pl.pallas_call(
        functools.partial(_inproj_kernel, add_pos=add_pos),
        out_shape=(jax.ShapeDtypeStruct((ntok, width), BF16),
                   jax.ShapeDtypeStruct((ntok, LR_PAD), F32)),
        grid=(ntok // tm, width // tn),
        in_specs=in_specs,
        out_specs=(pl.BlockSpec((tm, tn), lambda i, j: (i, j)),
                   pl.BlockSpec((tm, LR_PAD), lambda i, j: (i, 0))),
        scratch_shapes=[pltpu.VMEM((tm, d), BF16)],
        compiler_params=_cparams(("arbitrary", "arbitrary")),
        name="inproj_pos" if add_pos else "inproj_ctx",
    )(*args)


def _log_sigmoid(z):
    return jnp.minimum(z, 0.0) - jnp.log1p(jnp.exp(-jnp.abs(z)))


def _gla_chunks(dirs, c):
    n_sub = dirs[0][0].shape[0] // c
    row = lax.broadcasted_iota(jnp.int32, (c, c), 0)
    col = lax.broadcasted_iota(jnp.int32, (c, c), 1)

    chains = {}
    log2_qscale = math.log2(DK ** -0.5)

    def front(step):
        for di, (qk_ref, v_ref, lr_ref, wlr_ref, blr_ref, s_scr, o_ref, backward) in enumerate(dirs):
            keep = (row <= col) if backward else (row >= col)
            tri = jnp.where(keep, 1.0, 0.0).astype(BF16)
            last = 0 if backward else c - 1
            mid = c // 2 if backward else c // 2 - 1
            sub = n_sub - 1 - step if backward else step
            rows = slice(sub * c, (sub + 1) * c)
            z = jnp.dot(lr_ref[rows, :].astype(BF16), wlr_ref[...], preferred_element_type=F32) + blr_ref[...]
            la = _log_sigmoid(z) * (math.log2(math.e) / TAU)
            la_hi = la.astype(BF16)
            la_lo = (la - la_hi.astype(F32)).astype(BF16)
            cum2 = jnp.dot(tri, jnp.concatenate([la_hi, la_lo], axis=1), preferred_element_type=F32)
            cum_all = cum2[:, :KEY_W] + cum2[:, KEY_W:]
            for h in range(HEADS):
                cum = cum_all[:, h * DK:(h + 1) * DK]
                tot = cum[last:last + 1, :]
                ref_pt = cum[mid:mid + 1, :]
                q = qk_ref[rows, h * DK:(h + 1) * DK].astype(F32)
                k = qk_ref[rows, KEY_W + h * DK:KEY_W + (h + 1) * DK].astype(F32)
                chains[(step, di, h)] = dict(
                    keep=keep, rows=rows,
                    v=v_ref[rows, h * DV:(h + 1) * DV],
                    q_mid=(q * jnp.exp2(cum - (ref_pt - log2_qscale))).astype(BF16),
                    k_mid=(k * jnp.exp2(ref_pt - cum)).astype(BF16),
                    q_dec=(q * jnp.exp2(cum + log2_qscale)).astype(BF16),
                    k_end_t=(k * jnp.exp2(tot - cum)).T.astype(BF16),
                    dec_col=jnp.broadcast_to(jnp.exp2(tot), (DK, DK)).T)

    def middle(step):
        for di in range(len(dirs)):
            for h in range(HEADS):
                ch = chains[(step, di, h)]
                s = lax.dot_general(ch["q_mid"], ch["k_mid"], (((1,), (1,)), ((), ())),
                                    preferred_element_type=F32)
                ch["scores"] = jnp.where(ch["keep"], s, 0.0).astype(BF16)
                ch["kv"] = jnp.dot(ch["k_end_t"], ch["v"], preferred_element_type=F32)

    state = {(di, h): d[5][h] for di, d in enumerate(dirs) for h in range(HEADS)}

    def tail(step):
        for di, d in enumerate(dirs):
            o_ref = d[6]
            for h in range(HEADS):
                ch = chains.pop((step, di, h))
                s_prev = state[(di, h)]
                lhs = jnp.concatenate([ch["scores"], ch["q_dec"]], axis=1)
                rhs = jnp.concatenate([ch["v"], s_prev.astype(BF16)], axis=0)
                o = jnp.dot(lhs, rhs, preferred_element_type=F32)
                if o_ref is not None:
                    o_ref[ch["rows"], h * DV:(h + 1) * DV] = o.astype(o_ref.dtype)
                dec = jnp.concatenate([ch["dec_col"]] * (DV // DK), axis=1)
                state[(di, h)] = s_prev * dec + ch["kv"]

    front(0)
    for step in range(n_sub):
        if step + 1 < n_sub:
            front(step + 1)
        middle(step)
        tail(step)
    for (di, h), s in state.items():
        dirs[di][5][h] = s


def _gla_kernel(*refs, emit_o):
    (qkf, vf, lrf, qkb, vb, lrb, wf, bf, wb, bb, s0f, s0b) = refs[:12]
    if emit_o:
        of, ob, sf_out, sb_out, s_scr = refs[12:]
    else:
        sf_out, sb_out, s_scr = refs[12:]
        of = ob = None
    i = pl.program_id(1)

    @pl.when(i == 0)
    def _():
        s_scr[0] = s0f[...]
        s_scr[1] = s0b[...]

    _gla_chunks([(qkf, vf, lrf, wf, bf, s_scr.at[0], of, False),
                 (qkb, vb, lrb, wb, bb, s_scr.at[1], ob, True)], GLA_CHUNK)

    @pl.when(i == pl.num_programs(1) - 1)
    def _():
        sf_out[...] = s_scr[0]
        sb_out[...] = s_scr[1]


def _gla(p3, lr3, wlr_f, blr_f, wlr_b, blr_b, s0f, s0b, emit_o):
    b, t, _ = p3.shape
    per_step = max(s for s in range(1, GLA_CHUNKS_PER_STEP + 1) if t % (GLA_CHUNK * s) == 0)
    c = GLA_CHUNK * per_step
    n = t // c
    fwd = lambda blk: (lambda bi, i: (bi, i, blk))
    bwd = lambda blk: (lambda bi, i: (bi, n - 1 - i, blk))

    def seq_specs(mk):
        return [pl.BlockSpec((None, c, 2 * KEY_W), mk(0)),
                pl.BlockSpec((None, c, VAL_W), mk(1)),
                pl.BlockSpec((None, c, LR_PAD), mk(0))]
    full2 = lambda shape: pl.BlockSpec(shape, lambda bi, i: (0, 0))
    st_spec = pl.BlockSpec((None, HEADS, DK, DV), lambda bi, i: (bi, 0, 0, 0))
    in_specs = (seq_specs(fwd) + seq_specs(bwd)
                + [full2(wlr_f.shape), full2(blr_f.shape), full2(wlr_b.shape), full2(blr_b.shape),
                   st_spec, st_spec])
    st_shape = jax.ShapeDtypeStruct((b, HEADS, DK, DV), F32)
    out_shape = [st_shape, st_shape]
    out_specs = [st_spec, st_spec]
    if emit_o:
        o_shape = jax.ShapeDtypeStruct((b, t, VAL_W), BF16)
        out_shape = [o_shape, o_shape] + out_shape
        out_specs = [pl.BlockSpec((None, c, VAL_W), fwd(0)), pl.BlockSpec((None, c, VAL_W), bwd(0))] + out_specs
    return pl.pallas_call(
        functools.partial(_gla_kernel, emit_o=emit_o),
        out_shape=tuple(out_shape),
        grid=(b, n),
        in_specs=in_specs,
        out_specs=tuple(out_specs),
        scratch_shapes=[pltpu.VMEM((2, HEADS, DK, DV), F32)],
        compiler_params=_cparams(("arbitrary", "arbitrary")),
        name="gla_seq" if emit_o else "gla_ctx",
    )(p3, p3, lr3, p3, p3, lr3, wlr_f, blr_f, wlr_b, blr_b, s0f, s0b)


def _cadd(a, b):
    return a[0] + b[0], a[1] + b[1]


def _csub(a, b):
    return a[0] - b[0], a[1] - b[1]


def _cmul_neg_i(a):
    return a[1], -a[0]


def _dft4(y):
    t0, t1 = _cadd(y[0], y[2]), _csub(y[0], y[2])
    t2, t3 = _cadd(y[1], y[3]), _cmul_neg_i(_csub(y[1], y[3]))
    return [_cadd(t0, t2), _cadd(t1, t3), _csub(t0, t2), _csub(t1, t3)]


def _dft8(z):
    r = math.sqrt(0.5)
    s = [_cadd(z[a], z[a + 4]) for a in range(4)]
    d = [_csub(z[a], z[a + 4]) for a in range(4)]
    d1 = ((d[1][0] + d[1][1]) * r, (d[1][1] - d[1][0]) * r)
    d2 = _cmul_neg_i(d[2])
    d3 = ((d[3][1] - d[3][0]) * r, (-d[3][1] - d[3][0]) * r)
    ev = _dft4(s)
    od = _dft4([d[0], d1, d2, d3])
    out = [None] * 8
    for j in range(4):
        out[2 * j] = ev[j]
        out[2 * j + 1] = od[j]
    return out


def _fft_kernel(*refs, n_slabs):
    u_refs = refs[:n_slabs]
    f_ref, twc_ref, tws_ref, xr_ref, xi_ref, wide_scr, ub_scr = refs[n_slabs:]
    m = ub_scr.shape[1]

    @pl.when(pl.program_id(2) == 0)
    def _():
        for s in range(n_slabs):
            wide_scr[s] = u_refs[s][...].astype(F32)
        for a in range(RADIX):
            rows = [wide_scr[s, pl.ds(a, m, stride=RADIX), :].astype(BF16) for s in range(n_slabs)]
            ub_scr[a] = jnp.concatenate(rows, axis=1)

    t = f_ref.shape[0] // 2
    f = f_ref[...]
    z = []
    for a in range(RADIX):
        za = jnp.dot(f, ub_scr[a], preferred_element_type=F32)
        zr, zi = za[:t], za[t:]
        cc, ss = twc_ref[a], tws_ref[a]
        z.append((zr * cc + zi * ss, zi * cc - zr * ss))
    x = _dft8(z)
    for k1 in range(RADIX):
        xr_ref[k1] = x[k1][0].astype(xr_ref.dtype)
        xi_ref[k1] = x[k1][1].astype(xi_ref.dtype)


def _fft(p3, col0, fmat, twc, tws):
    b, seq, _ = p3.shape
    w = FWIDTH
    m = seq // RADIX
    nt, t2, _ = fmat.shape
    t = t2 // 2
    chb = 256
    n_slabs = chb // LANES
    slab0 = col0 // LANES
    out = jax.ShapeDtypeStruct((b, RADIX, m, w), BF16)
    o_spec = pl.BlockSpec((None, RADIX, t, chb), lambda bi, cj, kt: (bi, 0, kt, cj))
    slab = lambda s: pl.BlockSpec((None, seq, LANES), lambda bi, cj, kt: (bi, 0, slab0 + cj * n_slabs + s))
    return pl.pallas_call(
        functools.partial(_fft_kernel, n_slabs=n_slabs),
        out_shape=(out, out),
        grid=(b, w // chb, nt),
        in_specs=[slab(s) for s in range(n_slabs)] + [
                  pl.BlockSpec((None, t2, m), lambda bi, cj, kt: (kt, 0, 0)),
                  pl.BlockSpec((RADIX, t, 1), lambda bi, cj, kt: (0, kt, 0)),
                  pl.BlockSpec((RADIX, t, 1), lambda bi, cj, kt: (0, kt, 0))],
        out_specs=(o_spec, o_spec),
        scratch_shapes=[pltpu.VMEM((n_slabs, seq, LANES), F32), pltpu.VMEM((RADIX, m, chb), BF16)],
        compiler_params=_cparams(("arbitrary", "arbitrary", "arbitrary")),
        name="fft",
    )(*([p3] * n_slabs), fmat, twc, tws)


def _merge_kernel(of_ref, ob_ref, g_ref, gates_ref, xr_ref, xi_ref, x_ref, rt_ref, ct_ref, gt_ref,
                  gng_ref, wgo_ref, wfo_ref, wo_ref, csg_ref, o_ref):
    o = of_ref[...].astype(F32) + ob_ref[...].astype(F32)
    g = g_ref[...].astype(F32)
    gng = gng_ref[...]
    heads = []
    for h in range(HEADS):
        sl = slice(h * DV, (h + 1) * DV)
        heads.append((_rms(o[:, sl], gng) * _silu(g[:, sl])).astype(BF16))
    y_gla = jnp.dot(jnp.concatenate(heads, axis=1), wgo_ref[...], preferred_element_type=F32)

    csg = csg_ref[...]
    groups = []
    for gi in range(FGROUPS):
        sl = slice(gi * FGDIM, (gi + 1) * FGDIM)
        xg = jnp.concatenate([xr_ref[:, sl], xi_ref[:, sl]], axis=1)
        groups.append(jnp.dot(xg, csg, preferred_element_type=F32).astype(BF16))
    y_fft = jnp.dot(jnp.concatenate(groups, axis=1), wfo_ref[...], preferred_element_type=F32)

    gates = jax.nn.sigmoid(gates_ref[...].astype(F32))
    d = y_fft.shape[1]
    zmix = (gates[:, :d] * y_fft + gates[:, d:] * y_gla).astype(BF16)
    y = jnp.dot(zmix, wo_ref[...], preferred_element_type=F32)
    o_ref[...] = _add_pos(x_ref[...], rt_ref, ct_ref) + gt_ref[...] * y


def _const_spec(shape):
    zeros = (0,) * len(shape)
    return pl.BlockSpec(shape, lambda *idx: zeros, pipeline_mode=pl.Buffered(1))


def _merge(o_f, o_b, p2, xr, xi, x2, rt3, ct, mod3, gng, wgo, wfo, wo, csg, t):
    ntok, d = x2.shape
    tm = 256
    tiles_per_seq = t // tm
    rpt = tm // GRID_W
    row = lambda blk: (lambda i: (i, blk))
    in_specs = [pl.BlockSpec((tm, VAL_W), row(0)),
                pl.BlockSpec((tm, VAL_W), row(0)),
                pl.BlockSpec((tm, VAL_W), row(2)),
                pl.BlockSpec((tm, 2 * d), row(1)),
                pl.BlockSpec((tm, FWIDTH), row(0)),
                pl.BlockSpec((tm, FWIDTH), row(0)),
                pl.BlockSpec((tm, d), row(0)),
                pl.BlockSpec((rpt, 1, d // 2), lambda i: (i % tiles_per_seq, 0, 0)),
                _const_spec((GRID_W, d // 2)),
                pl.BlockSpec((None, 1, d), lambda i: (i // tiles_per_seq, 0, 2)),
                _const_spec(gng.shape), _const_spec(wgo.shape), _const_spec(wfo.shape),
                _const_spec(wo.shape), _const_spec(csg.shape)]
    return pl.pallas_call(
        _merge_kernel,
        out_shape=jax.ShapeDtypeStruct((ntok, d), F32),
        grid=(ntok // tm,),
        in_specs=in_specs,
        out_specs=pl.BlockSpec((tm, d), row(0)),
        compiler_params=_cparams(("arbitrary",)),
        name="merge",
    )(o_f, o_b, p2, p2, xr, xi, x2, rt3, ct, mod3, gng, wgo, wfo, wo, csg)


def _mlp_kernel(x_ref, xn_ref, sh_ref, sc_ref, shn_ref, scn_ref, gt_ref, g2_ref, w1_ref, w2_ref, fg_ref,
                o_ref, h_even, h_odd):
    i, j = pl.program_id(0), pl.program_id(1)
    nf = pl.num_programs(1)
    tm = x_ref.shape[0]
    slab = tm // nf
    slot = i % 2

    def normed(x, sh, sc):
        return (_rms(x, g2_ref[...]) * (1.0 + sc) + sh).astype(BF16)

    @pl.when((i == 0) & (j == 0))
    def _():
        h_even[...] = normed(x_ref[...], sh_ref[...], sc_ref[...])

    @pl.when(j == 0)
    def _():
        o_ref[...] = jnp.zeros_like(o_ref)

    def step(h_cur, h_next):
        r0 = pl.multiple_of(j * slab, slab)
        h_next[pl.ds(r0, slab), :] = normed(xn_ref[pl.ds(r0, slab), :], shn_ref[...], scn_ref[...])
        hid = jnp.dot(h_cur[...], w1_ref[...], preferred_element_type=F32)
        hid = jnp.square(jnp.maximum(hid, 0.0)).astype(BF16)
        o_ref[...] += jnp.dot(hid, w2_ref[...], preferred_element_type=F32)

    @pl.when(slot == 0)
    def _():
        step(h_even, h_odd)

    @pl.when(slot == 1)
    def _():
        step(h_odd, h_even)

    @pl.when(j == nf - 1)
    def _():
        xo = x_ref[...] + gt_ref[...] * o_ref[...]
        o_ref[...] = _rms(xo, fg_ref[...])


def _mlp(x2, mod3, g2, w1, w2, fg, t):
    ntok, d = x2.shape
    dff = w1.shape[1]
    tm, tf = 512, 1024
    assert (tm // (dff // tf)) % 8 == 0
    nt = ntok // tm
    tiles_per_seq = t // tm
    nxt = lambda i: jnp.minimum(i + 1, nt - 1)
    modspec = lambda blk: pl.BlockSpec((None, 1, d), lambda i, j: (i // tiles_per_seq, 0, blk))
    modspec_next = lambda blk: pl.BlockSpec((None, 1, d), lambda i, j: (nxt(i) // tiles_per_seq, 0, blk))
    return pl.pallas_call(
        _mlp_kernel,
        out_shape=jax.ShapeDtypeStruct((ntok, d), F32),
        grid=(nt, dff // tf),
        in_specs=[pl.BlockSpec((tm, d), lambda i, j: (jnp.where(j >= 2, i, jnp.maximum(i - 1, 0)), 0)),
                  pl.BlockSpec((tm, d), lambda i, j: (nxt(i), 0)),
                  modspec(3), modspec(4), modspec_next(3), modspec_next(4), modspec(5),
                  pl.BlockSpec((1, d), lambda i, j: (0, 0)),
                  pl.BlockSpec((d, tf), lambda i, j: (0, j)),
                  pl.BlockSpec((tf, d), lambda i, j: (j, 0)),
                  pl.BlockSpec((1, d), lambda i, j: (0, 0))],
        out_specs=pl.BlockSpec((tm, d), lambda i, j: (i, 0)),
        scratch_shapes=[pltpu.VMEM((tm, d), BF16), pltpu.VMEM((tm, d), BF16)],
        compiler_params=_cparams(("arbitrary", "arbitrary")),
        name="mlp",
    )(x2, x2, mod3, mod3, mod3, mod3, mod3, g2, w1, w2, fg)


def _pos_tables(t, d):
    quarter = d // 4
    omega = 1.0 / (POS_TEMP ** (jnp.arange(quarter, dtype=F32) / quarter))
    er = jnp.arange(t // GRID_W, dtype=F32)[:, None] * omega[None, :]
    ec = jnp.arange(GRID_W, dtype=F32)[:, None] * omega[None, :]
    rt = jnp.concatenate([jnp.sin(er), jnp.cos(er)], axis=-1)
    ct = jnp.concatenate([jnp.sin(ec), jnp.cos(ec)], axis=-1)
    return rt[:, None, :], ct


def _dft_tables(t):
    m = t // RADIX
    tile = min(256, m)
    k = np.arange(m)
    ang = 2.0 * np.pi * ((k[:, None] * k[None, :]) % m) / m
    cos_t = np.cos(ang).reshape(m // tile, tile, m)
    sin_t = np.sin(ang).reshape(m // tile, tile, m)
    fmat = np.concatenate([cos_t, -sin_t], axis=1)
    a = np.arange(RADIX)
    tw = 2.0 * np.pi * (a[:, None] * k[None, :]) / t
    scale = 1.0 / math.sqrt(t * FGDIM)
    twc = (np.cos(tw) * scale)[:, :, None]
    tws = (np.sin(tw) * scale)[:, :, None]
    c = np.arange(FGDIM)
    cang = 2.0 * np.pi * ((c[:, None] * c[None, :]) % FGDIM) / FGDIM
    csg = np.concatenate([np.cos(cang), np.sin(cang)], axis=0)
    as_f32 = lambda a: jnp.asarray(a.astype(np.float32))
    return as_f32(fmat).astype(BF16), as_f32(twc), as_f32(tws), as_f32(csg).astype(BF16)


def _pad_lr_weight(w_lr, row0):
    out = jnp.zeros((LR_PAD, KEY_W), F32)
    return out.at[row0:row0 + RANK].set(w_lr).astype(BF16)


def kernel(x, c, ctx, c_ctx, w_mod, b_mod, norm1_g, norm2_g, w_in, w_lr_f, b_lr_f, w_lr_b, b_lr_b,
           gla_norm_g, w_fourier_out, w_gla_out, w_out, w_mlp_in, w_mlp_out, final_norm_g):
    b, t, d = x.shape
    tc = ctx.shape[1]
    depth = w_mod.shape[0]
    assert depth == 1 and d == D_MODEL and t % (RADIX * GRID_W) == 0 and tc % GLA_CHUNK == 0
    li = 0

    w_main, w_lr = _wprep(w_in[li].T)
    wlr_f, wlr_b = _pad_lr_weight(w_lr_f[li], 0), _pad_lr_weight(w_lr_b[li], RANK)
    blr_f, blr_b = b_lr_f[li][None, :], b_lr_b[li][None, :]
    wgo, wfo, wo = w_gla_out[li].astype(BF16), w_fourier_out[li].astype(BF16), w_out[li].astype(BF16)
    w1, w2 = w_mlp_in[li].astype(BF16), w_mlp_out[li].astype(BF16)

    rt3, ct = _pos_tables(t, d)
    fmat, twc, tws, csg = _dft_tables(t)

    rows = 8
    cpad = jnp.concatenate([c, c_ctx[None, :], jnp.zeros((rows - b - 1, d), F32)], axis=0)
    mod3 = _mod(cpad.T, b + 1, w_mod[li], b_mod[li][None, :]).reshape(rows, 1, N_MOD * d)

    tm_ctx = min(512, b * tc)
    qkv_w = 2 * KEY_W + VAL_W
    p_ctx, lr_ctx = _inproj(ctx.reshape(b * tc, d), mod3, lambda i: b, norm1_g[li][None, :],
                            w_main, w_lr, None, tm_ctx, qkv_w)
    s_zero = jnp.zeros((b, HEADS, DK, DV), F32)
    s_f, s_b = _gla(p_ctx.reshape(b, tc, qkv_w), lr_ctx.reshape(b, tc, LR_PAD),
                    wlr_f, blr_f, wlr_b, blr_b, s_zero, s_zero, emit_o=False)

    tm = min(1024, t)
    tiles = t // tm
    x2 = x.reshape(b * t, d)
    p, lr = _inproj(x2, mod3, lambda i: i // tiles, norm1_g[li][None, :], w_main, w_lr,
                    (rt3, ct, tiles), tm, P_WIDTH)
    p3 = p.reshape(b, t, P_WIDTH)
    o_f, o_b, _, _ = _gla(p3, lr.reshape(b, t, LR_PAD), wlr_f, blr_f, wlr_b, blr_b, s_f, s_b, emit_o=True)

    xr, xi = _fft(p3, QKVG_W, fmat, twc, tws)

    x1 = _merge(o_f.reshape(b * t, VAL_W), o_b.reshape(b * t, VAL_W), p,
                xr.reshape(b * t, FWIDTH), xi.reshape(b * t, FWIDTH), x2, rt3, ct, mod3,
                gla_norm_g[li][None, :], wgo, wfo, wo, csg, t)

    out = _mlp(x1, mod3, norm2_g[li][None, :], w1, w2, final_norm_g[None, :], t)
    return out.reshape(b, t, d)
```

```python
import functools
import math

import jax
import jax.numpy as jnp
import numpy as np
from jax import lax
from jax.experimental import pallas as pl
from jax.experimental.pallas import tpu as pltpu

F32 = jnp.float32
BF16 = jnp.bfloat16

D_MODEL = 2048
GRID_W = 64
HEADS = 4
DK = 128
DV = 256
KEY_W = HEADS * DK
VAL_W = HEADS * DV
RANK = 16
TAU = 16.0
FGROUPS = 4
FGDIM = 256
FWIDTH = FGROUPS * FGDIM
D_FF = 4 * D_MODEL
N_MOD = 6
EPS = 1e-6
POS_TEMP = 10000.0

QKVG_W = 2 * KEY_W + 2 * VAL_W
P_WIDTH = QKVG_W + FWIDTH + 2 * D_MODEL
LR_PAD = 128
GLA_CHUNK = 128
GLA_CHUNKS_PER_STEP = 4
RADIX = 8
LANES = 128

V7X_VMEM_LIMIT = 56 * 1024 * 1024


def _cparams(sem, vmem=V7X_VMEM_LIMIT):
    return pltpu.CompilerParams(dimension_semantics=sem, vmem_limit_bytes=vmem)


def _silu(x):
    return x * jax.nn.sigmoid(x)


def _mod_kernel(ct_ref, w_ref, b_ref, o_ref, *, n_used):
    st = _silu(ct_ref[...])
    w = w_ref[...]
    bias = b_ref[...]
    rows = [jnp.sum(w * st[:, m:m + 1], axis=0, keepdims=True) + bias for m in range(n_used)]
    rows += [bias] * (o_ref.shape[0] - n_used)
    o_ref[...] = jnp.concatenate(rows, axis=0)


def _mod(cpad_t, n_used, w_mod, b_mod):
    d, rows = cpad_t.shape
    n = w_mod.shape[1]
    tn = 1024
    return pl.pallas_call(
        functools.partial(_mod_kernel, n_used=n_used),
        out_shape=jax.ShapeDtypeStruct((rows, n), F32),
        grid=(n // tn,),
        in_specs=[pl.BlockSpec((d, rows), lambda j: (0, 0)),
                  pl.BlockSpec((d, tn), lambda j: (0, j)),
                  pl.BlockSpec((1, tn), lambda j: (0, j))],
        out_specs=pl.BlockSpec((rows, tn), lambda j: (0, j)),
        compiler_params=_cparams(("arbitrary",)),
        name="mod",
    )(cpad_t, w_mod, b_mod)


def _wprep_kernel(wt_ref, lrt_ref, wm_ref, wlr_ref):
    wm_ref[...] = wt_ref[...].T.astype(BF16)

    @pl.when(pl.program_id(0) == 0)
    def _():
        lrt = lrt_ref[...]
        r = lax.broadcasted_iota(jnp.int32, lrt.shape, 0)
        wlr_ref[...] = jnp.where(r < 2 * RANK, lrt, 0.0).T.astype(BF16)


def _wprep(w_in_t):
    n, d = w_in_t.shape
    tc = 512
    n_before = QKVG_W // tc

    def src_row(j):
        return pl.multiple_of(jnp.where(j >= n_before, j * tc + 2 * RANK, j * tc), 2 * RANK)

    return pl.pallas_call(
        _wprep_kernel,
        out_shape=(jax.ShapeDtypeStruct((d, P_WIDTH), BF16), jax.ShapeDtypeStruct((d, LR_PAD), BF16)),
        grid=(P_WIDTH // tc,),
        in_specs=[pl.BlockSpec((pl.Element(tc), pl.Element(d)), lambda j: (src_row(j), 0)),
                  pl.BlockSpec((LR_PAD, d), lambda j: (QKVG_W // LR_PAD, 0))],
        out_specs=(pl.BlockSpec((d, tc), lambda j: (0, j)), pl.BlockSpec((d, LR_PAD), lambda j: (0, 0))),
        compiler_params=_cparams(("arbitrary",)),
        name="wprep",
    )(w_in_t, w_in_t)


def _add_pos(x, rt_ref, ct_ref):
    tm, d = x.shape
    x3 = x.reshape(tm // GRID_W, GRID_W, d)
    half = d // 2
    lo = x3[:, :, :half] + rt_ref[...]
    hi = x3[:, :, half:] + ct_ref[...][None]
    return jnp.concatenate([lo, hi], axis=-1).reshape(tm, d)


def _rms(x, g):
    return x * lax.rsqrt(jnp.mean(x * x, axis=-1, keepdims=True) + EPS) * g


def _inproj_kernel(*refs, add_pos):
    if add_pos:
        x_ref, rt_ref, ct_ref, sh_ref, sc_ref, g_ref, w_ref, wlr_ref, p_ref, lr_ref, h_scr = refs
    else:
        x_ref, sh_ref, sc_ref, g_ref, w_ref, wlr_ref, p_ref, lr_ref, h_scr = refs

    @pl.when(pl.program_id(1) == 0)
    def _():
        x = x_ref[...]
        if add_pos:
            x = _add_pos(x, rt_ref, ct_ref)
        h = _rms(x, g_ref[...]) * (1.0 + sc_ref[...]) + sh_ref[...]
        hb = h.astype(BF16)
        h_scr[...] = hb
        lr_ref[...] = jnp.dot(hb, wlr_ref[...], preferred_element_type=F32)

    p_ref[...] = jnp.dot(h_scr[...], w_ref[...], preferred_element_type=F32).astype(BF16)


def _inproj(x2, mod3, mod_row_of_tile, norm_g, w_main, w_lr, pos_tabs, tm, width):
    ntok, d = x2.shape
    tn = 1024
    add_pos = pos_tabs is not None
    in_specs = [pl.BlockSpec((tm, d), lambda i, j: (i, 0))]
    args = [x2]
    if add_pos:
        rt3, ct, tiles_per_seq = pos_tabs
        rpt = tm // GRID_W
        in_specs += [pl.BlockSpec((rpt, 1, d // 2), lambda i, j: (i % tiles_per_seq, 0, 0)),
                     pl.BlockSpec((GRID_W, d // 2), lambda i, j: (0, 0))]
        args += [rt3, ct]
    in_specs += [pl.BlockSpec((None, 1, d), lambda i, j: (mod_row_of_tile(i), 0, 0)),
                 pl.BlockSpec((None, 1, d), lambda i, j: (mod_row_of_tile(i), 0, 1)),
                 pl.BlockSpec((1, d), lambda i, j: (0, 0)),
                 pl.BlockSpec((d, tn), lambda i, j: (0, j)),
                 pl.BlockSpec((d, LR_PAD), lambda i, j: (0, 0))]
    args += [mod3, mod3, norm_g, w_main, w_lr]
    return pl.pallas_call(
        functools.partial(_inproj_kernel, add_pos=add_pos),
        out_shape=(jax.ShapeDtypeStruct((ntok, width), BF16),
                   jax.ShapeDtypeStruct((ntok, LR_PAD), F32)),
        grid=(ntok // tm, width // tn),
        in_specs=in_specs,
        out_specs=(pl.BlockSpec((tm, tn), lambda i, j: (i, j)),
                   pl.BlockSpec((tm, LR_PAD), lambda i, j: (i, 0))),
        scratch_shapes=[pltpu.VMEM((tm, d), BF16)],
        compiler_params=_cparams(("arbitrary", "arbitrary")),
        name="inproj_pos" if add_pos else "inproj_ctx",
    )(*args)


def _log_sigmoid(z):
    return jnp.minimum(z, 0.0) - jnp.log1p(jnp.exp(-jnp.abs(z)))


def _gla_chunks(dirs, c):
    n_sub = dirs[0][0].shape[0] // c
    row = lax.broadcasted_iota(jnp.int32, (c, c), 0)
    col = lax.broadcasted_iota(jnp.int32, (c, c), 1)

    chains = {}
    log2_qscale = math.log2(DK ** -0.5)

    def front(step):
        for di, (qk_ref, v_ref, lr_ref, wlr_ref, blr_ref, s_scr, o_ref, backward) in enumerate(dirs):
            keep = (row <= col) if backward else (row >= col)
            tri = jnp.where(keep, 1.0, 0.0).astype(BF16)
            last = 0 if backward else c - 1
            mid = c // 2 if backward else c // 2 - 1
            sub = n_sub - 1 - step if backward else step
            rows = slice(sub * c, (sub + 1) * c)
            z = jnp.dot(lr_ref[rows, :].astype(BF16), wlr_ref[...], preferred_element_type=F32) + blr_ref[...]
            la = _log_sigmoid(z) * (math.log2(math.e) / TAU)
            la_hi = la.astype(BF16)
            la_lo = (la - la_hi.astype(F32)).astype(BF16)
            cum2 = jnp.dot(tri, jnp.concatenate([la_hi, la_lo], axis=1), preferred_element_type=F32)
            cum_all = cum2[:, :KEY_W] + cum2[:, KEY_W:]
            for h in range(HEADS):
                cum = cum_all[:, h * DK:(h + 1) * DK]
                tot = cum[last:last + 1, :]
                ref_pt = cum[mid:mid + 1, :]
                q = qk_ref[rows, h * DK:(h + 1) * DK].astype(F32)
                k = qk_ref[rows, KEY_W + h * DK:KEY_W + (h + 1) * DK].astype(F32)
                chains[(step, di, h)] = dict(
                    keep=keep, rows=rows,
                    v=v_ref[rows, h * DV:(h + 1) * DV],
                    q_mid=(q * jnp.exp2(cum - (ref_pt - log2_qscale))).astype(BF16),
                    k_mid=(k * jnp.exp2(ref_pt - cum)).astype(BF16),
                    q_dec=(q * jnp.exp2(cum + log2_qscale)).astype(BF16),
                    k_end_t=(k * jnp.exp2(tot - cum)).T.astype(BF16),
                    dec_col=jnp.broadcast_to(jnp.exp2(tot), (DK, DK)).T)

    def middle(step):
        for di in range(len(dirs)):
            for h in range(HEADS):
                ch = chains[(step, di, h)]
                s = lax.dot_general(ch["q_mid"], ch["k_mid"], (((1,), (1,)), ((), ())),
                                    preferred_element_type=F32)
                ch["scores"] = jnp.where(ch["keep"], s, 0.0).astype(BF16)
                ch["kv"] = jnp.dot(ch["k_end_t"], ch["v"], preferred_element_type=F32)

    state = {(di, h): d[5][h] for di, d in enumerate(dirs) for h in range(HEADS)}

    def tail(step):
        for di, d in enumerate(dirs):
            o_ref = d[6]
            for h in range(HEADS):
                ch = chains.pop((step, di, h))
                s_prev = state[(di, h)]
                lhs = jnp.concatenate([ch["scores"], ch["q_dec"]], axis=1)
                rhs = jnp.concatenate([ch["v"], s_prev.astype(BF16)], axis=0)
                o = jnp.dot(lhs, rhs, preferred_element_type=F32)
                if o_ref is not None:
                    o_ref[ch["rows"], h * DV:(h + 1) * DV] = o.astype(o_ref.dtype)
                dec = jnp.concatenate([ch["dec_col"]] * (DV // DK), axis=1)
                state[(di, h)] = s_prev * dec + ch["kv"]

    front(0)
    for step in range(n_sub):
        if step + 1 < n_sub:
            front(step + 1)
        middle(step)
        tail(step)
    for (di, h), s in state.items():
        dirs[di][5][h] = s


def _rider_specs(weights, n_steps, flat_step):
    specs, shapes = [], []
    for w in weights:
        rows, cols = w.shape
        assert rows % (16 * n_steps) == 0
        specs.append(pl.BlockSpec((rows // n_steps, cols), lambda *idx: (flat_step(*idx), 0)))
        shapes.append(jax.ShapeDtypeStruct(w.shape, BF16))
    return specs, list(specs), shapes


def _cast_riders(in_refs, out_refs):
    for src, dst in zip(in_refs, out_refs):
        dst[...] = src[...].astype(dst.dtype)


def _gla_kernel(*refs, emit_o, n_riders):
    (qkf, vf, lrf, qkb, vb, lrb, wf, bf, wb, bb, s0f, s0b) = refs[:12]
    rider_in, refs = refs[12:12 + n_riders], refs[12 + n_riders:]
    if emit_o:
        of, ob, sf_out, sb_out = refs[:4]
        rider_out = refs[4:4 + n_riders]
    else:
        sf_out, sb_out = refs[:2]
        rider_out = refs[2:2 + n_riders]
        of = ob = None
    s_scr = refs[-1]
    _cast_riders(rider_in, rider_out)
    i = pl.program_id(1)

    @pl.when(i == 0)
    def _():
        s_scr[0] = s0f[...]
        s_scr[1] = s0b[...]

    _gla_chunks([(qkf, vf, lrf, wf, bf, s_scr.at[0], of, False),
                 (qkb, vb, lrb, wb, bb, s_scr.at[1], ob, True)], GLA_CHUNK)

    @pl.when(i == pl.num_programs(1) - 1)
    def _():
        sf_out[...] = s_scr[0]
        sb_out[...] = s_scr[1]


def _gla(p3, lr3, wlr_f, blr_f, wlr_b, blr_b, s0f, s0b, emit_o, riders=()):
    b, t, _ = p3.shape
    per_step = max(s for s in range(1, GLA_CHUNKS_PER_STEP + 1) if t % (GLA_CHUNK * s) == 0)
    c = GLA_CHUNK * per_step
    n = t // c
    fwd = lambda blk: (lambda bi, i: (bi, i, blk))
    bwd = lambda blk: (lambda bi, i: (bi, n - 1 - i, blk))

    def seq_specs(mk):
        return [pl.BlockSpec((None, c, 2 * KEY_W), mk(0)),
                pl.BlockSpec((None, c, VAL_W), mk(1)),
                pl.BlockSpec((None, c, LR_PAD), mk(0))]
    full2 = lambda shape: pl.BlockSpec(shape, lambda bi, i: (0, 0))
    st_spec = pl.BlockSpec((None, HEADS, DK, DV), lambda bi, i: (bi, 0, 0, 0))
    in_specs = (seq_specs(fwd) + seq_specs(bwd)
                + [full2(wlr_f.shape), full2(blr_f.shape), full2(wlr_b.shape), full2(blr_b.shape),
                   st_spec, st_spec])
    st_shape = jax.ShapeDtypeStruct((b, HEADS, DK, DV), F32)
    out_shape = [st_shape, st_shape]
    out_specs = [st_spec, st_spec]
    if emit_o:
        o_shape = jax.ShapeDtypeStruct((b, t, VAL_W), BF16)
        out_shape = [o_shape, o_shape] + out_shape
        out_specs = [pl.BlockSpec((None, c, VAL_W), fwd(0)), pl.BlockSpec((None, c, VAL_W), bwd(0))] + out_specs
    r_in, r_out, r_shapes = _rider_specs(riders, b * n, lambda bi, i: bi * n + i)
    return pl.pallas_call(
        functools.partial(_gla_kernel, emit_o=emit_o, n_riders=len(riders)),
        out_shape=tuple(out_shape + r_shapes),
        grid=(b, n),
        in_specs=in_specs + r_in,
        out_specs=tuple(out_specs + r_out),
        scratch_shapes=[pltpu.VMEM((2, HEADS, DK, DV), F32)],
        compiler_params=_cparams(("arbitrary", "arbitrary")),
        name="gla_seq" if emit_o else "gla_ctx",
    )(p3, p3, lr3, p3, p3, lr3, wlr_f, blr_f, wlr_b, blr_b, s0f, s0b, *riders)


def _cadd(a, b):
    return a[0] + b[0], a[1] + b[1]


def _csub(a, b):
    return a[0] - b[0], a[1] - b[1]


def _cmul_neg_i(a):
    return a[1], -a[0]


def _dft4(y):
    t0, t1 = _cadd(y[0], y[2]), _csub(y[0], y[2])
    t2, t3 = _cadd(y[1], y[3]), _cmul_neg_i(_csub(y[1], y[3]))
    return [_cadd(t0, t2), _cadd(t1, t3), _csub(t0, t2), _csub(t1, t3)]


def _dft8(z):
    r = math.sqrt(0.5)
    s = [_cadd(z[a], z[a + 4]) for a in range(4)]
    d = [_csub(z[a], z[a + 4]) for a in range(4)]
    d1 = ((d[1][0] + d[1][1]) * r, (d[1][1] - d[1][0]) * r)
    d2 = _cmul_neg_i(d[2])
    d3 = ((d[3][1] - d[3][0]) * r, (-d[3][1] - d[3][0]) * r)
    ev = _dft4(s)
    od = _dft4([d[0], d1, d2, d3])
    out = [None] * 8
    for j in range(4):
        out[2 * j] = ev[j]
        out[2 * j + 1] = od[j]
    return out


def _fft_kernel(*refs, n_slabs, n_riders):
    u_refs = refs[:n_slabs]
    f_ref, twc_ref, tws_ref = refs[n_slabs:n_slabs + 3]
    rider_in = refs[n_slabs + 3:n_slabs + 3 + n_riders]
    xr_ref, xi_ref = refs[n_slabs + 3 + n_riders:n_slabs + 5 + n_riders]
    rider_out = refs[n_slabs + 5 + n_riders:n_slabs + 5 + 2 * n_riders]
    wide_scr, ub_scr = refs[-2:]
    _cast_riders(rider_in, rider_out)
    m = ub_scr.shape[1]

    @pl.when(pl.program_id(2) == 0)
    def _():
        for s in range(n_slabs):
            wide_scr[s] = u_refs[s][...].astype(F32)
        for a in range(RADIX):
            rows = [wide_scr[s, pl.ds(a, m, stride=RADIX), :].astype(BF16) for s in range(n_slabs)]
            ub_scr[a] = jnp.concatenate(rows, axis=1)

    t = f_ref.shape[0] // 2
    f = f_ref[...]
    z = []
    for a in range(RADIX):
        za = jnp.dot(f, ub_scr[a], preferred_element_type=F32)
        zr, zi = za[:t], za[t:]
        cc, ss = twc_ref[a], tws_ref[a]
        z.append((zr * cc + zi * ss, zi * cc - zr * ss))
    x = _dft8(z)
    for k1 in range(RADIX):
        xr_ref[k1] = x[k1][0].astype(xr_ref.dtype)
        xi_ref[k1] = x[k1][1].astype(xi_ref.dtype)


def _fft(p3, col0, fmat, twc, tws, riders=()):
    b, seq, _ = p3.shape
    w = FWIDTH
    m = seq // RADIX
    nt, t2, _ = fmat.shape
    t = t2 // 2
    chb = 256
    n_slabs = chb // LANES
    slab0 = col0 // LANES
    out = jax.ShapeDtypeStruct((b, RADIX, m, w), BF16)
    o_spec = pl.BlockSpec((None, RADIX, t, chb), lambda bi, cj, kt: (bi, 0, kt, cj))
    slab = lambda s: pl.BlockSpec((None, seq, LANES), lambda bi, cj, kt: (bi, 0, slab0 + cj * n_slabs + s))
    ncj = w // chb
    r_in, r_out, r_shapes = _rider_specs(riders, b * ncj * nt, lambda bi, cj, kt: (bi * ncj + cj) * nt + kt)
    return pl.pallas_call(
        functools.partial(_fft_kernel, n_slabs=n_slabs, n_riders=len(riders)),
        out_shape=tuple([out, out] + r_shapes),
        grid=(b, ncj, nt),
        in_specs=[slab(s) for s in range(n_slabs)] + [
                  pl.BlockSpec((None, t2, m), lambda bi, cj, kt: (kt, 0, 0)),
                  pl.BlockSpec((RADIX, t, 1), lambda bi, cj, kt: (0, kt, 0)),
                  pl.BlockSpec((RADIX, t, 1), lambda bi, cj, kt: (0, kt, 0))] + r_in,
        out_specs=tuple([o_spec, o_spec] + r_out),
        scratch_shapes=[pltpu.VMEM((n_slabs, seq, LANES), F32), pltpu.VMEM((RADIX, m, chb), BF16)],
        compiler_params=_cparams(("arbitrary", "arbitrary", "arbitrary")),
        name="fft",
    )(*([p3] * n_slabs), fmat, twc, tws, *riders)


def _merge_kernel(of_ref, ob_ref, g_ref, gates_ref, xr_ref, xi_ref, x_ref, rt_ref, ct_ref, gt_ref,
                  gng_ref, wgo_ref, wfo_ref, wo_ref, csg_ref, o_ref):
    o = of_ref[...].astype(F32) + ob_ref[...].astype(F32)
    g = g_ref[...].astype(F32)
    gng = gng_ref[...]
    heads = []
    for h in range(HEADS):
        sl = slice(h * DV, (h + 1) * DV)
        heads.append((_rms(o[:, sl], gng) * _silu(g[:, sl])).astype(BF16))
    y_gla = jnp.dot(jnp.concatenate(heads, axis=1), wgo_ref[...], preferred_element_type=F32)

    csg = csg_ref[...]
    groups = []
    for gi in range(FGROUPS):
        sl = slice(gi * FGDIM, (gi + 1) * FGDIM)
        xg = jnp.concatenate([xr_ref[:, sl], xi_ref[:, sl]], axis=1)
        groups.append(jnp.dot(xg, csg, preferred_element_type=F32).astype(BF16))
    y_fft = jnp.dot(jnp.concatenate(groups, axis=1), wfo_ref[...], preferred_element_type=F32)

    gates = jax.nn.sigmoid(gates_ref[...].astype(F32))
    d = y_fft.shape[1]
    zmix = (gates[:, :d] * y_fft + gates[:, d:] * y_gla).astype(BF16)
    y = jnp.dot(zmix, wo_ref[...], preferred_element_type=F32)
    o_ref[...] = _add_pos(x_ref[...], rt_ref, ct_ref) + gt_ref[...] * y


def _const_spec(shape):
    zeros = (0,) * len(shape)
    return pl.BlockSpec(shape, lambda *idx: zeros, pipeline_mode=pl.Buffered(1))


def _merge(o_f, o_b, p2, xr, xi, x2, rt3, ct, mod3, gng, wgo, wfo, wo, csg, t):
    ntok, d = x2.shape
    tm = 256
    tiles_per_seq = t // tm
    rpt = tm // GRID_W
    row = lambda blk: (lambda i: (i, blk))
    in_specs = [pl.BlockSpec((tm, VAL_W), row(0)),
                pl.BlockSpec((tm, VAL_W), row(0)),
                pl.BlockSpec((tm, VAL_W), row(2)),
                pl.BlockSpec((tm, 2 * d), row(1)),
                pl.BlockSpec((tm, FWIDTH), row(0)),
                pl.BlockSpec((tm, FWIDTH), row(0)),
                pl.BlockSpec((tm, d), row(0)),
                pl.BlockSpec((rpt, 1, d // 2), lambda i: (i % tiles_per_seq, 0, 0)),
                _const_spec((GRID_W, d // 2)),
                pl.BlockSpec((None, 1, d), lambda i: (i // tiles_per_seq, 0, 2)),
                _const_spec(gng.shape), _const_spec(wgo.shape), _const_spec(wfo.shape),
                _const_spec(wo.shape), _const_spec(csg.shape)]
    return pl.pallas_call(
        _merge_kernel,
        out_shape=jax.ShapeDtypeStruct((ntok, d), F32),
        grid=(ntok // tm,),
        in_specs=in_specs,
        out_specs=pl.BlockSpec((tm, d), row(0)),
        compiler_params=_cparams(("arbitrary",)),
        name="merge",
    )(o_f, o_b, p2, p2, xr, xi, x2, rt3, ct, mod3, gng, wgo, wfo, wo, csg)


def _mlp_kernel(x_ref, xn_ref, sh_ref, sc_ref, shn_ref, scn_ref, gt_ref, g2_ref, w1_ref, w2_ref, fg_ref,
                o_ref, h_even, h_odd):
    i, j = pl.program_id(0), pl.program_id(1)
    nf = pl.num_programs(1)
    tm = x_ref.shape[0]
    slab = tm // nf
    slot = i % 2

    def normed(x, sh, sc):
        return (_rms(x, g2_ref[...]) * (1.0 + sc) + sh).astype(BF16)

    @pl.when((i == 0) & (j == 0))
    def _():
        h_even[...] = normed(x_ref[...], sh_ref[...], sc_ref[...])

    @pl.when(j == 0)
    def _():
        o_ref[...] = jnp.zeros_like(o_ref)

    def step(h_cur, h_next):
        r0 = pl.multiple_of(j * slab, slab)
        h_next[pl.ds(r0, slab), :] = normed(xn_ref[pl.ds(r0, slab), :], shn_ref[...], scn_ref[...])
        hid = jnp.dot(h_cur[...], w1_ref[...], preferred_element_type=F32)
        hid = jnp.square(jnp.maximum(hid, 0.0)).astype(BF16)
        o_ref[...] += jnp.dot(hid, w2_ref[...], preferred_element_type=F32)

    @pl.when(slot == 0)
    def _():
        step(h_even, h_odd)

    @pl.when(slot == 1)
    def _():
        step(h_odd, h_even)

    @pl.when(j == nf - 1)
    def _():
        xo = x_ref[...] + gt_ref[...] * o_ref[...]
        o_ref[...] = _rms(xo, fg_ref[...])


def _mlp(x2, mod3, g2, w1, w2, fg, t):
    ntok, d = x2.shape
    dff = w1.shape[1]
    tm, tf = 512, 1024
    assert (tm // (dff // tf)) % 8 == 0
    nt = ntok // tm
    tiles_per_seq = t // tm
    nxt = lambda i: jnp.minimum(i + 1, nt - 1)
    modspec = lambda blk: pl.BlockSpec((None, 1, d), lambda i, j: (i // tiles_per_seq, 0, blk))
    modspec_next = lambda blk: pl.BlockSpec((None, 1, d), lambda i, j: (nxt(i) // tiles_per_seq, 0, blk))
    return pl.pallas_call(
        _mlp_kernel,
        out_shape=jax.ShapeDtypeStruct((ntok, d), F32),
        grid=(nt, dff // tf),
        in_specs=[pl.BlockSpec((tm, d), lambda i, j: (jnp.where(j >= 2, i, jnp.maximum(i - 1, 0)), 0)),
                  pl.BlockSpec((tm, d), lambda i, j: (nxt(i), 0)),
                  modspec(3), modspec(4), modspec_next(3), modspec_next(4), modspec(5),
                  pl.BlockSpec((1, d), lambda i, j: (0, 0)),
                  pl.BlockSpec((d, tf), lambda i, j: (0, j)),
                  pl.BlockSpec((tf, d), lambda i, j: (j, 0)),
                  pl.BlockSpec((1, d), lambda i, j: (0, 0))],
        out_specs=pl.BlockSpec((tm, d), lambda i, j: (i, 0)),
        scratch_shapes=[pltpu.VMEM((tm, d), BF16), pltpu.VMEM((tm, d), BF16)],
        compiler_params=_cparams(("arbitrary", "arbitrary")),
        name="mlp",
    )(x2, x2, mod3, mod3, mod3, mod3, mod3, g2, w1, w2, fg)


def _pos_tables(t, d):
    quarter = d // 4
    omega = 1.0 / (POS_TEMP ** (jnp.arange(quarter, dtype=F32) / quarter))
    er = jnp.arange(t // GRID_W, dtype=F32)[:, None] * omega[None, :]
    ec = jnp.arange(GRID_W, dtype=F32)[:, None] * omega[None, :]
    rt = jnp.concatenate([jnp.sin(er), jnp.cos(er)], axis=-1)
    ct = jnp.concatenate([jnp.sin(ec), jnp.cos(ec)], axis=-1)
    return rt[:, None, :], ct


def _dft_tables(t):
    m = t // RADIX
    tile = min(256, m)
    k = np.arange(m)
    ang = 2.0 * np.pi * ((k[:, None] * k[None, :]) % m) / m
    cos_t = np.cos(ang).reshape(m // tile, tile, m)
    sin_t = np.sin(ang).reshape(m // tile, tile, m)
    fmat = np.concatenate([cos_t, -sin_t], axis=1)
    a = np.arange(RADIX)
    tw = 2.0 * np.pi * (a[:, None] * k[None, :]) / t
    scale = 1.0 / math.sqrt(t * FGDIM)
    twc = (np.cos(tw) * scale)[:, :, None]
    tws = (np.sin(tw) * scale)[:, :, None]
    c = np.arange(FGDIM)
    cang = 2.0 * np.pi * ((c[:, None] * c[None, :]) % FGDIM) / FGDIM
    csg = np.concatenate([np.cos(cang), np.sin(cang)], axis=0)
    as_f32 = lambda a: jnp.asarray(a.astype(np.float32))
    return as_f32(fmat).astype(BF16), as_f32(twc), as_f32(tws), as_f32(csg).astype(BF16)


def _pad_lr_weight(w_lr, row0):
    out = jnp.zeros((LR_PAD, KEY_W), F32)
    return out.at[row0:row0 + RANK].set(w_lr).astype(BF16)


def kernel(x, c, ctx, c_ctx, w_mod, b_mod, norm1_g, norm2_g, w_in, w_lr_f, b_lr_f, w_lr_b, b_lr_b,
           gla_norm_g, w_fourier_out, w_gla_out, w_out, w_mlp_in, w_mlp_out, final_norm_g):
    b, t, d = x.shape
    tc = ctx.shape[1]
    depth = w_mod.shape[0]
    assert depth == 1 and d == D_MODEL and t % (RADIX * GRID_W) == 0 and tc % GLA_CHUNK == 0
    li = 0

    w_main, w_lr = _wprep(w_in[li].T)
    wlr_f, wlr_b = _pad_lr_weight(w_lr_f[li], 0), _pad_lr_weight(w_lr_b[li], RANK)
    blr_f, blr_b = b_lr_f[li][None, :], b_lr_b[li][None, :]

    rt3, ct = _pos_tables(t, d)
    fmat, twc, tws, csg = _dft_tables(t)

    rows = 8
    cpad = jnp.concatenate([c, c_ctx[None, :], jnp.zeros((rows - b - 1, d), F32)], axis=0)
    mod3 = _mod(cpad.T, b + 1, w_mod[li], b_mod[li][None, :]).reshape(rows, 1, N_MOD * d)

    tm_ctx = min(512, b * tc)
    qkv_w = 2 * KEY_W + VAL_W
    p_ctx, lr_ctx = _inproj(ctx.reshape(b * tc, d), mod3, lambda i: b, norm1_g[li][None, :],
                            w_main, w_lr, None, tm_ctx, qkv_w)
    s_zero = jnp.zeros((b, HEADS, DK, DV), F32)
    s_f, s_b = _gla(p_ctx.reshape(b, tc, qkv_w), lr_ctx.reshape(b, tc, LR_PAD),
                    wlr_f, blr_f, wlr_b, blr_b, s_zero, s_zero, emit_o=False)

    tm = min(1024, t)
    tiles = t // tm
    x2 = x.reshape(b * t, d)
    p, lr = _inproj(x2, mod3, lambda i: i // tiles, norm1_g[li][None, :], w_main, w_lr,
                    (rt3, ct, tiles), tm, P_WIDTH)
    p3 = p.reshape(b, t, P_WIDTH)
    o_f, o_b, _, _, w1, wgo, wfo, wo = _gla(
        p3, lr.reshape(b, t, LR_PAD), wlr_f, blr_f, wlr_b, blr_b, s_f, s_b, emit_o=True,
        riders=(w_mlp_in[li], w_gla_out[li], w_fourier_out[li], w_out[li]))

    xr, xi, w2 = _fft(p3, QKVG_W, fmat, twc, tws, riders=(w_mlp_out[li],))

    x1 = _merge(o_f.reshape(b * t, VAL_W), o_b.reshape(b * t, VAL_W), p,
                xr.reshape(b * t, FWIDTH), xi.reshape(b * t, FWIDTH), x2, rt3, ct, mod3,
                gla_norm_g[li][None, :], wgo, wfo, wo, csg, t)

    out = _mlp(x1, mod3, norm2_g[li][None, :], w1, w2, final_norm_g[None, :], t)
    return out.reshape(b, t, d)
```

```python
import functools
import math

import jax
import jax.numpy as jnp
import numpy as np
from jax import lax
from jax.experimental import pallas as pl
from jax.experimental.pallas import tpu as pltpu

F32 = jnp.float32
BF16 = jnp.bfloat16

D_MODEL = 2048
GRID_W = 64
HEADS = 4
DK = 128
DV = 256
KEY_W = HEADS * DK
VAL_W = HEADS * DV
RANK = 16
TAU = 16.0
FGROUPS = 4
FGDIM = 256
FWIDTH = FGROUPS * FGDIM
D_FF = 4 * D_MODEL
N_MOD = 6
EPS = 1e-6
POS_TEMP = 10000.0

QKVG_W = 2 * KEY_W + 2 * VAL_W
P_WIDTH = QKVG_W + FWIDTH + 2 * D_MODEL
LR_PAD = 128
GLA_CHUNK = 128
GLA_CHUNKS_PER_STEP = 4
RADIX = 8
LANES = 128

V7X_VMEM_LIMIT = 56 * 1024 * 1024
BIG_TILE_VMEM_LIMIT = 63 * 1024 * 1024
MLP_CHUNK = 1024


def _cparams(sem, vmem=V7X_VMEM_LIMIT):
    return pltpu.CompilerParams(dimension_semantics=sem, vmem_limit_bytes=vmem)


def _silu(x):
    return x * jax.nn.sigmoid(x)


def _mod_kernel(ct_ref, w_ref, b_ref, o_ref, *, n_used):
    st = _silu(ct_ref[...])
    w = w_ref[...]
    bias = b_ref[...]
    rows = [jnp.sum(w * st[:, m:m + 1], axis=0, keepdims=True) + bias for m in range(n_used)]
    rows += [bias] * (o_ref.shape[0] - n_used)
    o_ref[...] = jnp.concatenate(rows, axis=0)


def _mod(cpad_t, n_used, w_mod, b_mod):
    d, rows = cpad_t.shape
    n = w_mod.shape[1]
    tn = 1024
    return pl.pallas_call(
        functools.partial(_mod_kernel, n_used=n_used),
        out_shape=jax.ShapeDtypeStruct((rows, n), F32),
        grid=(n // tn,),
        in_specs=[pl.BlockSpec((d, rows), lambda j: (0, 0)),
                  pl.BlockSpec((d, tn), lambda j: (0, j)),
                  pl.BlockSpec((1, tn), lambda j: (0, j))],
        out_specs=pl.BlockSpec((rows, tn), lambda j: (0, j)),
        compiler_params=_cparams(("arbitrary",)),
        name="mod",
    )(cpad_t, w_mod, b_mod)


def _wprep_kernel(wt_ref, lrt_ref, wm_ref, wlr_ref):
    wm_ref[...] = wt_ref[...].T.astype(BF16)

    @pl.when(pl.program_id(0) == 0)
    def _():
        lrt = lrt_ref[...]
        r = lax.broadcasted_iota(jnp.int32, lrt.shape, 0)
        wlr_ref[...] = jnp.where(r < 2 * RANK, lrt, 0.0).T.astype(BF16)


def _wprep(w_in_t):
    n, d = w_in_t.shape
    tc = 512
    n_before = QKVG_W // tc

    def src_row(j):
        return pl.multiple_of(jnp.where(j >= n_before, j * tc + 2 * RANK, j * tc), 2 * RANK)

    return pl.pallas_call(
        _wprep_kernel,
        out_shape=(jax.ShapeDtypeStruct((d, P_WIDTH), BF16), jax.ShapeDtypeStruct((d, LR_PAD), BF16)),
        grid=(P_WIDTH // tc,),
        in_specs=[pl.BlockSpec((pl.Element(tc), pl.Element(d)), lambda j: (src_row(j), 0)),
                  pl.BlockSpec((LR_PAD, d), lambda j: (QKVG_W // LR_PAD, 0))],
        out_specs=(pl.BlockSpec((d, tc), lambda j: (0, j)), pl.BlockSpec((d, LR_PAD), lambda j: (0, 0))),
        compiler_params=_cparams(("arbitrary",)),
        name="wprep",
    )(w_in_t, w_in_t)


def _add_pos(x, rt_ref, ct_ref):
    tm, d = x.shape
    x3 = x.reshape(tm // GRID_W, GRID_W, d)
    half = d // 2
    lo = x3[:, :, :half] + rt_ref[...]
    hi = x3[:, :, half:] + ct_ref[...][None]
    return jnp.concatenate([lo, hi], axis=-1).reshape(tm, d)


def _rms(x, g):
    return x * lax.rsqrt(jnp.mean(x * x, axis=-1, keepdims=True) + EPS) * g


def _inproj_kernel(*refs, add_pos):
    if add_pos:
        x_ref, rt_ref, ct_ref, sh_ref, sc_ref, g_ref, w_ref, wlr_ref, p_ref, lr_ref, h_scr = refs
    else:
        x_ref, sh_ref, sc_ref, g_ref, w_ref, wlr_ref, p_ref, lr_ref, h_scr = refs

    @pl.when(pl.program_id(1) == 0)
    def _():
        x = x_ref[...]
        if add_pos:
            x = _add_pos(x, rt_ref, ct_ref)
        h = _rms(x, g_ref[...]) * (1.0 + sc_ref[...]) + sh_ref[...]
        hb = h.astype(BF16)
        h_scr[...] = hb
        lr_ref[...] = jnp.dot(hb, wlr_ref[...], preferred_element_type=F32)

    p_ref[...] = jnp.dot(h_scr[...], w_ref[...], preferred_element_type=F32).astype(BF16)


def _inproj(x2, mod3, mod_row_of_tile, norm_g, w_main, w_lr, pos_tabs, tm, width):
    ntok, d = x2.shape
    tn = 2048
    add_pos = pos_tabs is not None
    in_specs = [pl.BlockSpec((tm, d), lambda i, j: (i, 0))]
    args = [x2]
    if add_pos:
        rt3, ct, tiles_per_seq = pos_tabs
        rpt = tm // GRID_W
        in_specs += [pl.BlockSpec((rpt, 1, d // 2), lambda i, j: (i % tiles_per_seq, 0, 0)),
                     pl.BlockSpec((GRID_W, d // 2), lambda i, j: (0, 0))]
        args += [rt3, ct]
    in_specs += [pl.BlockSpec((None, 1, d), lambda i, j: (mod_row_of_tile(i), 0, 0)),
                 pl.BlockSpec((None, 1, d), lambda i, j: (mod_row_of_tile(i), 0, 1)),
                 pl.BlockSpec((1, d), lambda i, j: (0, 0)),
                 pl.BlockSpec((d, tn), lambda i, j: (0, j)),
                 pl.BlockSpec((d, LR_PAD), lambda i, j: (0, 0))]
    args += [mod3, mod3, norm_g, w_main, w_lr]
    return pl.pallas_call(
        functools.partial(_inproj_kernel, add_pos=add_pos),
        out_shape=(jax.ShapeDtypeStruct((ntok, width), BF16),
                   jax.ShapeDtypeStruct((ntok, LR_PAD), F32)),
        grid=(ntok // tm, width // tn),
        in_specs=in_specs,
        out_specs=(pl.BlockSpec((tm, tn), lambda i, j: (i, j)),
                   pl.BlockSpec((tm, LR_PAD), lambda i, j: (i, 0))),
        scratch_shapes=[pltpu.VMEM((tm, d), BF16)],
        compiler_params=_cparams(("arbitrary", "arbitrary"), vmem=BIG_TILE_VMEM_LIMIT),
        name="inproj_pos" if add_pos else "inproj_ctx",
    )(*args)


def _log_sigmoid(z):
    return jnp.minimum(z, 0.0) - jnp.log1p(jnp.exp(-jnp.abs(z)))


def _gla_chunks(dirs, c):
    n_sub = dirs[0][0].shape[0] // c
    row = lax.broadcasted_iota(jnp.int32, (c, c), 0)
    col = lax.broadcasted_iota(jnp.int32, (c, c), 1)

    chains = {}
    log2_qscale = math.log2(DK ** -0.5)

    def front(step):
        for di, (qk_ref, v_ref, lr_ref, wlr_ref, blr_ref, s_scr, o_ref, backward) in enumerate(dirs):
            keep = (row <= col) if backward else (row >= col)
            tri = jnp.where(keep, 1.0, 0.0).astype(BF16)
            last = 0 if backward else c - 1
            mid = c // 2 if backward else c // 2 - 1
            sub = n_sub - 1 - step if backward else step
            rows = slice(sub * c, (sub + 1) * c)
            z = jnp.dot(lr_ref[rows, :].astype(BF16), wlr_ref[...], preferred_element_type=F32) + blr_ref[...]
            la = _log_sigmoid(z) * (math.log2(math.e) / TAU)
            la_hi = la.astype(BF16)
            la_lo = (la - la_hi.astype(F32)).astype(BF16)
            cum2 = jnp.dot(tri, jnp.concatenate([la_hi, la_lo], axis=1), preferred_element_type=F32)
            cum_all = cum2[:, :KEY_W] + cum2[:, KEY_W:]
            for h in range(HEADS):
                cum = cum_all[:, h * DK:(h + 1) * DK]
                tot = cum[last:last + 1, :]
                ref_pt = cum[mid:mid + 1, :]
                q = qk_ref[rows, h * DK:(h + 1) * DK].astype(F32)
                k = qk_ref[rows, KEY_W + h * DK:KEY_W + (h + 1) * DK].astype(F32)
                chains[(step, di, h)] = dict(
                    keep=keep, rows=rows,
                    v=v_ref[rows, h * DV:(h + 1) * DV],
                    q_mid=(q * jnp.exp2(cum - (ref_pt - log2_qscale))).astype(BF16),
                    k_mid=(k * jnp.exp2(ref_pt - cum)).astype(BF16),
                    q_dec=(q * jnp.exp2(cum + log2_qscale)).astype(BF16),
                    k_end_t=(k * jnp.exp2(tot - cum)).T.astype(BF16),
                    dec_col=jnp.broadcast_to(jnp.exp2(tot), (DK, DK)).T)

    def middle(step):
        for di in range(len(dirs)):
            for h in range(HEADS):
                ch = chains[(step, di, h)]
                s = lax.dot_general(ch["q_mid"], ch["k_mid"], (((1,), (1,)), ((), ())),
                                    preferred_element_type=F32)
                ch["scores"] = jnp.where(ch["keep"], s, 0.0).astype(BF16)
                ch["kv"] = jnp.dot(ch["k_end_t"], ch["v"], preferred_element_type=F32)

    state = {(di, h): d[5][h] for di, d in enumerate(dirs) for h in range(HEADS)}

    def tail(step):
        for di, d in enumerate(dirs):
            o_ref = d[6]
            for h in range(HEADS):
                ch = chains.pop((step, di, h))
                s_prev = state[(di, h)]
                lhs = jnp.concatenate([ch["scores"], ch["q_dec"]], axis=1)
                rhs = jnp.concatenate([ch["v"], s_prev.astype(BF16)], axis=0)
                o = jnp.dot(lhs, rhs, preferred_element_type=F32)
                if o_ref is not None:
                    o_ref[ch["rows"], h * DV:(h + 1) * DV] = o.astype(o_ref.dtype)
                dec = jnp.concatenate([ch["dec_col"]] * (DV // DK), axis=1)
                state[(di, h)] = s_prev * dec + ch["kv"]

    front(0)
    for step in range(n_sub):
        if step + 1 < n_sub:
            front(step + 1)
        middle(step)
        tail(step)
    for (di, h), s in state.items():
        dirs[di][5][h] = s


def _rider_specs(weights, n_steps, flat_step):
    specs, shapes = [], []
    for w in weights:
        rows, cols = w.shape
        assert rows % (16 * n_steps) == 0
        specs.append(pl.BlockSpec((rows // n_steps, cols), lambda *idx: (flat_step(*idx), 0)))
        shapes.append(jax.ShapeDtypeStruct(w.shape, BF16))
    return specs, list(specs), shapes


def _cast_riders(in_refs, out_refs):
    for src, dst in zip(in_refs, out_refs):
        dst[...] = src[...].astype(dst.dtype)


def _gla_kernel(*refs, emit_o, n_riders):
    (qkf, vf, lrf, qkb, vb, lrb, wf, bf, wb, bb, s0f, s0b) = refs[:12]
    rider_in, refs = refs[12:12 + n_riders], refs[12 + n_riders:]
    if emit_o:
        of, ob, sf_out, sb_out = refs[:4]
        rider_out = refs[4:4 + n_riders]
    else:
        sf_out, sb_out = refs[:2]
        rider_out = refs[2:2 + n_riders]
        of = ob = None
    s_scr = refs[-1]
    _cast_riders(rider_in, rider_out)
    i = pl.program_id(1)

    @pl.when(i == 0)
    def _():
        s_scr[0] = s0f[...]
        s_scr[1] = s0b[...]

    _gla_chunks([(qkf, vf, lrf, wf, bf, s_scr.at[0], of, False),
                 (qkb, vb, lrb, wb, bb, s_scr.at[1], ob, True)], GLA_CHUNK)

    @pl.when(i == pl.num_programs(1) - 1)
    def _():
        sf_out[...] = s_scr[0]
        sb_out[...] = s_scr[1]


def _gla(p3, lr3, wlr_f, blr_f, wlr_b, blr_b, s0f, s0b, emit_o, riders=()):
    b, t, _ = p3.shape
    per_step = max(s for s in range(1, GLA_CHUNKS_PER_STEP + 1) if t % (GLA_CHUNK * s) == 0)
    c = GLA_CHUNK * per_step
    n = t // c
    fwd = lambda blk: (lambda bi, i: (bi, i, blk))
    bwd = lambda blk: (lambda bi, i: (bi, n - 1 - i, blk))

    def seq_specs(mk):
        return [pl.BlockSpec((None, c, 2 * KEY_W), mk(0)),
                pl.BlockSpec((None, c, VAL_W), mk(1)),
                pl.BlockSpec((None, c, LR_PAD), mk(0))]
    full2 = lambda shape: pl.BlockSpec(shape, lambda bi, i: (0, 0))
    st_spec = pl.BlockSpec((None, HEADS, DK, DV), lambda bi, i: (bi, 0, 0, 0))
    in_specs = (seq_specs(fwd) + seq_specs(bwd)
                + [full2(wlr_f.shape), full2(blr_f.shape), full2(wlr_b.shape), full2(blr_b.shape),
                   st_spec, st_spec])
    st_shape = jax.ShapeDtypeStruct((b, HEADS, DK, DV), F32)
    out_shape = [st_shape, st_shape]
    out_specs = [st_spec, st_spec]
    if emit_o:
        o_shape = jax.ShapeDtypeStruct((b, t, VAL_W), BF16)
        out_shape = [o_shape, o_shape] + out_shape
        out_specs = [pl.BlockSpec((None, c, VAL_W), fwd(0)), pl.BlockSpec((None, c, VAL_W), bwd(0))] + out_specs
    r_in, r_out, r_shapes = _rider_specs(riders, b * n, lambda bi, i: bi * n + i)
    return pl.pallas_call(
        functools.partial(_gla_kernel, emit_o=emit_o, n_riders=len(riders)),
        out_shape=tuple(out_shape + r_shapes),
        grid=(b, n),
        in_specs=in_specs + r_in,
        out_specs=tuple(out_specs + r_out),
        scratch_shapes=[pltpu.VMEM((2, HEADS, DK, DV), F32)],
        compiler_params=_cparams(("arbitrary", "arbitrary")),
        name="gla_seq" if emit_o else "gla_ctx",
    )(p3, p3, lr3, p3, p3, lr3, wlr_f, blr_f, wlr_b, blr_b, s0f, s0b, *riders)


def _cadd(a, b):
    return a[0] + b[0], a[1] + b[1]


def _csub(a, b):
    return a[0] - b[0], a[1] - b[1]


def _cmul_neg_i(a):
    return a[1], -a[0]


def _dft4(y):
    t0, t1 = _cadd(y[0], y[2]), _csub(y[0], y[2])
    t2, t3 = _cadd(y[1], y[3]), _cmul_neg_i(_csub(y[1], y[3]))
    return [_cadd(t0, t2), _cadd(t1, t3), _csub(t0, t2), _csub(t1, t3)]


def _dft8(z):
    r = math.sqrt(0.5)
    s = [_cadd(z[a], z[a + 4]) for a in range(4)]
    d = [_csub(z[a], z[a + 4]) for a in range(4)]
    d1 = ((d[1][0] + d[1][1]) * r, (d[1][1] - d[1][0]) * r)
    d2 = _cmul_neg_i(d[2])
    d3 = ((d[3][1] - d[3][0]) * r, (-d[3][1] - d[3][0]) * r)
    ev = _dft4(s)
    od = _dft4([d[0], d1, d2, d3])
    out = [None] * 8
    for j in range(4):
        out[2 * j] = ev[j]
        out[2 * j + 1] = od[j]
    return out


def _fft_kernel(*refs, n_slabs, n_riders):
    u_refs = refs[:n_slabs]
    f_ref, twc_ref, tws_ref = refs[n_slabs:n_slabs + 3]
    rider_in = refs[n_slabs + 3:n_slabs + 3 + n_riders]
    xr_ref, xi_ref = refs[n_slabs + 3 + n_riders:n_slabs + 5 + n_riders]
    rider_out = refs[n_slabs + 5 + n_riders:n_slabs + 5 + 2 * n_riders]
    wide_scr, ub_scr = refs[-2:]
    _cast_riders(rider_in, rider_out)
    m = ub_scr.shape[1]

    @pl.when(pl.program_id(2) == 0)
    def _():
        for s in range(n_slabs):
            wide_scr[s] = u_refs[s][...].astype(F32)
        for a in range(RADIX):
            rows = [wide_scr[s, pl.ds(a, m, stride=RADIX), :].astype(BF16) for s in range(n_slabs)]
            ub_scr[a] = jnp.concatenate(rows, axis=1)

    t = f_ref.shape[0] // 2
    f = f_ref[...]
    z = []
    for a in range(RADIX):
        za = jnp.dot(f, ub_scr[a], preferred_element_type=F32)
        zr, zi = za[:t], za[t:]
        cc, ss = twc_ref[a], tws_ref[a]
        z.append((zr * cc + zi * ss, zi * cc - zr * ss))
    x = _dft8(z)
    for k1 in range(RADIX):
        xr_ref[k1] = x[k1][0].astype(xr_ref.dtype)
        xi_ref[k1] = x[k1][1].astype(xi_ref.dtype)


def _fft(p3, col0, fmat, twc, tws, riders=()):
    b, seq, _ = p3.shape
    w = FWIDTH
    m = seq // RADIX
    nt, t2, _ = fmat.shape
    t = t2 // 2
    chb = 256
    n_slabs = chb // LANES
    slab0 = col0 // LANES
    out = jax.ShapeDtypeStruct((b, RADIX, m, w), BF16)
    o_spec = pl.BlockSpec((None, RADIX, t, chb), lambda bi, cj, kt: (bi, 0, kt, cj))
    slab = lambda s: pl.BlockSpec((None, seq, LANES), lambda bi, cj, kt: (bi, 0, slab0 + cj * n_slabs + s))
    ncj = w // chb
    r_in, r_out, r_shapes = _rider_specs(riders, b * ncj * nt, lambda bi, cj, kt: (bi * ncj + cj) * nt + kt)
    return pl.pallas_call(
        functools.partial(_fft_kernel, n_slabs=n_slabs, n_riders=len(riders)),
        out_shape=tuple([out, out] + r_shapes),
        grid=(b, ncj, nt),
        in_specs=[slab(s) for s in range(n_slabs)] + [
                  pl.BlockSpec((None, t2, m), lambda bi, cj, kt: (kt, 0, 0)),
                  pl.BlockSpec((RADIX, t, 1), lambda bi, cj, kt: (0, kt, 0)),
                  pl.BlockSpec((RADIX, t, 1), lambda bi, cj, kt: (0, kt, 0))] + r_in,
        out_specs=tuple([o_spec, o_spec] + r_out),
        scratch_shapes=[pltpu.VMEM((n_slabs, seq, LANES), F32), pltpu.VMEM((RADIX, m, chb), BF16)],
        compiler_params=_cparams(("arbitrary", "arbitrary", "arbitrary")),
        name="fft",
    )(*([p3] * n_slabs), fmat, twc, tws, *riders)


def _merge_kernel(of_ref, ob_ref, g_ref, gates_ref, xr_ref, xi_ref, x_ref, rt_ref, ct_ref, gt_ref,
                  gng_ref, wgo_ref, wfo_ref, wo_ref, csg_ref, o_ref):
    o = of_ref[...].astype(F32) + ob_ref[...].astype(F32)
    g = g_ref[...].astype(F32)
    gng = gng_ref[...]
    heads = []
    for h in range(HEADS):
        sl = slice(h * DV, (h + 1) * DV)
        heads.append((_rms(o[:, sl], gng) * _silu(g[:, sl])).astype(BF16))
    y_gla = jnp.dot(jnp.concatenate(heads, axis=1), wgo_ref[...], preferred_element_type=F32)

    csg = csg_ref[...]
    groups = []
    for gi in range(FGROUPS):
        sl = slice(gi * FGDIM, (gi + 1) * FGDIM)
        xg = jnp.concatenate([xr_ref[:, sl], xi_ref[:, sl]], axis=1)
        groups.append(jnp.dot(xg, csg, preferred_element_type=F32).astype(BF16))
    y_fft = jnp.dot(jnp.concatenate(groups, axis=1), wfo_ref[...], preferred_element_type=F32)

    gates = jax.nn.sigmoid(gates_ref[...].astype(F32))
    d = y_fft.shape[1]
    zmix = (gates[:, :d] * y_fft + gates[:, d:] * y_gla).astype(BF16)
    y = jnp.dot(zmix, wo_ref[...], preferred_element_type=F32)
    o_ref[...] = _add_pos(x_ref[...], rt_ref, ct_ref) + gt_ref[...] * y


def _const_spec(shape):
    zeros = (0,) * len(shape)
    return pl.BlockSpec(shape, lambda *idx: zeros, pipeline_mode=pl.Buffered(1))


def _merge(o_f, o_b, p2, xr, xi, x2, rt3, ct, mod3, gng, wgo, wfo, wo, csg, t):
    ntok, d = x2.shape
    tm = 256
    tiles_per_seq = t // tm
    rpt = tm // GRID_W
    row = lambda blk: (lambda i: (i, blk))
    in_specs = [pl.BlockSpec((tm, VAL_W), row(0)),
                pl.BlockSpec((tm, VAL_W), row(0)),
                pl.BlockSpec((tm, VAL_W), row(2)),
                pl.BlockSpec((tm, 2 * d), row(1)),
                pl.BlockSpec((tm, FWIDTH), row(0)),
                pl.BlockSpec((tm, FWIDTH), row(0)),
                pl.BlockSpec((tm, d), row(0)),
                pl.BlockSpec((rpt, 1, d // 2), lambda i: (i % tiles_per_seq, 0, 0)),
                _const_spec((GRID_W, d // 2)),
                pl.BlockSpec((None, 1, d), lambda i: (i // tiles_per_seq, 0, 2)),
                _const_spec(gng.shape), _const_spec(wgo.shape), _const_spec(wfo.shape),
                _const_spec(wo.shape), _const_spec(csg.shape)]
    return pl.pallas_call(
        _merge_kernel,
        out_shape=jax.ShapeDtypeStruct((ntok, d), F32),
        grid=(ntok // tm,),
        in_specs=in_specs,
        out_specs=pl.BlockSpec((tm, d), row(0)),
        compiler_params=_cparams(("arbitrary",)),
        name="merge",
    )(o_f, o_b, p2, p2, xr, xi, x2, rt3, ct, mod3, gng, wgo, wfo, wo, csg)


def _mlp_kernel(x_ref, xn_ref, sh_ref, sc_ref, shn_ref, scn_ref, gt_ref, g2_ref, w1_ref, w2_ref, fg_ref,
                o_ref, h_even, h_odd):
    i, j = pl.program_id(0), pl.program_id(1)
    nf = pl.num_programs(1)
    tm = x_ref.shape[0]
    slab = tm // nf
    slot = i % 2

    def normed(x, sh, sc):
        return (_rms(x, g2_ref[...]) * (1.0 + sc) + sh).astype(BF16)

    @pl.when((i == 0) & (j == 0))
    def _():
        h_even[...] = normed(x_ref[...], sh_ref[...], sc_ref[...])

    @pl.when(j == 0)
    def _():
        o_ref[...] = jnp.zeros_like(o_ref)

    def step(h_cur, h_next):
        r0 = pl.multiple_of(j * slab, slab)
        h_next[pl.ds(r0, slab), :] = normed(xn_ref[...], shn_ref[...], scn_ref[...])
        tf = w1_ref.shape[1]
        for c0 in range(0, tf, MLP_CHUNK):
            hid = jnp.dot(h_cur[...], w1_ref[:, c0:c0 + MLP_CHUNK], preferred_element_type=F32)
            hid = jnp.square(jnp.maximum(hid, 0.0)).astype(BF16)
            o_ref[...] += jnp.dot(hid, w2_ref[c0:c0 + MLP_CHUNK, :], preferred_element_type=F32)

    @pl.when(slot == 0)
    def _():
        step(h_even, h_odd)

    @pl.when(slot == 1)
    def _():
        step(h_odd, h_even)

    @pl.when(j == nf - 1)
    def _():
        xo = x_ref[...] + gt_ref[...] * o_ref[...]
        o_ref[...] = _rms(xo, fg_ref[...])


def _mlp(x2, mod3, g2, w1, w2, fg, t):
    ntok, d = x2.shape
    dff = w1.shape[1]
    tm, tf = 512, 2048
    nf = dff // tf
    slab = tm // nf
    assert slab % 8 == 0 and tf % MLP_CHUNK == 0
    nt = ntok // tm
    tiles_per_seq = t // tm
    nxt = lambda i: jnp.minimum(i + 1, nt - 1)
    modspec = lambda blk: pl.BlockSpec((None, 1, d), lambda i, j: (i // tiles_per_seq, 0, blk))
    modspec_next = lambda blk: pl.BlockSpec((None, 1, d), lambda i, j: (nxt(i) // tiles_per_seq, 0, blk))
    return pl.pallas_call(
        _mlp_kernel,
        out_shape=jax.ShapeDtypeStruct((ntok, d), F32),
        grid=(nt, dff // tf),
        in_specs=[pl.BlockSpec((tm, d), lambda i, j: (jnp.where(j >= 2, i, jnp.maximum(i - 1, 0)), 0)),
                  pl.BlockSpec((slab, d), lambda i, j: (nxt(i) * nf + j, 0)),
                  modspec(3), modspec(4), modspec_next(3), modspec_next(4), modspec(5),
                  pl.BlockSpec((1, d), lambda i, j: (0, 0)),
                  pl.BlockSpec((d, tf), lambda i, j: (0, j)),
                  pl.BlockSpec((tf, d), lambda i, j: (j, 0)),
                  pl.BlockSpec((1, d), lambda i, j: (0, 0))],
        out_specs=pl.BlockSpec((tm, d), lambda i, j: (i, 0)),
        scratch_shapes=[pltpu.VMEM((tm, d), BF16), pltpu.VMEM((tm, d), BF16)],
        compiler_params=_cparams(("arbitrary", "arbitrary"), vmem=BIG_TILE_VMEM_LIMIT),
        name="mlp",
    )(x2, x2, mod3, mod3, mod3, mod3, mod3, g2, w1, w2, fg)


def _pos_tables(t, d):
    quarter = d // 4
    omega = 1.0 / (POS_TEMP ** (jnp.arange(quarter, dtype=F32) / quarter))
    er = jnp.arange(t // GRID_W, dtype=F32)[:, None] * omega[None, :]
    ec = jnp.arange(GRID_W, dtype=F32)[:, None] * omega[None, :]
    rt = jnp.concatenate([jnp.sin(er), jnp.cos(er)], axis=-1)
    ct = jnp.concatenate([jnp.sin(ec), jnp.cos(ec)], axis=-1)
    return rt[:, None, :], ct


def _dft_tables(t):
    m = t // RADIX
    tile = min(256, m)
    k = np.arange(m)
    ang = 2.0 * np.pi * ((k[:, None] * k[None, :]) % m) / m
    cos_t = np.cos(ang).reshape(m // tile, tile, m)
    sin_t = np.sin(ang).reshape(m // tile, tile, m)
    fmat = np.concatenate([cos_t, -sin_t], axis=1)
    a = np.arange(RADIX)
    tw = 2.0 * np.pi * (a[:, None] * k[None, :]) / t
    scale = 1.0 / math.sqrt(t * FGDIM)
    twc = (np.cos(tw) * scale)[:, :, None]
    tws = (np.sin(tw) * scale)[:, :, None]
    c = np.arange(FGDIM)
    cang = 2.0 * np.pi * ((c[:, None] * c[None, :]) % FGDIM) / FGDIM
    csg = np.concatenate([np.cos(cang), np.sin(cang)], axis=0)
    as_f32 = lambda a: jnp.asarray(a.astype(np.float32))
    return as_f32(fmat).astype(BF16), as_f32(twc), as_f32(tws), as_f32(csg).astype(BF16)


def _pad_lr_weight(w_lr, row0):
    out = jnp.zeros((LR_PAD, KEY_W), F32)
    return out.at[row0:row0 + RANK].set(w_lr).astype(BF16)


def kernel(x, c, ctx, c_ctx, w_mod, b_mod, norm1_g, norm2_g, w_in, w_lr_f, b_lr_f, w_lr_b, b_lr_b,
           gla_norm_g, w_fourier_out, w_gla_out, w_out, w_mlp_in, w_mlp_out, final_norm_g):
    b, t, d = x.shape
    tc = ctx.shape[1]
    depth = w_mod.shape[0]
    assert depth == 1 and d == D_MODEL and t % (RADIX * GRID_W) == 0 and tc % GLA_CHUNK == 0
    li = 0

    w_main, w_lr = _wprep(w_in[li].T)
    wlr_f, wlr_b = _pad_lr_weight(w_lr_f[li], 0), _pad_lr_weight(w_lr_b[li], RANK)
    blr_f, blr_b = b_lr_f[li][None, :], b_lr_b[li][None, :]

    rt3, ct = _pos_tables(t, d)
    fmat, twc, tws, csg = _dft_tables(t)

    rows = 8
    cpad = jnp.concatenate([c, c_ctx[None, :], jnp.zeros((rows - b - 1, d), F32)], axis=0)
    mod3 = _mod(cpad.T, b + 1, w_mod[li], b_mod[li][None, :]).reshape(rows, 1, N_MOD * d)

    tm_ctx = min(512, b * tc)
    qkv_w = 2 * KEY_W + VAL_W
    p_ctx, lr_ctx = _inproj(ctx.reshape(b * tc, d), mod3, lambda i: b, norm1_g[li][None, :],
                            w_main, w_lr, None, tm_ctx, qkv_w)
    s_zero = jnp.zeros((b, HEADS, DK, DV), F32)
    s_f, s_b = _gla(p_ctx.reshape(b, tc, qkv_w), lr_ctx.reshape(b, tc, LR_PAD),
                    wlr_f, blr_f, wlr_b, blr_b, s_zero, s_zero, emit_o=False)

    tm = min(1024, t)
    tiles = t // tm
    x2 = x.reshape(b * t, d)
    p, lr = _inproj(x2, mod3, lambda i: i // tiles, norm1_g[li][None, :], w_main, w_lr,
                    (rt3, ct, tiles), tm, P_WIDTH)
    p3 = p.reshape(b, t, P_WIDTH)
    o_f, o_b, _, _, w1, wgo, wfo, wo = _gla(
        p3, lr.reshape(b, t, LR_PAD), wlr_f, blr_f, wlr_b, blr_b, s_f, s_b, emit_o=True,
        riders=(w_mlp_in[li], w_gla_out[li], w_fourier_out[li], w_out[li]))

    xr, xi, w2 = _fft(p3, QKVG_W, fmat, twc, tws, riders=(w_mlp_out[li],))

    x1 = _merge(o_f.reshape(b * t, VAL_W), o_b.reshape(b * t, VAL_W), p,
                xr.reshape(b * t, FWIDTH), xi.reshape(b * t, FWIDTH), x2, rt3, ct, mod3,
                gla_norm_g[li][None, :], wgo, wfo, wo, csg, t)

    out = _mlp(x1, mod3, norm2_g[li][None, :], w1, w2, final_norm_g[None, :], t)
    return out.reshape(b, t, d)
```

```python
import functools
import math

import jax
import jax.numpy as jnp
import numpy as np
from jax import lax
from jax.experimental import pallas as pl
from jax.experimental.pallas import tpu as pltpu

F32 = jnp.float32
BF16 = jnp.bfloat16

D_MODEL = 2048
GRID_W = 64
HEADS = 4
DK = 128
DV = 256
KEY_W = HEADS * DK
VAL_W = HEADS * DV
RANK = 16
TAU = 16.0
FGROUPS = 4
FGDIM = 256
FWIDTH = FGROUPS * FGDIM
D_FF = 4 * D_MODEL
N_MOD = 6
EPS = 1e-6
POS_TEMP = 10000.0

QKVG_W = 2 * KEY_W + 2 * VAL_W
P_WIDTH = QKVG_W + FWIDTH + 2 * D_MODEL
LR_PAD = 128
GLA_CHUNK = 128
GLA_CHUNKS_PER_STEP = 8
RADIX = 8
LANES = 128

V7X_VMEM_LIMIT = 56 * 1024 * 1024
BIG_TILE_VMEM_LIMIT = 63 * 1024 * 1024
MLP_CHUNK = 1024


def _cparams(sem, vmem=V7X_VMEM_LIMIT):
    return pltpu.CompilerParams(dimension_semantics=sem, vmem_limit_bytes=vmem)


def _silu(x):
    return x * jax.nn.sigmoid(x)


def _mod_kernel(ct_ref, w_ref, b_ref, o_ref, *, n_used):
    st = _silu(ct_ref[...])
    w = w_ref[...]
    bias = b_ref[...]
    rows = [jnp.sum(w * st[:, m:m + 1], axis=0, keepdims=True) + bias for m in range(n_used)]
    rows += [bias] * (o_ref.shape[0] - n_used)
    o_ref[...] = jnp.concatenate(rows, axis=0)


def _mod(cpad_t, n_used, w_mod, b_mod):
    d, rows = cpad_t.shape
    n = w_mod.shape[1]
    tn = 1024
    return pl.pallas_call(
        functools.partial(_mod_kernel, n_used=n_used),
        out_shape=jax.ShapeDtypeStruct((rows, n), F32),
        grid=(n // tn,),
        in_specs=[pl.BlockSpec((d, rows), lambda j: (0, 0)),
                  pl.BlockSpec((d, tn), lambda j: (0, j)),
                  pl.BlockSpec((1, tn), lambda j: (0, j))],
        out_specs=pl.BlockSpec((rows, tn), lambda j: (0, j)),
        compiler_params=_cparams(("arbitrary",)),
        name="mod",
    )(cpad_t, w_mod, b_mod)


def _wprep_kernel(wt_ref, lrt_ref, wm_ref, wlr_ref):
    wm_ref[...] = wt_ref[...].T.astype(BF16)

    @pl.when(pl.program_id(0) == 0)
    def _():
        lrt = lrt_ref[...]
        r = lax.broadcasted_iota(jnp.int32, lrt.shape, 0)
        wlr_ref[...] = jnp.where(r < 2 * RANK, lrt, 0.0).T.astype(BF16)


def _wprep(w_in_t):
    n, d = w_in_t.shape
    tc = 512
    n_before = QKVG_W // tc

    def src_row(j):
        return pl.multiple_of(jnp.where(j >= n_before, j * tc + 2 * RANK, j * tc), 2 * RANK)

    return pl.pallas_call(
        _wprep_kernel,
        out_shape=(jax.ShapeDtypeStruct((d, P_WIDTH), BF16), jax.ShapeDtypeStruct((d, LR_PAD), BF16)),
        grid=(P_WIDTH // tc,),
        in_specs=[pl.BlockSpec((pl.Element(tc), pl.Element(d)), lambda j: (src_row(j), 0)),
                  pl.BlockSpec((LR_PAD, d), lambda j: (QKVG_W // LR_PAD, 0))],
        out_specs=(pl.BlockSpec((d, tc), lambda j: (0, j)), pl.BlockSpec((d, LR_PAD), lambda j: (0, 0))),
        compiler_params=_cparams(("arbitrary",)),
        name="wprep",
    )(w_in_t, w_in_t)


def _add_pos(x, rt_ref, ct_ref):
    tm, d = x.shape
    x3 = x.reshape(tm // GRID_W, GRID_W, d)
    half = d // 2
    lo = x3[:, :, :half] + rt_ref[...]
    hi = x3[:, :, half:] + ct_ref[...][None]
    return jnp.concatenate([lo, hi], axis=-1).reshape(tm, d)


def _rms(x, g):
    return x * lax.rsqrt(jnp.mean(x * x, axis=-1, keepdims=True) + EPS) * g


def _inproj_kernel(*refs, add_pos):
    if add_pos:
        x_ref, rt_ref, ct_ref, sh_ref, sc_ref, g_ref, w_ref, wlr_ref, p_ref, lr_ref, h_scr = refs
    else:
        x_ref, sh_ref, sc_ref, g_ref, w_ref, wlr_ref, p_ref, lr_ref, h_scr = refs

    @pl.when(pl.program_id(1) == 0)
    def _():
        x = x_ref[...]
        if add_pos:
            x = _add_pos(x, rt_ref, ct_ref)
        h = _rms(x, g_ref[...]) * (1.0 + sc_ref[...]) + sh_ref[...]
        hb = h.astype(BF16)
        h_scr[...] = hb
        lr_ref[...] = jnp.dot(hb, wlr_ref[...], preferred_element_type=F32)

    p_ref[...] = jnp.dot(h_scr[...], w_ref[...], preferred_element_type=F32).astype(BF16)


def _inproj(x2, mod3, mod_row_of_tile, norm_g, w_main, w_lr, pos_tabs, tm, width):
    ntok, d = x2.shape
    tn = 2048
    add_pos = pos_tabs is not None
    in_specs = [pl.BlockSpec((tm, d), lambda i, j: (i, 0))]
    args = [x2]
    if add_pos:
        rt3, ct, tiles_per_seq = pos_tabs
        rpt = tm // GRID_W
        in_specs += [pl.BlockSpec((rpt, 1, d // 2), lambda i, j: (i % tiles_per_seq, 0, 0)),
                     pl.BlockSpec((GRID_W, d // 2), lambda i, j: (0, 0))]
        args += [rt3, ct]
    in_specs += [pl.BlockSpec((None, 1, d), lambda i, j: (mod_row_of_tile(i), 0, 0)),
                 pl.BlockSpec((None, 1, d), lambda i, j: (mod_row_of_tile(i), 0, 1)),
                 pl.BlockSpec((1, d), lambda i, j: (0, 0)),
                 pl.BlockSpec((d, tn), lambda i, j: (0, j)),
                 pl.BlockSpec((d, LR_PAD), lambda i, j: (0, 0))]
    args += [mod3, mod3, norm_g, w_main, w_lr]
    return pl.pallas_call(
        functools.partial(_inproj_kernel, add_pos=add_pos),
        out_shape=(jax.ShapeDtypeStruct((ntok, width), BF16),
                   jax.ShapeDtypeStruct((ntok, LR_PAD), F32)),
        grid=(ntok // tm, width // tn),
        in_specs=in_specs,
        out_specs=(pl.BlockSpec((tm, tn), lambda i, j: (i, j)),
                   pl.BlockSpec((tm, LR_PAD), lambda i, j: (i, 0))),
        scratch_shapes=[pltpu.VMEM((tm, d), BF16)],
        compiler_params=_cparams(("arbitrary", "arbitrary"), vmem=BIG_TILE_VMEM_LIMIT),
        name="inproj_pos" if add_pos else "inproj_ctx",
    )(*args)


def _log_sigmoid(z):
    return jnp.minimum(z, 0.0) - jnp.log1p(jnp.exp(-jnp.abs(z)))


def _gla_chunks(dirs, c):
    n_sub = dirs[0][0].shape[0] // c
    row = lax.broadcasted_iota(jnp.int32, (c, c), 0)
    col = lax.broadcasted_iota(jnp.int32, (c, c), 1)

    chains = {}
    log2_qscale = math.log2(DK ** -0.5)

    def front(step):
        for di, (qk_ref, v_ref, lr_ref, wlr_ref, blr_ref, s_scr, o_ref, backward) in enumerate(dirs):
            keep = (row <= col) if backward else (row >= col)
            tri = jnp.where(keep, 1.0, 0.0).astype(BF16)
            last = 0 if backward else c - 1
            mid = c // 2 if backward else c // 2 - 1
            sub = n_sub - 1 - step if backward else step
            rows = slice(sub * c, (sub + 1) * c)
            z = jnp.dot(lr_ref[rows, :].astype(BF16), wlr_ref[...], preferred_element_type=F32) + blr_ref[...]
            la = _log_sigmoid(z) * (math.log2(math.e) / TAU)
            la_hi = la.astype(BF16)
            la_lo = (la - la_hi.astype(F32)).astype(BF16)
            cum2 = jnp.dot(tri, jnp.concatenate([la_hi, la_lo], axis=1), preferred_element_type=F32)
            cum_all = cum2[:, :KEY_W] + cum2[:, KEY_W:]
            for h in range(HEADS):
                cum = cum_all[:, h * DK:(h + 1) * DK]
                tot = cum[last:last + 1, :]
                ref_pt = cum[mid:mid + 1, :]
                q = qk_ref[rows, h * DK:(h + 1) * DK].astype(F32)
                k = qk_ref[rows, KEY_W + h * DK:KEY_W + (h + 1) * DK].astype(F32)
                chains[(step, di, h)] = dict(
                    keep=keep, rows=rows,
                    v=v_ref[rows, h * DV:(h + 1) * DV],
                    q_mid=(q * jnp.exp2(cum - (ref_pt - log2_qscale))).astype(BF16),
                    k_mid=(k * jnp.exp2(ref_pt - cum)).astype(BF16),
                    q_dec=(q * jnp.exp2(cum + log2_qscale)).astype(BF16),
                    k_end_t=(k * jnp.exp2(tot - cum)).T.astype(BF16),
                    dec_col=jnp.broadcast_to(jnp.exp2(tot), (DK, DK)).T)

    def middle(step):
        for di in range(len(dirs)):
            for h in range(HEADS):
                ch = chains[(step, di, h)]
                s = lax.dot_general(ch["q_mid"], ch["k_mid"], (((1,), (1,)), ((), ())),
                                    preferred_element_type=F32)
                ch["scores"] = jnp.where(ch["keep"], s, 0.0).astype(BF16)
                ch["kv"] = jnp.dot(ch["k_end_t"], ch["v"], preferred_element_type=F32)

    state = {(di, h): d[5][h] for di, d in enumerate(dirs) for h in range(HEADS)}

    def tail(step):
        for di, d in enumerate(dirs):
            o_ref = d[6]
            for h in range(HEADS):
                ch = chains.pop((step, di, h))
                s_prev = state[(di, h)]
                lhs = jnp.concatenate([ch["scores"], ch["q_dec"]], axis=1)
                rhs = jnp.concatenate([ch["v"], s_prev.astype(BF16)], axis=0)
                o = jnp.dot(lhs, rhs, preferred_element_type=F32)
                if o_ref is not None:
                    o_ref[ch["rows"], h * DV:(h + 1) * DV] = o.astype(o_ref.dtype)
                dec = jnp.concatenate([ch["dec_col"]] * (DV // DK), axis=1)
                state[(di, h)] = s_prev * dec + ch["kv"]

    front(0)
    for step in range(n_sub):
        if step + 1 < n_sub:
            front(step + 1)
        middle(step)
        tail(step)
    for (di, h), s in state.items():
        dirs[di][5][h] = s


def _rider_specs(weights, n_steps, flat_step):
    specs, shapes = [], []
    for w in weights:
        rows, cols = w.shape
        assert rows % (16 * n_steps) == 0
        specs.append(pl.BlockSpec((rows // n_steps, cols), lambda *idx: (flat_step(*idx), 0)))
        shapes.append(jax.ShapeDtypeStruct(w.shape, BF16))
    return specs, list(specs), shapes


def _cast_riders(in_refs, out_refs):
    for src, dst in zip(in_refs, out_refs):
        dst[...] = src[...].astype(dst.dtype)


def _gla_kernel(*refs, emit_o, n_riders):
    (qkf, vf, lrf, qkb, vb, lrb, wf, bf, wb, bb, s0f, s0b) = refs[:12]
    rider_in, refs = refs[12:12 + n_riders], refs[12 + n_riders:]
    if emit_o:
        of, ob, sf_out, sb_out = refs[:4]
        rider_out = refs[4:4 + n_riders]
    else:
        sf_out, sb_out = refs[:2]
        rider_out = refs[2:2 + n_riders]
        of = ob = None
    s_scr = refs[-1]
    _cast_riders(rider_in, rider_out)
    i = pl.program_id(1)

    @pl.when(i == 0)
    def _():
        s_scr[0] = s0f[...]
        s_scr[1] = s0b[...]

    _gla_chunks([(qkf, vf, lrf, wf, bf, s_scr.at[0], of, False),
                 (qkb, vb, lrb, wb, bb, s_scr.at[1], ob, True)], GLA_CHUNK)

    @pl.when(i == pl.num_programs(1) - 1)
    def _():
        sf_out[...] = s_scr[0]
        sb_out[...] = s_scr[1]


def _gla(p3, lr3, wlr_f, blr_f, wlr_b, blr_b, s0f, s0b, emit_o, riders=()):
    b, t, _ = p3.shape
    per_step = max(s for s in range(1, GLA_CHUNKS_PER_STEP + 1) if t % (GLA_CHUNK * s) == 0)
    c = GLA_CHUNK * per_step
    n = t // c
    fwd = lambda blk: (lambda bi, i: (bi, i, blk))
    bwd = lambda blk: (lambda bi, i: (bi, n - 1 - i, blk))

    def seq_specs(mk):
        return [pl.BlockSpec((None, c, 2 * KEY_W), mk(0)),
                pl.BlockSpec((None, c, VAL_W), mk(1)),
                pl.BlockSpec((None, c, LR_PAD), mk(0))]
    full2 = lambda shape: pl.BlockSpec(shape, lambda bi, i: (0, 0))
    st_spec = pl.BlockSpec((None, HEADS, DK, DV), lambda bi, i: (bi, 0, 0, 0))
    in_specs = (seq_specs(fwd) + seq_specs(bwd)
                + [full2(wlr_f.shape), full2(blr_f.shape), full2(wlr_b.shape), full2(blr_b.shape),
                   st_spec, st_spec])
    st_shape = jax.ShapeDtypeStruct((b, HEADS, DK, DV), F32)
    out_shape = [st_shape, st_shape]
    out_specs = [st_spec, st_spec]
    if emit_o:
        o_shape = jax.ShapeDtypeStruct((b, t, VAL_W), BF16)
        out_shape = [o_shape, o_shape] + out_shape
        out_specs = [pl.BlockSpec((None, c, VAL_W), fwd(0)), pl.BlockSpec((None, c, VAL_W), bwd(0))] + out_specs
    r_in, r_out, r_shapes = _rider_specs(riders, b * n, lambda bi, i: bi * n + i)
    return pl.pallas_call(
        functools.partial(_gla_kernel, emit_o=emit_o, n_riders=len(riders)),
        out_shape=tuple(out_shape + r_shapes),
        grid=(b, n),
        in_specs=in_specs + r_in,
        out_specs=tuple(out_specs + r_out),
        scratch_shapes=[pltpu.VMEM((2, HEADS, DK, DV), F32)],
        compiler_params=_cparams(("arbitrary", "arbitrary")),
        name="gla_seq" if emit_o else "gla_ctx",
    )(p3, p3, lr3, p3, p3, lr3, wlr_f, blr_f, wlr_b, blr_b, s0f, s0b, *riders)


def _cadd(a, b):
    return a[0] + b[0], a[1] + b[1]


def _csub(a, b):
    return a[0] - b[0], a[1] - b[1]


def _cmul_neg_i(a):
    return a[1], -a[0]


def _dft4(y):
    t0, t1 = _cadd(y[0], y[2]), _csub(y[0], y[2])
    t2, t3 = _cadd(y[1], y[3]), _cmul_neg_i(_csub(y[1], y[3]))
    return [_cadd(t0, t2), _cadd(t1, t3), _csub(t0, t2), _csub(t1, t3)]


def _dft8(z):
    r = math.sqrt(0.5)
    s = [_cadd(z[a], z[a + 4]) for a in range(4)]
    d = [_csub(z[a], z[a + 4]) for a in range(4)]
    d1 = ((d[1][0] + d[1][1]) * r, (d[1][1] - d[1][0]) * r)
    d2 = _cmul_neg_i(d[2])
    d3 = ((d[3][1] - d[3][0]) * r, (-d[3][1] - d[3][0]) * r)
    ev = _dft4(s)
    od = _dft4([d[0], d1, d2, d3])
    out = [None] * 8
    for j in range(4):
        out[2 * j] = ev[j]
        out[2 * j + 1] = od[j]
    return out


def _fft_kernel(*refs, n_slabs, n_riders):
    u_refs = refs[:n_slabs]
    f_ref, twc_ref, tws_ref = refs[n_slabs:n_slabs + 3]
    rider_in = refs[n_slabs + 3:n_slabs + 3 + n_riders]
    xr_ref, xi_ref = refs[n_slabs + 3 + n_riders:n_slabs + 5 + n_riders]
    rider_out = refs[n_slabs + 5 + n_riders:n_slabs + 5 + 2 * n_riders]
    wide_scr, ub_scr = refs[-2:]
    _cast_riders(rider_in, rider_out)
    m = ub_scr.shape[1]

    @pl.when(pl.program_id(2) == 0)
    def _():
        for s in range(n_slabs):
            wide_scr[s] = u_refs[s][...].astype(F32)
        for a in range(RADIX):
            rows = [wide_scr[s, pl.ds(a, m, stride=RADIX), :].astype(BF16) for s in range(n_slabs)]
            ub_scr[a] = jnp.concatenate(rows, axis=1)

    t = f_ref.shape[0] // 2
    f = f_ref[...]
    z = []
    for a in range(RADIX):
        za = jnp.dot(f, ub_scr[a], preferred_element_type=F32)
        zr, zi = za[:t], za[t:]
        cc, ss = twc_ref[a], tws_ref[a]
        z.append((zr * cc + zi * ss, zi * cc - zr * ss))
    x = _dft8(z)
    for k1 in range(RADIX):
        xr_ref[k1] = x[k1][0].astype(xr_ref.dtype)
        xi_ref[k1] = x[k1][1].astype(xi_ref.dtype)


def _fft(p3, col0, fmat, twc, tws, riders=()):
    b, seq, _ = p3.shape
    w = FWIDTH
    m = seq // RADIX
    nt, t2, _ = fmat.shape
    t = t2 // 2
    chb = 256
    n_slabs = chb // LANES
    slab0 = col0 // LANES
    out = jax.ShapeDtypeStruct((b, RADIX, m, w), BF16)
    o_spec = pl.BlockSpec((None, RADIX, t, chb), lambda bi, cj, kt: (bi, 0, kt, cj))
    slab = lambda s: pl.BlockSpec((None, seq, LANES), lambda bi, cj, kt: (bi, 0, slab0 + cj * n_slabs + s))
    ncj = w // chb
    r_in, r_out, r_shapes = _rider_specs(riders, b * ncj * nt, lambda bi, cj, kt: (bi * ncj + cj) * nt + kt)
    return pl.pallas_call(
        functools.partial(_fft_kernel, n_slabs=n_slabs, n_riders=len(riders)),
        out_shape=tuple([out, out] + r_shapes),
        grid=(b, ncj, nt),
        in_specs=[slab(s) for s in range(n_slabs)] + [
                  pl.BlockSpec((None, t2, m), lambda bi, cj, kt: (kt, 0, 0)),
                  pl.BlockSpec((RADIX, t, 1), lambda bi, cj, kt: (0, kt, 0)),
                  pl.BlockSpec((RADIX, t, 1), lambda bi, cj, kt: (0, kt, 0))] + r_in,
        out_specs=tuple([o_spec, o_spec] + r_out),
        scratch_shapes=[pltpu.VMEM((n_slabs, seq, LANES), F32), pltpu.VMEM((RADIX, m, chb), BF16)],
        compiler_params=_cparams(("arbitrary", "arbitrary", "arbitrary")),
        name="fft",
    )(*([p3] * n_slabs), fmat, twc, tws, *riders)


def _merge_kernel(of_ref, ob_ref, g_ref, gates_ref, xr_ref, xi_ref, x_ref, rt_ref, ct_ref, gt_ref,
                  gng_ref, wgo_ref, wfo_ref, wo_ref, csg_ref, o_ref):
    o = of_ref[...].astype(F32) + ob_ref[...].astype(F32)
    g = g_ref[...].astype(F32)
    gng = gng_ref[...]
    heads = []
    for h in range(HEADS):
        sl = slice(h * DV, (h + 1) * DV)
        heads.append((_rms(o[:, sl], gng) * _silu(g[:, sl])).astype(BF16))
    y_gla = jnp.dot(jnp.concatenate(heads, axis=1), wgo_ref[...], preferred_element_type=F32)

    csg = csg_ref[...]
    groups = []
    for gi in range(FGROUPS):
        sl = slice(gi * FGDIM, (gi + 1) * FGDIM)
        xg = jnp.concatenate([xr_ref[:, sl], xi_ref[:, sl]], axis=1)
        groups.append(jnp.dot(xg, csg, preferred_element_type=F32).astype(BF16))
    y_fft = jnp.dot(jnp.concatenate(groups, axis=1), wfo_ref[...], preferred_element_type=F32)

    gates = jax.nn.sigmoid(gates_ref[...].astype(F32))
    d = y_fft.shape[1]
    zmix = (gates[:, :d] * y_fft + gates[:, d:] * y_gla).astype(BF16)
    y = jnp.dot(zmix, wo_ref[...], preferred_element_type=F32)
    o_ref[...] = _add_pos(x_ref[...], rt_ref, ct_ref) + gt_ref[...] * y


def _const_spec(shape):
    zeros = (0,) * len(shape)
    return pl.BlockSpec(shape, lambda *idx: zeros, pipeline_mode=pl.Buffered(1))


def _merge(o_f, o_b, p2, xr, xi, x2, rt3, ct, mod3, gng, wgo, wfo, wo, csg, t):
    ntok, d = x2.shape
    tm = 512
    tiles_per_seq = t // tm
    rpt = tm // GRID_W
    row = lambda blk: (lambda i: (i, blk))
    in_specs = [pl.BlockSpec((tm, VAL_W), row(0)),
                pl.BlockSpec((tm, VAL_W), row(0)),
                pl.BlockSpec((tm, VAL_W), row(2)),
                pl.BlockSpec((tm, 2 * d), row(1)),
                pl.BlockSpec((tm, FWIDTH), row(0)),
                pl.BlockSpec((tm, FWIDTH), row(0)),
                pl.BlockSpec((tm, d), row(0)),
                pl.BlockSpec((rpt, 1, d // 2), lambda i: (i % tiles_per_seq, 0, 0)),
                _const_spec((GRID_W, d // 2)),
                pl.BlockSpec((None, 1, d), lambda i: (i // tiles_per_seq, 0, 2)),
                _const_spec(gng.shape), _const_spec(wgo.shape), _const_spec(wfo.shape),
                _const_spec(wo.shape), _const_spec(csg.shape)]
    return pl.pallas_call(
        _merge_kernel,
        out_shape=jax.ShapeDtypeStruct((ntok, d), F32),
        grid=(ntok // tm,),
        in_specs=in_specs,
        out_specs=pl.BlockSpec((tm, d), row(0)),
        compiler_params=_cparams(("arbitrary",), vmem=BIG_TILE_VMEM_LIMIT),
        name="merge",
    )(o_f, o_b, p2, p2, xr, xi, x2, rt3, ct, mod3, gng, wgo, wfo, wo, csg)


def _mlp_kernel(x_ref, xn_ref, sh_ref, sc_ref, shn_ref, scn_ref, gt_ref, g2_ref, w1_ref, w2_ref, fg_ref,
                o_ref, h_even, h_odd):
    i, j = pl.program_id(0), pl.program_id(1)
    nf = pl.num_programs(1)
    tm = x_ref.shape[0]
    slab = tm // nf
    slot = i % 2

    def normed(x, sh, sc):
        return (_rms(x, g2_ref[...]) * (1.0 + sc) + sh).astype(BF16)

    @pl.when((i == 0) & (j == 0))
    def _():
        h_even[...] = normed(x_ref[...], sh_ref[...], sc_ref[...])

    @pl.when(j == 0)
    def _():
        o_ref[...] = jnp.zeros_like(o_ref)

    def step(h_cur, h_next):
        r0 = pl.multiple_of(j * slab, slab)
        h_next[pl.ds(r0, slab), :] = normed(xn_ref[...], shn_ref[...], scn_ref[...])
        tf = w1_ref.shape[1]
        for c0 in range(0, tf, MLP_CHUNK):
            hid = jnp.dot(h_cur[...], w1_ref[:, c0:c0 + MLP_CHUNK], preferred_element_type=F32)
            hid = jnp.square(jnp.maximum(hid, 0.0)).astype(BF16)
            o_ref[...] += jnp.dot(hid, w2_ref[c0:c0 + MLP_CHUNK, :], preferred_element_type=F32)

    @pl.when(slot == 0)
    def _():
        step(h_even, h_odd)

    @pl.when(slot == 1)
    def _():
        step(h_odd, h_even)

    @pl.when(j == nf - 1)
    def _():
        xo = x_ref[...] + gt_ref[...] * o_ref[...]
        o_ref[...] = _rms(xo, fg_ref[...])


def _mlp(x2, mod3, g2, w1, w2, fg, t):
    ntok, d = x2.shape
    dff = w1.shape[1]
    tm, tf = 512, 2048
    nf = dff // tf
    slab = tm // nf
    assert slab % 8 == 0 and tf % MLP_CHUNK == 0
    nt = ntok // tm
    tiles_per_seq = t // tm
    nxt = lambda i: jnp.minimum(i + 1, nt - 1)
    modspec = lambda blk: pl.BlockSpec((None, 1, d), lambda i, j: (i // tiles_per_seq, 0, blk))
    modspec_next = lambda blk: pl.BlockSpec((None, 1, d), lambda i, j: (nxt(i) // tiles_per_seq, 0, blk))
    return pl.pallas_call(
        _mlp_kernel,
        out_shape=jax.ShapeDtypeStruct((ntok, d), F32),
        grid=(nt, dff // tf),
        in_specs=[pl.BlockSpec((tm, d), lambda i, j: (jnp.where(j >= 2, i, jnp.maximum(i - 1, 0)), 0)),
                  pl.BlockSpec((slab, d), lambda i, j: (nxt(i) * nf + j, 0)),
                  modspec(3), modspec(4), modspec_next(3), modspec_next(4), modspec(5),
                  pl.BlockSpec((1, d), lambda i, j: (0, 0)),
                  pl.BlockSpec((d, tf), lambda i, j: (0, j)),
                  pl.BlockSpec((tf, d), lambda i, j: (j, 0)),
                  pl.BlockSpec((1, d), lambda i, j: (0, 0))],
        out_specs=pl.BlockSpec((tm, d), lambda i, j: (i, 0)),
        scratch_shapes=[pltpu.VMEM((tm, d), BF16), pltpu.VMEM((tm, d), BF16)],
        compiler_params=_cparams(("arbitrary", "arbitrary"), vmem=BIG_TILE_VMEM_LIMIT),
        name="mlp",
    )(x2, x2, mod3, mod3, mod3, mod3, mod3, g2, w1, w2, fg)


def _pos_tables(t, d):
    quarter = d // 4
    omega = 1.0 / (POS_TEMP ** (jnp.arange(quarter, dtype=F32) / quarter))
    er = jnp.arange(t // GRID_W, dtype=F32)[:, None] * omega[None, :]
    ec = jnp.arange(GRID_W, dtype=F32)[:, None] * omega[None, :]
    rt = jnp.concatenate([jnp.sin(er), jnp.cos(er)], axis=-1)
    ct = jnp.concatenate([jnp.sin(ec), jnp.cos(ec)], axis=-1)
    return rt[:, None, :], ct


def _dft_tables(t):
    m = t // RADIX
    tile = min(256, m)
    k = np.arange(m)
    ang = 2.0 * np.pi * ((k[:, None] * k[None, :]) % m) / m
    cos_t = np.cos(ang).reshape(m // tile, tile, m)
    sin_t = np.sin(ang).reshape(m // tile, tile, m)
    fmat = np.concatenate([cos_t, -sin_t], axis=1)
    a = np.arange(RADIX)
    tw = 2.0 * np.pi * (a[:, None] * k[None, :]) / t
    scale = 1.0 / math.sqrt(t * FGDIM)
    twc = (np.cos(tw) * scale)[:, :, None]
    tws = (np.sin(tw) * scale)[:, :, None]
    c = np.arange(FGDIM)
    cang = 2.0 * np.pi * ((c[:, None] * c[None, :]) % FGDIM) / FGDIM
    csg = np.concatenate([np.cos(cang), np.sin(cang)], axis=0)
    as_f32 = lambda a: jnp.asarray(a.astype(np.float32))
    return as_f32(fmat).astype(BF16), as_f32(twc), as_f32(tws), as_f32(csg).astype(BF16)


def _pad_lr_weight(w_lr, row0):
    out = jnp.zeros((LR_PAD, KEY_W), F32)
    return out.at[row0:row0 + RANK].set(w_lr).astype(BF16)


def kernel(x, c, ctx, c_ctx, w_mod, b_mod, norm1_g, norm2_g, w_in, w_lr_f, b_lr_f, w_lr_b, b_lr_b,
           gla_norm_g, w_fourier_out, w_gla_out, w_out, w_mlp_in, w_mlp_out, final_norm_g):
    b, t, d = x.shape
    tc = ctx.shape[1]
    depth = w_mod.shape[0]
    assert depth == 1 and d == D_MODEL and t % (RADIX * GRID_W) == 0 and tc % GLA_CHUNK == 0
    li = 0

    w_main, w_lr = _wprep(w_in[li].T)
    wlr_f, wlr_b = _pad_lr_weight(w_lr_f[li], 0), _pad_lr_weight(w_lr_b[li], RANK)
    blr_f, blr_b = b_lr_f[li][None, :], b_lr_b[li][None, :]

    rt3, ct = _pos_tables(t, d)
    fmat, twc, tws, csg = _dft_tables(t)

    rows = 8
    cpad = jnp.concatenate([c, c_ctx[None, :], jnp.zeros((rows - b - 1, d), F32)], axis=0)
    mod3 = _mod(cpad.T, b + 1, w_mod[li], b_mod[li][None, :]).reshape(rows, 1, N_MOD * d)

    tm_ctx = min(512, b * tc)
    qkv_w = 2 * KEY_W + VAL_W
    p_ctx, lr_ctx = _inproj(ctx.reshape(b * tc, d), mod3, lambda i: b, norm1_g[li][None, :],
                            w_main, w_lr, None, tm_ctx, qkv_w)
    s_zero = jnp.zeros((b, HEADS, DK, DV), F32)
    s_f, s_b = _gla(p_ctx.reshape(b, tc, qkv_w), lr_ctx.reshape(b, tc, LR_PAD),
                    wlr_f, blr_f, wlr_b, blr_b, s_zero, s_zero, emit_o=False)

    tm = min(1024, t)
    tiles = t // tm
    x2 = x.reshape(b * t, d)
    p, lr = _inproj(x2, mod3, lambda i: i // tiles, norm1_g[li][None, :], w_main, w_lr,
                    (rt3, ct, tiles), tm, P_WIDTH)
    p3 = p.reshape(b, t, P_WIDTH)
    o_f, o_b, _, _, w1, wgo, wfo, wo = _gla(
        p3, lr.reshape(b, t, LR_PAD), wlr_f, blr_f, wlr_b, blr_b, s_f, s_b, emit_o=True,
        riders=(w_mlp_in[li], w_gla_out[li], w_fourier_out[li], w_out[li]))

    xr, xi, w2 = _fft(p3, QKVG_W, fmat, twc, tws, riders=(w_mlp_out[li],))

    x1 = _merge(o_f.reshape(b * t, VAL_W), o_b.reshape(b * t, VAL_W), p,
                xr.reshape(b * t, FWIDTH), xi.reshape(b * t, FWIDTH), x2, rt3, ct, mod3,
                gla_norm_g[li][None, :], wgo, wfo, wo, csg, t)

    out = _mlp(x1, mod3, norm2_g[li][None, :], w1, w2, final_norm_g[None, :], t)
    return out.reshape(b, t, d)
```

```python
import functools
import math

import jax
import jax.numpy as jnp
import numpy as np
from jax import lax
from jax.experimental import pallas as pl
from jax.experimental.pallas import tpu as pltpu

F32 = jnp.float32
BF16 = jnp.bfloat16

D_MODEL = 2048
GRID_W = 64
HEADS = 4
DK = 128
DV = 256
KEY_W = HEADS * DK
VAL_W = HEADS * DV
RANK = 16
TAU = 16.0
FGROUPS = 4
FGDIM = 256
FWIDTH = FGROUPS * FGDIM
D_FF = 4 * D_MODEL
N_MOD = 6
N_MOD_EARLY = 2
EPS = 1e-6
POS_TEMP = 10000.0

QKVG_W = 2 * KEY_W + 2 * VAL_W
P_WIDTH = QKVG_W + FWIDTH + 2 * D_MODEL
LR_PAD = 128
GLA_CHUNK = 128
GLA_CHUNKS_PER_STEP = 4
RADIX = 8
LANES = 128

V7X_VMEM_LIMIT = 56 * 1024 * 1024
BIG_TILE_VMEM_LIMIT = 63 * 1024 * 1024
MLP_CHUNK = 2048


def _cparams(sem, vmem=V7X_VMEM_LIMIT):
    return pltpu.CompilerParams(dimension_semantics=sem, vmem_limit_bytes=vmem)


def _silu(x):
    return x * jax.nn.sigmoid(x)


def _mod_kernel(ct_ref, w_ref, b_ref, o_ref, *, n_used):
    st = _silu(ct_ref[...])
    w = w_ref[...]
    bias = b_ref[...]
    rows = [jnp.sum(w * st[:, m:m + 1], axis=0, keepdims=True) + bias for m in range(n_used)]
    rows += [bias] * (o_ref.shape[0] - n_used)
    o_ref[...] = jnp.concatenate(rows, axis=0)


def _mod(cpad_t, n_used, w_mod, b_mod, n):
    d, rows = cpad_t.shape
    tn = 1024
    return pl.pallas_call(
        functools.partial(_mod_kernel, n_used=n_used),
        out_shape=jax.ShapeDtypeStruct((rows, n), F32),
        grid=(n // tn,),
        in_specs=[pl.BlockSpec((d, rows), lambda j: (0, 0)),
                  pl.BlockSpec((d, tn), lambda j: (0, j)),
                  pl.BlockSpec((1, tn), lambda j: (0, j))],
        out_specs=pl.BlockSpec((rows, tn), lambda j: (0, j)),
        compiler_params=_cparams(("arbitrary",)),
        name="mod",
    )(cpad_t, w_mod, b_mod)


def _wprep_kernel(wt_ref, lrt_ref, wm_ref, wlr_ref):
    wm_ref[...] = wt_ref[...].T.astype(BF16)

    @pl.when(pl.program_id(0) == 0)
    def _():
        lrt = lrt_ref[...]
        r = lax.broadcasted_iota(jnp.int32, lrt.shape, 0)
        wlr_ref[...] = jnp.where(r < 2 * RANK, lrt, 0.0).T.astype(BF16)


def _wprep(w_in_t):
    n, d = w_in_t.shape
    tc = 512
    n_before = QKVG_W // tc

    def src_row(j):
        return pl.multiple_of(jnp.where(j >= n_before, j * tc + 2 * RANK, j * tc), 2 * RANK)

    return pl.pallas_call(
        _wprep_kernel,
        out_shape=(jax.ShapeDtypeStruct((d, P_WIDTH), BF16), jax.ShapeDtypeStruct((d, LR_PAD), BF16)),
        grid=(P_WIDTH // tc,),
        in_specs=[pl.BlockSpec((pl.Element(tc), pl.Element(d)), lambda j: (src_row(j), 0)),
                  pl.BlockSpec((LR_PAD, d), lambda j: (QKVG_W // LR_PAD, 0))],
        out_specs=(pl.BlockSpec((d, tc), lambda j: (0, j)), pl.BlockSpec((d, LR_PAD), lambda j: (0, 0))),
        compiler_params=_cparams(("arbitrary",)),
        name="wprep",
    )(w_in_t, w_in_t)


def _add_pos(x, rt_ref, ct_ref):
    tm, d = x.shape
    x3 = x.reshape(tm // GRID_W, GRID_W, d)
    half = d // 2
    lo = x3[:, :, :half] + rt_ref[...]
    hi = x3[:, :, half:] + ct_ref[...][None]
    return jnp.concatenate([lo, hi], axis=-1).reshape(tm, d)


def _rms(x, g):
    return x * lax.rsqrt(jnp.mean(x * x, axis=-1, keepdims=True) + EPS) * g


def _inproj_kernel(*refs, add_pos):
    if add_pos:
        x_ref, rt_ref, ct_ref, sh_ref, sc_ref, g_ref, w_ref, wlr_ref, p_ref, lr_ref, h_scr = refs
    else:
        x_ref, sh_ref, sc_ref, g_ref, w_ref, wlr_ref, p_ref, lr_ref, h_scr = refs

    @pl.when(pl.program_id(1) == 0)
    def _():
        x = x_ref[...]
        if add_pos:
            x = _add_pos(x, rt_ref, ct_ref)
        h = _rms(x, g_ref[...]) * (1.0 + sc_ref[...]) + sh_ref[...]
        hb = h.astype(BF16)
        h_scr[...] = hb
        lr_ref[...] = jnp.dot(hb, wlr_ref[...], preferred_element_type=F32)

    p_ref[...] = jnp.dot(h_scr[...], w_ref[...], preferred_element_type=F32).astype(BF16)


def _inproj(x2, mod3, mod_row_of_tile, norm_g, w_main, w_lr, pos_tabs, tm, width):
    ntok, d = x2.shape
    tn = 2048
    add_pos = pos_tabs is not None
    in_specs = [pl.BlockSpec((tm, d), lambda i, j: (i, 0))]
    args = [x2]
    if add_pos:
        rt3, ct, tiles_per_seq = pos_tabs
        rpt = tm // GRID_W
        in_specs += [pl.BlockSpec((rpt, 1, d // 2), lambda i, j: (i % tiles_per_seq, 0, 0)),
                     pl.BlockSpec((GRID_W, d // 2), lambda i, j: (0, 0))]
        args += [rt3, ct]
    in_specs += [pl.BlockSpec((None, 1, d), lambda i, j: (mod_row_of_tile(i), 0, 0)),
                 pl.BlockSpec((None, 1, d), lambda i, j: (mod_row_of_tile(i), 0, 1)),
                 pl.BlockSpec((1, d), lambda i, j: (0, 0)),
                 pl.BlockSpec((d, tn), lambda i, j: (0, j)),
                 pl.BlockSpec((d, LR_PAD), lambda i, j: (0, 0))]
    args += [mod3, mod3, norm_g, w_main, w_lr]
    return pl.pallas_call(
        functools.partial(_inproj_kernel, add_pos=add_pos),
        out_shape=(jax.ShapeDtypeStruct((ntok, width), BF16),
                   jax.ShapeDtypeStruct((ntok, LR_PAD), F32)),
        grid=(ntok // tm, width // tn),
        in_specs=in_specs,
        out_specs=(pl.BlockSpec((tm, tn), lambda i, j: (i, j)),
                   pl.BlockSpec((tm, LR_PAD), lambda i, j: (i, 0))),
        scratch_shapes=[pltpu.VMEM((tm, d), BF16)],
        compiler_params=_cparams(("arbitrary", "arbitrary"), vmem=BIG_TILE_VMEM_LIMIT),
        name="inproj_pos" if add_pos else "inproj_ctx",
    )(*args)


def _log_sigmoid(z):
    return jnp.minimum(z, 0.0) - jnp.log1p(jnp.exp(-jnp.abs(z)))


def _gla_chunks(dirs, c):
    n_sub = dirs[0][0].shape[0] // c
    row = lax.broadcasted_iota(jnp.int32, (c, c), 0)
    col = lax.broadcasted_iota(jnp.int32, (c, c), 1)

    chains = {}
    log2_qscale = math.log2(DK ** -0.5)

    def front(step):
        for di, (qk_ref, v_ref, lr_ref, wlr_ref, blr_ref, s_scr, o_ref, backward) in enumerate(dirs):
            keep = (row <= col) if backward else (row >= col)
            tri = jnp.where(keep, 1.0, 0.0).astype(BF16)
            last = 0 if backward else c - 1
            mid = c // 2 if backward else c // 2 - 1
            sub = n_sub - 1 - step if backward else step
            rows = slice(sub * c, (sub + 1) * c)
            z = jnp.dot(lr_ref[rows, :].astype(BF16), wlr_ref[...], preferred_element_type=F32) + blr_ref[...]
            la = _log_sigmoid(z) * (math.log2(math.e) / TAU)
            la_hi = la.astype(BF16)
            la_lo = (la - la_hi.astype(F32)).astype(BF16)
            cum2 = jnp.dot(tri, jnp.concatenate([la_hi, la_lo], axis=1), preferred_element_type=F32)
            cum_all = cum2[:, :KEY_W] + cum2[:, KEY_W:]
            for h in range(HEADS):
                cum = cum_all[:, h * DK:(h + 1) * DK]
                tot = cum[last:last + 1, :]
                ref_pt = cum[mid:mid + 1, :]
                q = qk_ref[rows, h * DK:(h + 1) * DK].astype(F32)
                k = qk_ref[rows, KEY_W + h * DK:KEY_W + (h + 1) * DK].astype(F32)
                chains[(step, di, h)] = dict(
                    keep=keep, rows=rows,
                    v=v_ref[rows, h * DV:(h + 1) * DV],
                    q_mid=(q * jnp.exp2(cum - (ref_pt - log2_qscale))).astype(BF16),
                    k_mid=(k * jnp.exp2(ref_pt - cum)).astype(BF16),
                    q_dec=(q * jnp.exp2(cum + log2_qscale)).astype(BF16),
                    k_end_t=(k * jnp.exp2(tot - cum)).T.astype(BF16),
                    dec_col=jnp.broadcast_to(jnp.exp2(tot), (DK, DK)).T)

    def middle(step):
        for di in range(len(dirs)):
            for h in range(HEADS):
                ch = chains[(step, di, h)]
                s = lax.dot_general(ch["q_mid"], ch["k_mid"], (((1,), (1,)), ((), ())),
                                    preferred_element_type=F32)
                ch["scores"] = jnp.where(ch["keep"], s, 0.0).astype(BF16)
                ch["kv"] = jnp.dot(ch["k_end_t"], ch["v"], preferred_element_type=F32)

    state = {(di, h): d[5][h] for di, d in enumerate(dirs) for h in range(HEADS)}

    def tail(step):
        for di, d in enumerate(dirs):
            o_ref = d[6]
            for h in range(HEADS):
                ch = chains.pop((step, di, h))
                s_prev = state[(di, h)]
                lhs = jnp.concatenate([ch["scores"], ch["q_dec"]], axis=1)
                rhs = jnp.concatenate([ch["v"], s_prev.astype(BF16)], axis=0)
                o = jnp.dot(lhs, rhs, preferred_element_type=F32)
                if o_ref is not None:
                    o_ref[ch["rows"], h * DV:(h + 1) * DV] = o.astype(o_ref.dtype)
                dec = jnp.concatenate([ch["dec_col"]] * (DV // DK), axis=1)
                state[(di, h)] = s_prev * dec + ch["kv"]

    front(0)
    for step in range(n_sub):
        if step + 1 < n_sub:
            front(step + 1)
        middle(step)
        tail(step)
    for (di, h), s in state.items():
        dirs[di][5][h] = s


def _rider_specs(weights, n_steps, flat_step):
    specs, shapes = [], []
    for w in weights:
        rows, cols = w.shape
        assert rows % (16 * n_steps) == 0
        specs.append(pl.BlockSpec((rows // n_steps, cols), lambda *idx: (flat_step(*idx), 0)))
        shapes.append(jax.ShapeDtypeStruct(w.shape, BF16))
    return specs, list(specs), shapes


def _cast_riders(in_refs, out_refs):
    for src, dst in zip(in_refs, out_refs):
        dst[...] = src[...].astype(dst.dtype)


def _gla_kernel(*refs, emit_o, n_riders):
    (qkf, vf, lrf, qkb, vb, lrb, wf, bf, wb, bb, s0f, s0b) = refs[:12]
    rider_in, refs = refs[12:12 + n_riders], refs[12 + n_riders:]
    if emit_o:
        of, ob, sf_out, sb_out = refs[:4]
        rider_out = refs[4:4 + n_riders]
    else:
        sf_out, sb_out = refs[:2]
        rider_out = refs[2:2 + n_riders]
        of = ob = None
    s_scr = refs[-1]
    i = pl.program_id(1)

    @pl.when(i == 0)
    def _():
        s_scr[0] = s0f[...]
        s_scr[1] = s0b[...]

    _cast_riders(rider_in, rider_out)

    _gla_chunks([(qkf, vf, lrf, wf, bf, s_scr.at[0], of, False),
                 (qkb, vb, lrb, wb, bb, s_scr.at[1], ob, True)], GLA_CHUNK)

    @pl.when(i == pl.num_programs(1) - 1)
    def _():
        sf_out[...] = s_scr[0]
        sb_out[...] = s_scr[1]


def _gla(p3, lr3, wlr_f, blr_f, wlr_b, blr_b, s0f, s0b, emit_o, riders=()):
    b, t, _ = p3.shape
    per_step = max(s for s in range(1, GLA_CHUNKS_PER_STEP + 1) if t % (GLA_CHUNK * s) == 0)
    c = GLA_CHUNK * per_step
    n = t // c
    fwd = lambda blk: (lambda bi, i: (bi, i, blk))
    bwd = lambda blk: (lambda bi, i: (bi, n - 1 - i, blk))

    def seq_specs(mk):
        return [pl.BlockSpec((None, c, 2 * KEY_W), mk(0)),
                pl.BlockSpec((None, c, VAL_W), mk(1)),
                pl.BlockSpec((None, c, LR_PAD), mk(0))]
    full2 = lambda shape: pl.BlockSpec(shape, lambda bi, i: (0, 0))
    st_spec = pl.BlockSpec((None, HEADS, DK, DV), lambda bi, i: (bi, 0, 0, 0))
    in_specs = (seq_specs(fwd) + seq_specs(bwd)
                + [full2(wlr_f.shape), full2(blr_f.shape), full2(wlr_b.shape), full2(blr_b.shape),
                   st_spec, st_spec])
    st_shape = jax.ShapeDtypeStruct((b, HEADS, DK, DV), F32)
    out_shape = [st_shape, st_shape]
    out_specs = [st_spec, st_spec]
    if emit_o:
        o_shape = jax.ShapeDtypeStruct((b, t, VAL_W), BF16)
        out_shape = [o_shape, o_shape] + out_shape
        out_specs = [pl.BlockSpec((None, c, VAL_W), fwd(0)), pl.BlockSpec((None, c, VAL_W), bwd(0))] + out_specs
    r_in, r_out, r_shapes = _rider_specs(riders, b * n, lambda bi, i: bi * n + i)
    return pl.pallas_call(
        functools.partial(_gla_kernel, emit_o=emit_o, n_riders=len(riders)),
        out_shape=tuple(out_shape + r_shapes),
        grid=(b, n),
        in_specs=in_specs + r_in,
        out_specs=tuple(out_specs + r_out),
        scratch_shapes=[pltpu.VMEM((2, HEADS, DK, DV), F32)],
        compiler_params=_cparams(("arbitrary", "arbitrary")),
        name="gla_seq" if emit_o else "gla_ctx",
    )(p3, p3, lr3, p3, p3, lr3, wlr_f, blr_f, wlr_b, blr_b, s0f, s0b, *riders)


def _cadd(a, b):
    return a[0] + b[0], a[1] + b[1]


def _csub(a, b):
    return a[0] - b[0], a[1] - b[1]


def _cmul_neg_i(a):
    return a[1], -a[0]


def _dft4(y):
    t0, t1 = _cadd(y[0], y[2]), _csub(y[0], y[2])
    t2, t3 = _cadd(y[1], y[3]), _cmul_neg_i(_csub(y[1], y[3]))
    return [_cadd(t0, t2), _cadd(t1, t3), _csub(t0, t2), _csub(t1, t3)]


def _dft8(z):
    r = math.sqrt(0.5)
    s = [_cadd(z[a], z[a + 4]) for a in range(4)]
    d = [_csub(z[a], z[a + 4]) for a in range(4)]
    d1 = ((d[1][0] + d[1][1]) * r, (d[1][1] - d[1][0]) * r)
    d2 = _cmul_neg_i(d[2])
    d3 = ((d[3][1] - d[3][0]) * r, (-d[3][1] - d[3][0]) * r)
    ev = _dft4(s)
    od = _dft4([d[0], d1, d2, d3])
    out = [None] * 8
    for j in range(4):
        out[2 * j] = ev[j]
        out[2 * j + 1] = od[j]
    return out


def _fft_kernel(*refs, n_slabs, n_riders, n_mod_rows):
    u_refs = refs[:n_slabs]
    f_ref, twc_ref, tws_ref, ct_ref, wmod_ref, bmod_ref = refs[n_slabs:n_slabs + 6]
    rider_in = refs[n_slabs + 6:n_slabs + 6 + n_riders]
    xr_ref, xi_ref, mod_ref = refs[n_slabs + 6 + n_riders:n_slabs + 9 + n_riders]
    rider_out = refs[n_slabs + 9 + n_riders:n_slabs + 9 + 2 * n_riders]
    wide_scr, ub_scr = refs[-2:]
    m = ub_scr.shape[1]

    @pl.when(pl.program_id(2) == 0)
    def _():
        for s in range(n_slabs):
            wide_scr[s] = u_refs[s][...].astype(F32)
        for a in range(RADIX):
            rows = [wide_scr[s, pl.ds(a, m, stride=RADIX), :].astype(BF16) for s in range(n_slabs)]
            ub_scr[a] = jnp.concatenate(rows, axis=1)

    _cast_riders(rider_in, rider_out)
    _mod_kernel(ct_ref, wmod_ref, bmod_ref, mod_ref, n_used=n_mod_rows)

    t = f_ref.shape[0] // 2
    f = f_ref[...]
    z = []
    for a in range(RADIX):
        za = jnp.dot(f, ub_scr[a], preferred_element_type=F32)
        zr, zi = za[:t], za[t:]
        cc, ss = twc_ref[a], tws_ref[a]
        z.append((zr * cc + zi * ss, zi * cc - zr * ss))
    x = _dft8(z)
    for k1 in range(RADIX):
        xr_ref[k1] = x[k1][0].astype(xr_ref.dtype)
        xi_ref[k1] = x[k1][1].astype(xi_ref.dtype)


def _fft(p3, col0, fmat, twc, tws, mod_job, riders=()):
    cpad_t, n_mod_rows, w_mod, b_mod, mod_col0 = mod_job
    b, seq, _ = p3.shape
    w = FWIDTH
    m = seq // RADIX
    nt, t2, _ = fmat.shape
    t = t2 // 2
    chb = 256
    n_slabs = chb // LANES
    slab0 = col0 // LANES
    out = jax.ShapeDtypeStruct((b, RADIX, m, w), BF16)
    o_spec = pl.BlockSpec((None, RADIX, t, chb), lambda bi, cj, kt: (bi, 0, kt, cj))
    slab = lambda s: pl.BlockSpec((None, seq, LANES), lambda bi, cj, kt: (bi, 0, slab0 + cj * n_slabs + s))
    ncj = w // chb
    n_steps = b * ncj * nt
    flat = lambda bi, cj, kt: (bi * ncj + cj) * nt + kt
    r_in, r_out, r_shapes = _rider_specs(riders, n_steps, flat)
    d_mod, mod_rows = cpad_t.shape
    n_late = w_mod.shape[1] - mod_col0
    cps = n_late // n_steps
    assert n_late % n_steps == 0 and cps % LANES == 0 and mod_col0 % cps == 0
    return pl.pallas_call(
        functools.partial(_fft_kernel, n_slabs=n_slabs, n_riders=len(riders), n_mod_rows=n_mod_rows),
        out_shape=tuple([out, out, jax.ShapeDtypeStruct((mod_rows, n_late), F32)] + r_shapes),
        grid=(b, ncj, nt),
        in_specs=[slab(s) for s in range(n_slabs)] + [
                  pl.BlockSpec((None, t2, m), lambda bi, cj, kt: (kt, 0, 0)),
                  pl.BlockSpec((RADIX, t, 1), lambda bi, cj, kt: (0, kt, 0)),
                  pl.BlockSpec((RADIX, t, 1), lambda bi, cj, kt: (0, kt, 0)),
                  pl.BlockSpec((d_mod, mod_rows), lambda bi, cj, kt: (0, 0)),
                  pl.BlockSpec((d_mod, cps), lambda *idx: (0, mod_col0 // cps + flat(*idx))),
                  pl.BlockSpec((1, cps), lambda *idx: (0, mod_col0 // cps + flat(*idx)))] + r_in,
        out_specs=tuple([o_spec, o_spec, pl.BlockSpec((mod_rows, cps), lambda *idx: (0, flat(*idx)))] + r_out),
        scratch_shapes=[pltpu.VMEM((n_slabs, seq, LANES), F32), pltpu.VMEM((RADIX, m, chb), BF16)],
        compiler_params=_cparams(("arbitrary", "arbitrary", "arbitrary")),
        name="fft",
    )(*([p3] * n_slabs), fmat, twc, tws, cpad_t, w_mod, b_mod, *riders)


def _merge_kernel(of_ref, ob_ref, g_ref, gates_ref, xr_ref, xi_ref, x_ref, rt_ref, ct_ref, gt_ref,
                  gng_ref, wgo_ref, wfo_ref, wo_ref, csg_ref, o_ref):
    o = of_ref[...].astype(F32) + ob_ref[...].astype(F32)
    g = g_ref[...].astype(F32)
    gng = gng_ref[...]
    heads = []
    for h in range(HEADS):
        sl = slice(h * DV, (h + 1) * DV)
        heads.append((_rms(o[:, sl], gng) * _silu(g[:, sl])).astype(BF16))
    y_gla = jnp.dot(jnp.concatenate(heads, axis=1), wgo_ref[...], preferred_element_type=F32)

    csg = csg_ref[...]
    groups = []
    for gi in range(FGROUPS):
        sl = slice(gi * FGDIM, (gi + 1) * FGDIM)
        xg = jnp.concatenate([xr_ref[:, sl], xi_ref[:, sl]], axis=1)
        groups.append(jnp.dot(xg, csg, preferred_element_type=F32).astype(BF16))
    y_fft = jnp.dot(jnp.concatenate(groups, axis=1), wfo_ref[...], preferred_element_type=F32)

    gates = jax.nn.sigmoid(gates_ref[...].astype(F32))
    d = y_fft.shape[1]
    zmix = (gates[:, :d] * y_fft + gates[:, d:] * y_gla).astype(BF16)
    y = jnp.dot(zmix, wo_ref[...], preferred_element_type=F32)
    o_ref[...] = _add_pos(x_ref[...], rt_ref, ct_ref) + gt_ref[...] * y


def _const_spec(shape):
    zeros = (0,) * len(shape)
    return pl.BlockSpec(shape, lambda *idx: zeros, pipeline_mode=pl.Buffered(1))


def _merge(o_f, o_b, p2, xr, xi, x2, rt3, ct, mod3, gng, wgo, wfo, wo, csg, t):
    ntok, d = x2.shape
    tm = 256
    tiles_per_seq = t // tm
    rpt = tm // GRID_W
    row = lambda blk: (lambda i: (i, blk))
    in_specs = [pl.BlockSpec((tm, VAL_W), row(0)),
                pl.BlockSpec((tm, VAL_W), row(0)),
                pl.BlockSpec((tm, VAL_W), row(2)),
                pl.BlockSpec((tm, 2 * d), row(1)),
                pl.BlockSpec((tm, FWIDTH), row(0)),
                pl.BlockSpec((tm, FWIDTH), row(0)),
                pl.BlockSpec((tm, d), row(0)),
                pl.BlockSpec((rpt, 1, d // 2), lambda i: (i % tiles_per_seq, 0, 0)),
                _const_spec((GRID_W, d // 2)),
                pl.BlockSpec((None, 1, d), lambda i: (i // tiles_per_seq, 0, 2 - N_MOD_EARLY)),
                _const_spec(gng.shape), _const_spec(wgo.shape), _const_spec(wfo.shape),
                _const_spec(wo.shape), _const_spec(csg.shape)]
    return pl.pallas_call(
        _merge_kernel,
        out_shape=jax.ShapeDtypeStruct((ntok, d), F32),
        grid=(ntok // tm,),
        in_specs=in_specs,
        out_specs=pl.BlockSpec((tm, d), row(0)),
        compiler_params=_cparams(("arbitrary",)),
        name="merge",
    )(o_f, o_b, p2, p2, xr, xi, x2, rt3, ct, mod3, gng, wgo, wfo, wo, csg)


def _mlp_kernel(x_ref, xn_ref, sh_ref, sc_ref, shn_ref, scn_ref, gt_ref, g2_ref, w1_ref, w2_ref, fg_ref,
                o_ref, h_even, h_odd):
    i, j = pl.program_id(0), pl.program_id(1)
    nf = pl.num_programs(1)
    tm = x_ref.shape[0]
    slab = tm // nf
    slot = i % 2

    def normed(x, sh, sc):
        return (_rms(x, g2_ref[...]) * (1.0 + sc) + sh).astype(BF16)

    @pl.when((i == 0) & (j == 0))
    def _():
        h_even[...] = normed(x_ref[...], sh_ref[...], sc_ref[...])

    @pl.when(j == 0)
    def _():
        o_ref[...] = jnp.zeros_like(o_ref)

    def step(h_cur, h_next):
        r0 = pl.multiple_of(j * slab, slab)
        h_next[pl.ds(r0, slab), :] = normed(xn_ref[...], shn_ref[...], scn_ref[...])
        tf = w1_ref.shape[1]
        for c0 in range(0, tf, MLP_CHUNK):
            hid = jnp.dot(h_cur[...], w1_ref[:, c0:c0 + MLP_CHUNK], preferred_element_type=F32)
            hid = jnp.square(jnp.maximum(hid, 0.0)).astype(BF16)
            o_ref[...] += jnp.dot(hid, w2_ref[c0:c0 + MLP_CHUNK, :], preferred_element_type=F32)

    @pl.when(slot == 0)
    def _():
        step(h_even, h_odd)

    @pl.when(slot == 1)
    def _():
        step(h_odd, h_even)

    @pl.when(j == nf - 1)
    def _():
        xo = x_ref[...] + gt_ref[...] * o_ref[...]
        o_ref[...] = _rms(xo, fg_ref[...])


def _mlp(x2, mod3, g2, w1, w2, fg, t):
    ntok, d = x2.shape
    dff = w1.shape[1]
    tm, tf = 512, 2048
    nf = dff // tf
    slab = tm // nf
    assert slab % 8 == 0 and tf % MLP_CHUNK == 0
    nt = ntok // tm
    tiles_per_seq = t // tm
    nxt = lambda i: jnp.minimum(i + 1, nt - 1)
    modspec = lambda blk: pl.BlockSpec((None, 1, d), lambda i, j: (i // tiles_per_seq, 0, blk))
    modspec_next = lambda blk: pl.BlockSpec((None, 1, d), lambda i, j: (nxt(i) // tiles_per_seq, 0, blk))
    return pl.pallas_call(
        _mlp_kernel,
        out_shape=jax.ShapeDtypeStruct((ntok, d), F32),
        grid=(nt, dff // tf),
        in_specs=[pl.BlockSpec((tm, d), lambda i, j: (jnp.where(j >= 2, i, jnp.maximum(i - 1, 0)), 0)),
                  pl.BlockSpec((slab, d), lambda i, j: (nxt(i) * nf + j, 0)),
                  modspec(3 - N_MOD_EARLY), modspec(4 - N_MOD_EARLY), modspec_next(3 - N_MOD_EARLY),
                  modspec_next(4 - N_MOD_EARLY), modspec(5 - N_MOD_EARLY),
                  pl.BlockSpec((1, d), lambda i, j: (0, 0)),
                  pl.BlockSpec((d, tf), lambda i, j: (0, j)),
                  pl.BlockSpec((tf, d), lambda i, j: (j, 0)),
                  pl.BlockSpec((1, d), lambda i, j: (0, 0))],
        out_specs=pl.BlockSpec((tm, d), lambda i, j: (i, 0)),
        scratch_shapes=[pltpu.VMEM((tm, d), BF16), pltpu.VMEM((tm, d), BF16)],
        compiler_params=_cparams(("arbitrary", "arbitrary"), vmem=BIG_TILE_VMEM_LIMIT),
        name="mlp",
    )(x2, x2, mod3, mod3, mod3, mod3, mod3, g2, w1, w2, fg)


def _pos_tables(t, d):
    quarter = d // 4
    omega = 1.0 / (POS_TEMP ** (jnp.arange(quarter, dtype=F32) / quarter))
    er = jnp.arange(t // GRID_W, dtype=F32)[:, None] * omega[None, :]
    ec = jnp.arange(GRID_W, dtype=F32)[:, None] * omega[None, :]
    rt = jnp.concatenate([jnp.sin(er), jnp.cos(er)], axis=-1)
    ct = jnp.concatenate([jnp.sin(ec), jnp.cos(ec)], axis=-1)
    return rt[:, None, :], ct


def _dft_tables(t):
    m = t // RADIX
    tile = min(256, m)
    k = np.arange(m)
    ang = 2.0 * np.pi * ((k[:, None] * k[None, :]) % m) / m
    cos_t = np.cos(ang).reshape(m // tile, tile, m)
    sin_t = np.sin(ang).reshape(m // tile, tile, m)
    fmat = np.concatenate([cos_t, -sin_t], axis=1)
    a = np.arange(RADIX)
    tw = 2.0 * np.pi * (a[:, None] * k[None, :]) / t
    scale = 1.0 / math.sqrt(t * FGDIM)
    twc = (np.cos(tw) * scale)[:, :, None]
    tws = (np.sin(tw) * scale)[:, :, None]
    c = np.arange(FGDIM)
    cang = 2.0 * np.pi * ((c[:, None] * c[None, :]) % FGDIM) / FGDIM
    csg = np.concatenate([np.cos(cang), np.sin(cang)], axis=0)
    as_f32 = lambda a: jnp.asarray(a.astype(np.float32))
    return as_f32(fmat).astype(BF16), as_f32(twc), as_f32(tws), as_f32(csg).astype(BF16)


def _pad_lr_weight(w_lr, row0):
    out = jnp.zeros((LR_PAD, KEY_W), F32)
    return out.at[row0:row0 + RANK].set(w_lr).astype(BF16)


def kernel(x, c, ctx, c_ctx, w_mod, b_mod, norm1_g, norm2_g, w_in, w_lr_f, b_lr_f, w_lr_b, b_lr_b,
           gla_norm_g, w_fourier_out, w_gla_out, w_out, w_mlp_in, w_mlp_out, final_norm_g):
    b, t, d = x.shape
    tc = ctx.shape[1]
    depth = w_mod.shape[0]
    assert depth == 1 and d == D_MODEL and t % (RADIX * GRID_W) == 0 and tc % GLA_CHUNK == 0
    li = 0

    w_main, w_lr = _wprep(w_in[li].T)
    wlr_f, wlr_b = _pad_lr_weight(w_lr_f[li], 0), _pad_lr_weight(w_lr_b[li], RANK)
    blr_f, blr_b = b_lr_f[li][None, :], b_lr_b[li][None, :]

    rt3, ct = _pos_tables(t, d)
    fmat, twc, tws, csg = _dft_tables(t)

    rows = 8
    cpad_t = jnp.concatenate([c, c_ctx[None, :], jnp.zeros((rows - b - 1, d), F32)], axis=0).T
    n_early = N_MOD_EARLY * d
    mod3 = _mod(cpad_t, b + 1, w_mod[li], b_mod[li][None, :], n_early).reshape(rows, 1, n_early)

    tm_ctx = min(512, b * tc)
    qkv_w = 2 * KEY_W + VAL_W
    p_ctx, lr_ctx = _inproj(ctx.reshape(b * tc, d), mod3, lambda i: b, norm1_g[li][None, :],
                            w_main, w_lr, None, tm_ctx, qkv_w)
    s_zero = jnp.zeros((b, HEADS, DK, DV), F32)
    s_f, s_b = _gla(p_ctx.reshape(b, tc, qkv_w), lr_ctx.reshape(b, tc, LR_PAD),
                    wlr_f, blr_f, wlr_b, blr_b, s_zero, s_zero, emit_o=False)

    tm = min(1024, t)
    tiles = t // tm
    x2 = x.reshape(b * t, d)
    p, lr = _inproj(x2, mod3, lambda i: i // tiles, norm1_g[li][None, :], w_main, w_lr,
                    (rt3, ct, tiles), tm, P_WIDTH)
    p3 = p.reshape(b, t, P_WIDTH)
    o_f, o_b, _, _, w1, wgo, wfo, wo = _gla(
        p3, lr.reshape(b, t, LR_PAD), wlr_f, blr_f, wlr_b, blr_b, s_f, s_b, emit_o=True,
        riders=(w_mlp_in[li], w_gla_out[li], w_fourier_out[li], w_out[li]))

    xr, xi, mod_late, w2 = _fft(p3, QKVG_W, fmat, twc, tws,
                                (cpad_t, b + 1, w_mod[li], b_mod[li][None, :], n_early),
                                riders=(w_mlp_out[li],))
    mod_late3 = mod_late.reshape(rows, 1, N_MOD * d - n_early)

    x1 = _merge(o_f.reshape(b * t, VAL_W), o_b.reshape(b * t, VAL_W), p,
                xr.reshape(b * t, FWIDTH), xi.reshape(b * t, FWIDTH), x2, rt3, ct, mod_late3,
                gla_norm_g[li][None, :], wgo, wfo, wo, csg, t)

    out = _mlp(x1, mod_late3, norm2_g[li][None, :], w1, w2, final_norm_g[None, :], t)
    return out.reshape(b, t, d)
```

```python
import functools
import math

import jax
import jax.numpy as jnp
import numpy as np
from jax import lax
from jax.experimental import pallas as pl
from jax.experimental.pallas import tpu as pltpu

F32 = jnp.float32
BF16 = jnp.bfloat16

D_MODEL = 2048
GRID_W = 64
HEADS = 4
DK = 128
DV = 256
KEY_W = HEADS * DK
VAL_W = HEADS * DV
RANK = 16
TAU = 16.0
FGROUPS = 4
FGDIM = 256
FWIDTH = FGROUPS * FGDIM
D_FF = 4 * D_MODEL
N_MOD = 6
N_MOD_EARLY = 2
EPS = 1e-6
POS_TEMP = 10000.0

QKVG_W = 2 * KEY_W + 2 * VAL_W
P_WIDTH = QKVG_W + FWIDTH + 2 * D_MODEL
LR_PAD = 128
GLA_CHUNK = 128
GLA_CHUNKS_PER_STEP = 4
RADIX = 8
LANES = 128

V7X_VMEM_LIMIT = 56 * 1024 * 1024
BIG_TILE_VMEM_LIMIT = 63 * 1024 * 1024
MLP_CHUNK = 2048


def _cparams(sem, vmem=V7X_VMEM_LIMIT):
    return pltpu.CompilerParams(dimension_semantics=sem, vmem_limit_bytes=vmem)


def _silu(x):
    return x * jax.nn.sigmoid(x)


def _mod_kernel(ct_ref, w_ref, b_ref, o_ref, *, n_used):
    st = _silu(ct_ref[...])
    w = w_ref[...]
    bias = b_ref[...]
    rows = [jnp.sum(w * st[:, m:m + 1], axis=0, keepdims=True) + bias for m in range(n_used)]
    rows += [bias] * (o_ref.shape[0] - n_used)
    o_ref[...] = jnp.concatenate(rows, axis=0)


def _mod(cpad_t, n_used, w_mod, b_mod, n):
    d, rows = cpad_t.shape
    tn = 1024
    return pl.pallas_call(
        functools.partial(_mod_kernel, n_used=n_used),
        out_shape=jax.ShapeDtypeStruct((rows, n), F32),
        grid=(n // tn,),
        in_specs=[pl.BlockSpec((d, rows), lambda j: (0, 0)),
                  pl.BlockSpec((d, tn), lambda j: (0, j)),
                  pl.BlockSpec((1, tn), lambda j: (0, j))],
        out_specs=pl.BlockSpec((rows, tn), lambda j: (0, j)),
        compiler_params=_cparams(("arbitrary",)),
        name="mod",
    )(cpad_t, w_mod, b_mod)


def _wprep_kernel(wt_ref, lrt_ref, wm_ref, wlr_ref):
    wm_ref[...] = wt_ref[...].T.astype(BF16)

    @pl.when(pl.program_id(0) == 0)
    def _():
        lrt = lrt_ref[...]
        r = lax.broadcasted_iota(jnp.int32, lrt.shape, 0)
        wlr_ref[...] = jnp.where(r < 2 * RANK, lrt, 0.0).T.astype(BF16)


def _wprep(w_in_t):
    n, d = w_in_t.shape
    tc = 512
    n_before = QKVG_W // tc

    def src_row(j):
        return pl.multiple_of(jnp.where(j >= n_before, j * tc + 2 * RANK, j * tc), 2 * RANK)

    return pl.pallas_call(
        _wprep_kernel,
        out_shape=(jax.ShapeDtypeStruct((d, P_WIDTH), BF16), jax.ShapeDtypeStruct((d, LR_PAD), BF16)),
        grid=(P_WIDTH // tc,),
        in_specs=[pl.BlockSpec((pl.Element(tc), pl.Element(d)), lambda j: (src_row(j), 0)),
                  pl.BlockSpec((LR_PAD, d), lambda j: (QKVG_W // LR_PAD, 0))],
        out_specs=(pl.BlockSpec((d, tc), lambda j: (0, j)), pl.BlockSpec((d, LR_PAD), lambda j: (0, 0))),
        compiler_params=_cparams(("arbitrary",)),
        name="wprep",
    )(w_in_t, w_in_t)


def _add_pos(x, rt_ref, ct_ref):
    tm, d = x.shape
    x3 = x.reshape(tm // GRID_W, GRID_W, d)
    half = d // 2
    lo = x3[:, :, :half] + rt_ref[...]
    hi = x3[:, :, half:] + ct_ref[...][None]
    return jnp.concatenate([lo, hi], axis=-1).reshape(tm, d)


def _rms(x, g):
    return x * lax.rsqrt(jnp.mean(x * x, axis=-1, keepdims=True) + EPS) * g


def _inproj_kernel(*refs, add_pos, n_mod_rows):
    refs = list(refs)
    x_ref = refs.pop(0)
    rt_ref, ct_ref = (refs.pop(0), refs.pop(0)) if add_pos else (None, None)
    sh_ref, sc_ref, g_ref, w_ref, wlr_ref = refs[:5]
    refs = refs[5:]
    mod_in = [refs.pop(0) for _ in range(3)] if n_mod_rows else None
    p_ref, lr_ref = refs.pop(0), refs.pop(0)
    mod_out = refs.pop(0) if n_mod_rows else None
    (h_scr,) = refs

    @pl.when(pl.program_id(1) == 0)
    def _():
        x = x_ref[...]
        if add_pos:
            x = _add_pos(x, rt_ref, ct_ref)
        h = _rms(x, g_ref[...]) * (1.0 + sc_ref[...]) + sh_ref[...]
        hb = h.astype(BF16)
        h_scr[...] = hb
        lr_ref[...] = jnp.dot(hb, wlr_ref[...], preferred_element_type=F32)

    if n_mod_rows:
        _mod_kernel(*mod_in, mod_out, n_used=n_mod_rows)
    p_ref[...] = jnp.dot(h_scr[...], w_ref[...], preferred_element_type=F32).astype(BF16)


def _inproj(x2, mod3, mod_row_of_tile, norm_g, w_main, w_lr, pos_tabs, tm, width, mod_job=None):
    ntok, d = x2.shape
    tn = 2048
    add_pos = pos_tabs is not None
    in_specs = [pl.BlockSpec((tm, d), lambda i, j: (i, 0))]
    args = [x2]
    if add_pos:
        rt3, ct, tiles_per_seq = pos_tabs
        rpt = tm // GRID_W
        in_specs += [pl.BlockSpec((rpt, 1, d // 2), lambda i, j: (i % tiles_per_seq, 0, 0)),
                     pl.BlockSpec((GRID_W, d // 2), lambda i, j: (0, 0))]
        args += [rt3, ct]
    in_specs += [pl.BlockSpec((None, 1, d), lambda i, j: (mod_row_of_tile(i), 0, 0)),
                 pl.BlockSpec((None, 1, d), lambda i, j: (mod_row_of_tile(i), 0, 1)),
                 pl.BlockSpec((1, d), lambda i, j: (0, 0)),
                 pl.BlockSpec((d, tn), lambda i, j: (0, j)),
                 pl.BlockSpec((d, LR_PAD), lambda i, j: (0, 0))]
    args += [mod3, mod3, norm_g, w_main, w_lr]
    out_shape = [jax.ShapeDtypeStruct((ntok, width), BF16), jax.ShapeDtypeStruct((ntok, LR_PAD), F32)]
    out_specs = [pl.BlockSpec((tm, tn), lambda i, j: (i, j)), pl.BlockSpec((tm, LR_PAD), lambda i, j: (i, 0))]
    ncol = width // tn
    n_mod_rows = 0
    if mod_job is not None:
        cpad_t, n_mod_rows, w_mod, b_mod, col0 = mod_job
        n_steps = (ntok // tm) * ncol
        n_late = w_mod.shape[1] - col0
        cps = n_late // n_steps
        assert n_late % n_steps == 0 and cps % LANES == 0 and col0 % cps == 0
        flat = lambda i, j: i * ncol + j
        in_specs += [pl.BlockSpec(cpad_t.shape, lambda i, j: (0, 0), pipeline_mode=pl.Buffered(1)),
                     pl.BlockSpec((w_mod.shape[0], cps), lambda i, j: (0, col0 // cps + flat(i, j))),
                     pl.BlockSpec((1, cps), lambda i, j: (0, col0 // cps + flat(i, j)))]
        args += [cpad_t, w_mod, b_mod]
        out_shape.append(jax.ShapeDtypeStruct((cpad_t.shape[1], n_late), F32))
        out_specs.append(pl.BlockSpec((cpad_t.shape[1], cps), lambda i, j: (0, flat(i, j))))
    return pl.pallas_call(
        functools.partial(_inproj_kernel, add_pos=add_pos, n_mod_rows=n_mod_rows),
        out_shape=tuple(out_shape),
        grid=(ntok // tm, ncol),
        in_specs=in_specs,
        out_specs=tuple(out_specs),
        scratch_shapes=[pltpu.VMEM((tm, d), BF16)],
        compiler_params=_cparams(("arbitrary", "arbitrary"), vmem=BIG_TILE_VMEM_LIMIT),
        name="inproj_pos" if add_pos else "inproj_ctx",
    )(*args)


def _log_sigmoid(z):
    return jnp.minimum(z, 0.0) - jnp.log1p(jnp.exp(-jnp.abs(z)))


def _gla_chunks(dirs, c):
    n_sub = dirs[0][0].shape[0] // c
    row = lax.broadcasted_iota(jnp.int32, (c, c), 0)
    col = lax.broadcasted_iota(jnp.int32, (c, c), 1)

    chains = {}
    log2_qscale = math.log2(DK ** -0.5)

    def front(step):
        for di, (qk_ref, v_ref, lr_ref, wlr_ref, blr_ref, s_scr, o_ref, backward) in enumerate(dirs):
            keep = (row <= col) if backward else (row >= col)
            tri = jnp.where(keep, 1.0, 0.0).astype(BF16)
            last = 0 if backward else c - 1
            mid = c // 2 if backward else c // 2 - 1
            sub = n_sub - 1 - step if backward else step
            rows = slice(sub * c, (sub + 1) * c)
            z = jnp.dot(lr_ref[rows, :].astype(BF16), wlr_ref[...], preferred_element_type=F32) + blr_ref[...]
            la = _log_sigmoid(z) * (math.log2(math.e) / TAU)
            la_hi = la.astype(BF16)
            la_lo = (la - la_hi.astype(F32)).astype(BF16)
            cum2 = jnp.dot(tri, jnp.concatenate([la_hi, la_lo], axis=1), preferred_element_type=F32)
            cum_all = cum2[:, :KEY_W] + cum2[:, KEY_W:]
            for h in range(HEADS):
                cum = cum_all[:, h * DK:(h + 1) * DK]
                tot = cum[last:last + 1, :]
                ref_pt = cum[mid:mid + 1, :]
                q = qk_ref[rows, h * DK:(h + 1) * DK].astype(F32)
                k = qk_ref[rows, KEY_W + h * DK:KEY_W + (h + 1) * DK].astype(F32)
                chains[(step, di, h)] = dict(
                    keep=keep, rows=rows,
                    v=v_ref[rows, h * DV:(h + 1) * DV],
                    q_mid=(q * jnp.exp2(cum - (ref_pt - log2_qscale))).astype(BF16),
                    k_mid=(k * jnp.exp2(ref_pt - cum)).astype(BF16),
                    q_dec=(q * jnp.exp2(cum + log2_qscale)).astype(BF16),
                    k_end_t=(k * jnp.exp2(tot - cum)).T.astype(BF16),
                    dec_col=jnp.broadcast_to(jnp.exp2(tot), (DK, DK)).T)

    def middle(step):
        for di in range(len(dirs)):
            for h in range(HEADS):
                ch = chains[(step, di, h)]
                s = lax.dot_general(ch["q_mid"], ch["k_mid"], (((1,), (1,)), ((), ())),
                                    preferred_element_type=F32)
                ch["scores"] = jnp.where(ch["keep"], s, 0.0).astype(BF16)
                ch["kv"] = jnp.dot(ch["k_end_t"], ch["v"], preferred_element_type=F32)

    state = {(di, h): d[5][h] for di, d in enumerate(dirs) for h in range(HEADS)}

    def tail(step):
        for di, d in enumerate(dirs):
            o_ref = d[6]
            for h in range(HEADS):
                ch = chains.pop((step, di, h))
                s_prev = state[(di, h)]
                lhs = jnp.concatenate([ch["scores"], ch["q_dec"]], axis=1)
                rhs = jnp.concatenate([ch["v"], s_prev.astype(BF16)], axis=0)
                o = jnp.dot(lhs, rhs, preferred_element_type=F32)
                if o_ref is not None:
                    o_ref[ch["rows"], h * DV:(h + 1) * DV] = o.astype(o_ref.dtype)
                dec = jnp.concatenate([ch["dec_col"]] * (DV // DK), axis=1)
                state[(di, h)] = s_prev * dec + ch["kv"]

    front(0)
    for step in range(n_sub):
        if step + 1 < n_sub:
            front(step + 1)
        middle(step)
        tail(step)
    for (di, h), s in state.items():
        dirs[di][5][h] = s


def _rider_specs(weights, n_steps, flat_step):
    specs, shapes = [], []
    for w in weights:
        rows, cols = w.shape
        assert rows % (16 * n_steps) == 0
        specs.append(pl.BlockSpec((rows // n_steps, cols), lambda *idx: (flat_step(*idx), 0)))
        shapes.append(jax.ShapeDtypeStruct(w.shape, BF16))
    return specs, list(specs), shapes


def _cast_riders(in_refs, out_refs):
    for src, dst in zip(in_refs, out_refs):
        dst[...] = src[...].astype(dst.dtype)


def _gla_kernel(*refs, emit_o, n_riders):
    (qkf, vf, lrf, qkb, vb, lrb, wf, bf, wb, bb, s0f, s0b) = refs[:12]
    rider_in, refs = refs[12:12 + n_riders], refs[12 + n_riders:]
    if emit_o:
        of, ob, sf_out, sb_out = refs[:4]
        rider_out = refs[4:4 + n_riders]
    else:
        sf_out, sb_out = refs[:2]
        rider_out = refs[2:2 + n_riders]
        of = ob = None
    s_scr = refs[-1]
    i = pl.program_id(1)

    @pl.when(i == 0)
    def _():
        s_scr[0] = s0f[...]
        s_scr[1] = s0b[...]

    _cast_riders(rider_in, rider_out)

    _gla_chunks([(qkf, vf, lrf, wf, bf, s_scr.at[0], of, False),
                 (qkb, vb, lrb, wb, bb, s_scr.at[1], ob, True)], GLA_CHUNK)

    @pl.when(i == pl.num_programs(1) - 1)
    def _():
        sf_out[...] = s_scr[0]
        sb_out[...] = s_scr[1]


def _gla(p3, lr3, wlr_f, blr_f, wlr_b, blr_b, s0f, s0b, emit_o, riders=()):
    b, t, _ = p3.shape
    per_step = max(s for s in range(1, GLA_CHUNKS_PER_STEP + 1) if t % (GLA_CHUNK * s) == 0)
    c = GLA_CHUNK * per_step
    n = t // c
    fwd = lambda blk: (lambda bi, i: (bi, i, blk))
    bwd = lambda blk: (lambda bi, i: (bi, n - 1 - i, blk))

    def seq_specs(mk):
        return [pl.BlockSpec((None, c, 2 * KEY_W), mk(0)),
                pl.BlockSpec((None, c, VAL_W), mk(1)),
                pl.BlockSpec((None, c, LR_PAD), mk(0))]
    full2 = lambda shape: pl.BlockSpec(shape, lambda bi, i: (0, 0))
    st_spec = pl.BlockSpec((None, HEADS, DK, DV), lambda bi, i: (bi, 0, 0, 0))
    in_specs = (seq_specs(fwd) + seq_specs(bwd)
                + [full2(wlr_f.shape), full2(blr_f.shape), full2(wlr_b.shape), full2(blr_b.shape),
                   st_spec, st_spec])
    st_shape = jax.ShapeDtypeStruct((b, HEADS, DK, DV), F32)
    out_shape = [st_shape, st_shape]
    out_specs = [st_spec, st_spec]
    if emit_o:
        o_shape = jax.ShapeDtypeStruct((b, t, VAL_W), BF16)
        out_shape = [o_shape, o_shape] + out_shape
        out_specs = [pl.BlockSpec((None, c, VAL_W), fwd(0)), pl.BlockSpec((None, c, VAL_W), bwd(0))] + out_specs
    r_in, r_out, r_shapes = _rider_specs(riders, b * n, lambda bi, i: bi * n + i)
    return pl.pallas_call(
        functools.partial(_gla_kernel, emit_o=emit_o, n_riders=len(riders)),
        out_shape=tuple(out_shape + r_shapes),
        grid=(b, n),
        in_specs=in_specs + r_in,
        out_specs=tuple(out_specs + r_out),
        scratch_shapes=[pltpu.VMEM((2, HEADS, DK, DV), F32)],
        compiler_params=_cparams(("arbitrary", "arbitrary")),
        name="gla_seq" if emit_o else "gla_ctx",
    )(p3, p3, lr3, p3, p3, lr3, wlr_f, blr_f, wlr_b, blr_b, s0f, s0b, *riders)


def _cadd(a, b):
    return a[0] + b[0], a[1] + b[1]


def _csub(a, b):
    return a[0] - b[0], a[1] - b[1]


def _cmul_neg_i(a):
    return a[1], -a[0]


def _dft4(y):
    t0, t1 = _cadd(y[0], y[2]), _csub(y[0], y[2])
    t2, t3 = _cadd(y[1], y[3]), _cmul_neg_i(_csub(y[1], y[3]))
    return [_cadd(t0, t2), _cadd(t1, t3), _csub(t0, t2), _csub(t1, t3)]


def _dft8(z):
    r = math.sqrt(0.5)
    s = [_cadd(z[a], z[a + 4]) for a in range(4)]
    d = [_csub(z[a], z[a + 4]) for a in range(4)]
    d1 = ((d[1][0] + d[1][1]) * r, (d[1][1] - d[1][0]) * r)
    d2 = _cmul_neg_i(d[2])
    d3 = ((d[3][1] - d[3][0]) * r, (-d[3][1] - d[3][0]) * r)
    ev = _dft4(s)
    od = _dft4([d[0], d1, d2, d3])
    out = [None] * 8
    for j in range(4):
        out[2 * j] = ev[j]
        out[2 * j + 1] = od[j]
    return out


def _fft_kernel(*refs, n_slabs, n_riders):
    u_refs = refs[:n_slabs]
    f_ref, twc_ref, tws_ref = refs[n_slabs:n_slabs + 3]
    rider_in = refs[n_slabs + 3:n_slabs + 3 + n_riders]
    xr_ref, xi_ref = refs[n_slabs + 3 + n_riders:n_slabs + 5 + n_riders]
    rider_out = refs[n_slabs + 5 + n_riders:n_slabs + 5 + 2 * n_riders]
    wide_scr, ub_scr = refs[-2:]
    m = ub_scr.shape[1]

    @pl.when(pl.program_id(2) == 0)
    def _():
        for s in range(n_slabs):
            wide_scr[s] = u_refs[s][...].astype(F32)
        for a in range(RADIX):
            rows = [wide_scr[s, pl.ds(a, m, stride=RADIX), :].astype(BF16) for s in range(n_slabs)]
            ub_scr[a] = jnp.concatenate(rows, axis=1)

    _cast_riders(rider_in, rider_out)

    t = f_ref.shape[0] // 2
    f = f_ref[...]
    z = []
    for a in range(RADIX):
        za = jnp.dot(f, ub_scr[a], preferred_element_type=F32)
        zr, zi = za[:t], za[t:]
        cc, ss = twc_ref[a], tws_ref[a]
        z.append((zr * cc + zi * ss, zi * cc - zr * ss))
    x = _dft8(z)
    for k1 in range(RADIX):
        xr_ref[k1] = x[k1][0].astype(xr_ref.dtype)
        xi_ref[k1] = x[k1][1].astype(xi_ref.dtype)


def _fft(p3, col0, fmat, twc, tws, riders=()):
    b, seq, _ = p3.shape
    w = FWIDTH
    m = seq // RADIX
    nt, t2, _ = fmat.shape
    t = t2 // 2
    chb = 256
    n_slabs = chb // LANES
    slab0 = col0 // LANES
    out = jax.ShapeDtypeStruct((b, RADIX, m, w), BF16)
    o_spec = pl.BlockSpec((None, RADIX, t, chb), lambda bi, cj, kt: (bi, 0, kt, cj))
    slab = lambda s: pl.BlockSpec((None, seq, LANES), lambda bi, cj, kt: (bi, 0, slab0 + cj * n_slabs + s))
    ncj = w // chb
    n_steps = b * ncj * nt
    flat = lambda bi, cj, kt: (bi * ncj + cj) * nt + kt
    r_in, r_out, r_shapes = _rider_specs(riders, n_steps, flat)
    return pl.pallas_call(
        functools.partial(_fft_kernel, n_slabs=n_slabs, n_riders=len(riders)),
        out_shape=tuple([out, out] + r_shapes),
        grid=(b, ncj, nt),
        in_specs=[slab(s) for s in range(n_slabs)] + [
                  pl.BlockSpec((None, t2, m), lambda bi, cj, kt: (kt, 0, 0)),
                  pl.BlockSpec((RADIX, t, 1), lambda bi, cj, kt: (0, kt, 0)),
                  pl.BlockSpec((RADIX, t, 1), lambda bi, cj, kt: (0, kt, 0))] + r_in,
        out_specs=tuple([o_spec, o_spec] + r_out),
        scratch_shapes=[pltpu.VMEM((n_slabs, seq, LANES), F32), pltpu.VMEM((RADIX, m, chb), BF16)],
        compiler_params=_cparams(("arbitrary", "arbitrary", "arbitrary")),
        name="fft",
    )(*([p3] * n_slabs), fmat, twc, tws, *riders)


def _merge_kernel(of_ref, ob_ref, g_ref, gates_ref, xr_ref, xi_ref, x_ref, rt_ref, ct_ref, gt_ref,
                  gng_ref, wgo_ref, wfo_ref, wo_ref, csg_ref, o_ref):
    o = of_ref[...].astype(F32) + ob_ref[...].astype(F32)
    g = g_ref[...].astype(F32)
    gng = gng_ref[...]
    heads = []
    for h in range(HEADS):
        sl = slice(h * DV, (h + 1) * DV)
        heads.append((_rms(o[:, sl], gng) * _silu(g[:, sl])).astype(BF16))
    y_gla = jnp.dot(jnp.concatenate(heads, axis=1), wgo_ref[...], preferred_element_type=F32)

    csg = csg_ref[...]
    groups = []
    for gi in range(FGROUPS):
        sl = slice(gi * FGDIM, (gi + 1) * FGDIM)
        xg = jnp.concatenate([xr_ref[:, sl], xi_ref[:, sl]], axis=1)
        groups.append(jnp.dot(xg, csg, preferred_element_type=F32).astype(BF16))
    y_fft = jnp.dot(jnp.concatenate(groups, axis=1), wfo_ref[...], preferred_element_type=F32)

    gates = jax.nn.sigmoid(gates_ref[...].astype(F32))
    d = y_fft.shape[1]
    zmix = (gates[:, :d] * y_fft + gates[:, d:] * y_gla).astype(BF16)
    y = jnp.dot(zmix, wo_ref[...], preferred_element_type=F32)
    o_ref[...] = _add_pos(x_ref[...], rt_ref, ct_ref) + gt_ref[...] * y


def _const_spec(shape):
    zeros = (0,) * len(shape)
    return pl.BlockSpec(shape, lambda *idx: zeros, pipeline_mode=pl.Buffered(1))


def _merge(o_f, o_b, p2, xr, xi, x2, rt3, ct, mod3, gng, wgo, wfo, wo, csg, t):
    ntok, d = x2.shape
    tm = 256
    tiles_per_seq = t // tm
    rpt = tm // GRID_W
    row = lambda blk: (lambda i: (i, blk))
    in_specs = [pl.BlockSpec((tm, VAL_W), row(0)),
                pl.BlockSpec((tm, VAL_W), row(0)),
                pl.BlockSpec((tm, VAL_W), row(2)),
                pl.BlockSpec((tm, 2 * d), row(1)),
                pl.BlockSpec((tm, FWIDTH), row(0)),
                pl.BlockSpec((tm, FWIDTH), row(0)),
                pl.BlockSpec((tm, d), row(0)),
                pl.BlockSpec((rpt, 1, d // 2), lambda i: (i % tiles_per_seq, 0, 0)),
                _const_spec((GRID_W, d // 2)),
                pl.BlockSpec((None, 1, d), lambda i: (i // tiles_per_seq, 0, 2 - N_MOD_EARLY)),
                _const_spec(gng.shape), _const_spec(wgo.shape), _const_spec(wfo.shape),
                _const_spec(wo.shape), _const_spec(csg.shape)]
    return pl.pallas_call(
        _merge_kernel,
        out_shape=jax.ShapeDtypeStruct((ntok, d), F32),
        grid=(ntok // tm,),
        in_specs=in_specs,
        out_specs=pl.BlockSpec((tm, d), row(0)),
        compiler_params=_cparams(("arbitrary",)),
        name="merge",
    )(o_f, o_b, p2, p2, xr, xi, x2, rt3, ct, mod3, gng, wgo, wfo, wo, csg)


def _mlp_kernel(x_ref, xn_ref, sh_ref, sc_ref, shn_ref, scn_ref, gt_ref, g2_ref, w1_ref, w2_ref, fg_ref,
                o_ref, h_even, h_odd):
    i, j = pl.program_id(0), pl.program_id(1)
    nf = pl.num_programs(1)
    tm = x_ref.shape[0]
    slab = tm // nf
    slot = i % 2

    def normed(x, sh, sc):
        return (_rms(x, g2_ref[...]) * (1.0 + sc) + sh).astype(BF16)

    @pl.when((i == 0) & (j == 0))
    def _():
        h_even[...] = normed(x_ref[...], sh_ref[...], sc_ref[...])

    @pl.when(j == 0)
    def _():
        o_ref[...] = jnp.zeros_like(o_ref)

    def step(h_cur, h_next):
        r0 = pl.multiple_of(j * slab, slab)
        h_next[pl.ds(r0, slab), :] = normed(xn_ref[...], shn_ref[...], scn_ref[...])
        tf = w1_ref.shape[1]
        for c0 in range(0, tf, MLP_CHUNK):
            hid = jnp.dot(h_cur[...], w1_ref[:, c0:c0 + MLP_CHUNK], preferred_element_type=F32)
            hid = jnp.square(jnp.maximum(hid, 0.0)).astype(BF16)
            o_ref[...] += jnp.dot(hid, w2_ref[c0:c0 + MLP_CHUNK, :], preferred_element_type=F32)

    @pl.when(slot == 0)
    def _():
        step(h_even, h_odd)

    @pl.when(slot == 1)
    def _():
        step(h_odd, h_even)

    @pl.when(j == nf - 1)
    def _():
        xo = x_ref[...] + gt_ref[...] * o_ref[...]
        o_ref[...] = _rms(xo, fg_ref[...])


def _mlp(x2, mod3, g2, w1, w2, fg, t):
    ntok, d = x2.shape
    dff = w1.shape[1]
    tm, tf = 512, 2048
    nf = dff // tf
    slab = tm // nf
    assert slab % 8 == 0 and tf % MLP_CHUNK == 0
    nt = ntok // tm
    tiles_per_seq = t // tm
    nxt = lambda i: jnp.minimum(i + 1, nt - 1)
    modspec = lambda blk: pl.BlockSpec((None, 1, d), lambda i, j: (i // tiles_per_seq, 0, blk))
    modspec_next = lambda blk: pl.BlockSpec((None, 1, d), lambda i, j: (nxt(i) // tiles_per_seq, 0, blk))
    return pl.pallas_call(
        _mlp_kernel,
        out_shape=jax.ShapeDtypeStruct((ntok, d), F32),
        grid=(nt, dff // tf),
        in_specs=[pl.BlockSpec((tm, d), lambda i, j: (jnp.where(j >= 2, i, jnp.maximum(i - 1, 0)), 0)),
                  pl.BlockSpec((slab, d), lambda i, j: (nxt(i) * nf + j, 0)),
                  modspec(3 - N_MOD_EARLY), modspec(4 - N_MOD_EARLY), modspec_next(3 - N_MOD_EARLY),
                  modspec_next(4 - N_MOD_EARLY), modspec(5 - N_MOD_EARLY),
                  pl.BlockSpec((1, d), lambda i, j: (0, 0)),
                  pl.BlockSpec((d, tf), lambda i, j: (0, j)),
                  pl.BlockSpec((tf, d), lambda i, j: (j, 0)),
                  pl.BlockSpec((1, d), lambda i, j: (0, 0))],
        out_specs=pl.BlockSpec((tm, d), lambda i, j: (i, 0)),
        scratch_shapes=[pltpu.VMEM((tm, d), BF16), pltpu.VMEM((tm, d), BF16)],
        compiler_params=_cparams(("arbitrary", "arbitrary"), vmem=BIG_TILE_VMEM_LIMIT),
        name="mlp",
    )(x2, x2, mod3, mod3, mod3, mod3, mod3, g2, w1, w2, fg)


def _pos_tables(t, d):
    quarter = d // 4
    omega = 1.0 / (POS_TEMP ** (jnp.arange(quarter, dtype=F32) / quarter))
    er = jnp.arange(t // GRID_W, dtype=F32)[:, None] * omega[None, :]
    ec = jnp.arange(GRID_W, dtype=F32)[:, None] * omega[None, :]
    rt = jnp.concatenate([jnp.sin(er), jnp.cos(er)], axis=-1)
    ct = jnp.concatenate([jnp.sin(ec), jnp.cos(ec)], axis=-1)
    return rt[:, None, :], ct


def _dft_tables(t):
    m = t // RADIX
    tile = min(256, m)
    k = np.arange(m)
    ang = 2.0 * np.pi * ((k[:, None] * k[None, :]) % m) / m
    cos_t = np.cos(ang).reshape(m // tile, tile, m)
    sin_t = np.sin(ang).reshape(m // tile, tile, m)
    fmat = np.concatenate([cos_t, -sin_t], axis=1)
    a = np.arange(RADIX)
    tw = 2.0 * np.pi * (a[:, None] * k[None, :]) / t
    scale = 1.0 / math.sqrt(t * FGDIM)
    twc = (np.cos(tw) * scale)[:, :, None]
    tws = (np.sin(tw) * scale)[:, :, None]
    c = np.arange(FGDIM)
    cang = 2.0 * np.pi * ((c[:, None] * c[None, :]) % FGDIM) / FGDIM
    csg = np.concatenate([np.cos(cang), np.sin(cang)], axis=0)
    as_f32 = lambda a: jnp.asarray(a.astype(np.float32))
    return as_f32(fmat).astype(BF16), as_f32(twc), as_f32(tws), as_f32(csg).astype(BF16)


def _pad_lr_weight(w_lr, row0):
    out = jnp.zeros((LR_PAD, KEY_W), F32)
    return out.at[row0:row0 + RANK].set(w_lr).astype(BF16)


def kernel(x, c, ctx, c_ctx, w_mod, b_mod, norm1_g, norm2_g, w_in, w_lr_f, b_lr_f, w_lr_b, b_lr_b,
           gla_norm_g, w_fourier_out, w_gla_out, w_out, w_mlp_in, w_mlp_out, final_norm_g):
    b, t, d = x.shape
    tc = ctx.shape[1]
    depth = w_mod.shape[0]
    assert depth == 1 and d == D_MODEL and t % (RADIX * GRID_W) == 0 and tc % GLA_CHUNK == 0
    li = 0

    w_main, w_lr = _wprep(w_in[li].T)
    wlr_f, wlr_b = _pad_lr_weight(w_lr_f[li], 0), _pad_lr_weight(w_lr_b[li], RANK)
    blr_f, blr_b = b_lr_f[li][None, :], b_lr_b[li][None, :]

    rt3, ct = _pos_tables(t, d)
    fmat, twc, tws, csg = _dft_tables(t)

    rows = 8
    cpad_t = jnp.concatenate([c, c_ctx[None, :], jnp.zeros((rows - b - 1, d), F32)], axis=0).T
    n_early = N_MOD_EARLY * d
    mod3 = _mod(cpad_t, b + 1, w_mod[li], b_mod[li][None, :], n_early).reshape(rows, 1, n_early)

    tm_ctx = min(512, b * tc)
    qkv_w = 2 * KEY_W + VAL_W
    p_ctx, lr_ctx = _inproj(ctx.reshape(b * tc, d), mod3, lambda i: b, norm1_g[li][None, :],
                            w_main, w_lr, None, tm_ctx, qkv_w)
    s_zero = jnp.zeros((b, HEADS, DK, DV), F32)
    s_f, s_b = _gla(p_ctx.reshape(b, tc, qkv_w), lr_ctx.reshape(b, tc, LR_PAD),
                    wlr_f, blr_f, wlr_b, blr_b, s_zero, s_zero, emit_o=False)

    tm = min(1024, t)
    tiles = t // tm
    x2 = x.reshape(b * t, d)
    p, lr, mod_late = _inproj(x2, mod3, lambda i: i // tiles, norm1_g[li][None, :], w_main, w_lr,
                              (rt3, ct, tiles), tm, P_WIDTH,
                              mod_job=(cpad_t, b + 1, w_mod[li], b_mod[li][None, :], n_early))
    mod_late3 = mod_late.reshape(rows, 1, N_MOD * d - n_early)
    p3 = p.reshape(b, t, P_WIDTH)
    o_f, o_b, _, _, w1, wgo, wfo, wo = _gla(
        p3, lr.reshape(b, t, LR_PAD), wlr_f, blr_f, wlr_b, blr_b, s_f, s_b, emit_o=True,
        riders=(w_mlp_in[li], w_gla_out[li], w_fourier_out[li], w_out[li]))

    xr, xi, w2 = _fft(p3, QKVG_W, fmat, twc, tws, riders=(w_mlp_out[li],))

    x1 = _merge(o_f.reshape(b * t, VAL_W), o_b.reshape(b * t, VAL_W), p,
                xr.reshape(b * t, FWIDTH), xi.reshape(b * t, FWIDTH), x2, rt3, ct, mod_late3,
                gla_norm_g[li][None, :], wgo, wfo, wo, csg, t)

    out = _mlp(x1, mod_late3, norm2_g[li][None, :], w1, w2, final_norm_g[None, :], t)
    return out.reshape(b, t, d)
```

```python
import functools
import math

import jax
import jax.numpy as jnp
import numpy as np
from jax import lax
from jax.experimental import pallas as pl
from jax.experimental.pallas import tpu as pltpu

F32 = jnp.float32
BF16 = jnp.bfloat16

D_MODEL = 2048
GRID_W = 64
HEADS = 4
DK = 128
DV = 256
KEY_W = HEADS * DK
VAL_W = HEADS * DV
RANK = 16
TAU = 16.0
FGROUPS = 4
FGDIM = 256
FWIDTH = FGROUPS * FGDIM
D_FF = 4 * D_MODEL
N_MOD = 6
N_MOD_EARLY = 2
EPS = 1e-6
POS_TEMP = 10000.0

QKVG_W = 2 * KEY_W + 2 * VAL_W
P_WIDTH = QKVG_W + FWIDTH + 2 * D_MODEL
LR_PAD = 128
GLA_CHUNK = 128
GLA_CHUNKS_PER_STEP = 4
RADIX = 8
LANES = 128

V7X_VMEM_LIMIT = 56 * 1024 * 1024
BIG_TILE_VMEM_LIMIT = 63 * 1024 * 1024
MLP_CHUNK = 2048


def _cparams(sem, vmem=V7X_VMEM_LIMIT):
    return pltpu.CompilerParams(dimension_semantics=sem, vmem_limit_bytes=vmem)


def _silu(x):
    return x * jax.nn.sigmoid(x)


def _mod_kernel(ct_ref, w_ref, b_ref, o_ref, *, n_used):
    st = _silu(ct_ref[...])
    w = w_ref[...]
    bias = b_ref[...]
    rows = [jnp.sum(w * st[:, m:m + 1], axis=0, keepdims=True) + bias for m in range(n_used)]
    rows += [bias] * (o_ref.shape[0] - n_used)
    o_ref[...] = jnp.concatenate(rows, axis=0)


def _mod(cpad_t, n_used, w_mod, b_mod, n):
    d, rows = cpad_t.shape
    tn = 1024
    return pl.pallas_call(
        functools.partial(_mod_kernel, n_used=n_used),
        out_shape=jax.ShapeDtypeStruct((rows, n), F32),
        grid=(n // tn,),
        in_specs=[pl.BlockSpec((d, rows), lambda j: (0, 0)),
                  pl.BlockSpec((d, tn), lambda j: (0, j)),
                  pl.BlockSpec((1, tn), lambda j: (0, j))],
        out_specs=pl.BlockSpec((rows, tn), lambda j: (0, j)),
        compiler_params=_cparams(("arbitrary",)),
        name="mod",
    )(cpad_t, w_mod, b_mod)


def _wprep_kernel(wt_ref, lrt_ref, wm_ref, wlr_ref):
    wm_ref[...] = wt_ref[...].T.astype(BF16)

    @pl.when(pl.program_id(0) == 0)
    def _():
        lrt = lrt_ref[...]
        r = lax.broadcasted_iota(jnp.int32, lrt.shape, 0)
        wlr_ref[...] = jnp.where(r < 2 * RANK, lrt, 0.0).T.astype(BF16)


def _wprep(w_in_t):
    n, d = w_in_t.shape
    tc = 512
    n_before = QKVG_W // tc

    def src_row(j):
        return pl.multiple_of(jnp.where(j >= n_before, j * tc + 2 * RANK, j * tc), 2 * RANK)

    return pl.pallas_call(
        _wprep_kernel,
        out_shape=(jax.ShapeDtypeStruct((d, P_WIDTH), BF16), jax.ShapeDtypeStruct((d, LR_PAD), BF16)),
        grid=(P_WIDTH // tc,),
        in_specs=[pl.BlockSpec((pl.Element(tc), pl.Element(d)), lambda j: (src_row(j), 0)),
                  pl.BlockSpec((LR_PAD, d), lambda j: (QKVG_W // LR_PAD, 0))],
        out_specs=(pl.BlockSpec((d, tc), lambda j: (0, j)), pl.BlockSpec((d, LR_PAD), lambda j: (0, 0))),
        compiler_params=_cparams(("arbitrary",)),
        name="wprep",
    )(w_in_t, w_in_t)


def _add_pos(x, rt_ref, ct_ref):
    tm, d = x.shape
    x3 = x.reshape(tm // GRID_W, GRID_W, d)
    half = d // 2
    lo = x3[:, :, :half] + rt_ref[...]
    hi = x3[:, :, half:] + ct_ref[...][None]
    return jnp.concatenate([lo, hi], axis=-1).reshape(tm, d)


def _rms(x, g):
    return x * lax.rsqrt(jnp.mean(x * x, axis=-1, keepdims=True) + EPS) * g


def _inproj_kernel(*refs, add_pos, n_mod_rows):
    refs = list(refs)
    x_ref = refs.pop(0)
    rt_ref, ct_ref = (refs.pop(0), refs.pop(0)) if add_pos else (None, None)
    sh_ref, sc_ref, g_ref, w_ref, wlr_ref = refs[:5]
    refs = refs[5:]
    mod_in = [refs.pop(0) for _ in range(3)] if n_mod_rows else None
    p_ref, lr_ref = refs.pop(0), refs.pop(0)
    mod_out = refs.pop(0) if n_mod_rows else None
    (h_scr,) = refs

    @pl.when(pl.program_id(1) == 0)
    def _():
        x = x_ref[...]
        if add_pos:
            x = _add_pos(x, rt_ref, ct_ref)
        h = _rms(x, g_ref[...]) * (1.0 + sc_ref[...]) + sh_ref[...]
        hb = h.astype(BF16)
        h_scr[...] = hb
        lr_ref[...] = jnp.dot(hb, wlr_ref[...], preferred_element_type=F32)

    if n_mod_rows:
        _mod_kernel(*mod_in, mod_out, n_used=n_mod_rows)
    p_ref[...] = jnp.dot(h_scr[...], w_ref[...], preferred_element_type=F32).astype(BF16)


def _inproj(x2, mod3, mod_row_of_tile, norm_g, w_main, w_lr, pos_tabs, tm, width, mod_job=None):
    ntok, d = x2.shape
    tn = 2048
    add_pos = pos_tabs is not None
    in_specs = [pl.BlockSpec((tm, d), lambda i, j: (i, 0))]
    args = [x2]
    if add_pos:
        rt3, ct, tiles_per_seq = pos_tabs
        rpt = tm // GRID_W
        in_specs += [pl.BlockSpec((rpt, 1, d // 2), lambda i, j: (i % tiles_per_seq, 0, 0)),
                     pl.BlockSpec((GRID_W, d // 2), lambda i, j: (0, 0))]
        args += [rt3, ct]
    in_specs += [pl.BlockSpec((None, 1, d), lambda i, j: (mod_row_of_tile(i), 0, 0)),
                 pl.BlockSpec((None, 1, d), lambda i, j: (mod_row_of_tile(i), 0, 1)),
                 pl.BlockSpec((1, d), lambda i, j: (0, 0)),
                 pl.BlockSpec((d, tn), lambda i, j: (0, j)),
                 pl.BlockSpec((d, LR_PAD), lambda i, j: (0, 0))]
    args += [mod3, mod3, norm_g, w_main, w_lr]
    out_shape = [jax.ShapeDtypeStruct((ntok, width), BF16), jax.ShapeDtypeStruct((ntok, LR_PAD), F32)]
    out_specs = [pl.BlockSpec((tm, tn), lambda i, j: (i, j)), pl.BlockSpec((tm, LR_PAD), lambda i, j: (i, 0))]
    ncol = width // tn
    n_mod_rows = 0
    if mod_job is not None:
        cpad_t, n_mod_rows, w_mod, b_mod, col0 = mod_job
        n_steps = (ntok // tm) * ncol
        n_late = w_mod.shape[1] - col0
        cps = n_late // n_steps
        assert n_late % n_steps == 0 and cps % LANES == 0 and col0 % cps == 0
        flat = lambda i, j: i * ncol + j
        in_specs += [pl.BlockSpec(cpad_t.shape, lambda i, j: (0, 0), pipeline_mode=pl.Buffered(1)),
                     pl.BlockSpec((w_mod.shape[0], cps), lambda i, j: (0, col0 // cps + flat(i, j))),
                     pl.BlockSpec((1, cps), lambda i, j: (0, col0 // cps + flat(i, j)))]
        args += [cpad_t, w_mod, b_mod]
        out_shape.append(jax.ShapeDtypeStruct((cpad_t.shape[1], n_late), F32))
        out_specs.append(pl.BlockSpec((cpad_t.shape[1], cps), lambda i, j: (0, flat(i, j))))
    return pl.pallas_call(
        functools.partial(_inproj_kernel, add_pos=add_pos, n_mod_rows=n_mod_rows),
        out_shape=tuple(out_shape),
        grid=(ntok // tm, ncol),
        in_specs=in_specs,
        out_specs=tuple(out_specs),
        scratch_shapes=[pltpu.VMEM((tm, d), BF16)],
        compiler_params=_cparams(("arbitrary", "arbitrary"), vmem=BIG_TILE_VMEM_LIMIT),
        name="inproj_pos" if add_pos else "inproj_ctx",
    )(*args)


def _log_sigmoid(z):
    return jnp.minimum(z, 0.0) - jnp.log(1.0 + jnp.exp(-jnp.abs(z)))


def _gla_chunks(dirs, c):
    n_sub = dirs[0][0].shape[0] // c
    row = lax.broadcasted_iota(jnp.int32, (c, c), 0)
    col = lax.broadcasted_iota(jnp.int32, (c, c), 1)

    chains = {}
    log2_qscale = math.log2(DK ** -0.5)

    def front(step):
        for di, (qk_ref, v_ref, lr_ref, wlr_ref, blr_ref, s_scr, o_ref, backward) in enumerate(dirs):
            keep = (row <= col) if backward else (row >= col)
            tri = jnp.where(keep, 1.0, 0.0).astype(BF16)
            last = 0 if backward else c - 1
            mid = c // 2 if backward else c // 2 - 1
            sub = n_sub - 1 - step if backward else step
            rows = slice(sub * c, (sub + 1) * c)
            z = jnp.dot(lr_ref[rows, :].astype(BF16), wlr_ref[...], preferred_element_type=F32) + blr_ref[...]
            la = _log_sigmoid(z) * (math.log2(math.e) / TAU)
            la_hi = la.astype(BF16)
            la_lo = (la - la_hi.astype(F32)).astype(BF16)
            cum2 = jnp.dot(tri, jnp.concatenate([la_hi, la_lo], axis=1), preferred_element_type=F32)
            cum_all = cum2[:, :KEY_W] + cum2[:, KEY_W:]
            for h in range(HEADS):
                cum = cum_all[:, h * DK:(h + 1) * DK]
                tot = cum[last:last + 1, :]
                ref_pt = cum[mid:mid + 1, :]
                q = qk_ref[rows, h * DK:(h + 1) * DK].astype(F32)
                k = qk_ref[rows, KEY_W + h * DK:KEY_W + (h + 1) * DK].astype(F32)
                chains[(step, di, h)] = dict(
                    keep=keep, rows=rows,
                    v=v_ref[rows, h * DV:(h + 1) * DV],
                    q_mid=(q * jnp.exp2(cum - (ref_pt - log2_qscale))).astype(BF16),
                    k_mid=(k * jnp.exp2(ref_pt - cum)).astype(BF16),
                    q_dec=(q * jnp.exp2(cum + log2_qscale)).astype(BF16),
                    k_end_t=(k * jnp.exp2(tot - cum)).T.astype(BF16),
                    dec_col=jnp.broadcast_to(jnp.exp2(tot), (DK, DK)).T)

    def middle(step):
        for di in range(len(dirs)):
            for h in range(HEADS):
                ch = chains[(step, di, h)]
                s = lax.dot_general(ch["q_mid"], ch["k_mid"], (((1,), (1,)), ((), ())),
                                    preferred_element_type=F32)
                ch["scores"] = jnp.where(ch["keep"], s, 0.0).astype(BF16)
                ch["kv"] = jnp.dot(ch["k_end_t"], ch["v"], preferred_element_type=F32)

    state = {(di, h): d[5][h] for di, d in enumerate(dirs) for h in range(HEADS)}

    def tail(step):
        for di, d in enumerate(dirs):
            o_ref = d[6]
            for h in range(HEADS):
                ch = chains.pop((step, di, h))
                s_prev = state[(di, h)]
                lhs = jnp.concatenate([ch["scores"], ch["q_dec"]], axis=1)
                rhs = jnp.concatenate([ch["v"], s_prev.astype(BF16)], axis=0)
                o = jnp.dot(lhs, rhs, preferred_element_type=F32)
                if o_ref is not None:
                    o_ref[ch["rows"], h * DV:(h + 1) * DV] = o.astype(o_ref.dtype)
                dec = jnp.concatenate([ch["dec_col"]] * (DV // DK), axis=1)
                state[(di, h)] = s_prev * dec + ch["kv"]

    front(0)
    for step in range(n_sub):
        if step + 1 < n_sub:
            front(step + 1)
        middle(step)
        tail(step)
    for (di, h), s in state.items():
        dirs[di][5][h] = s


def _rider_specs(weights, n_steps, flat_step):
    specs, shapes = [], []
    for w in weights:
        rows, cols = w.shape
        assert rows % (16 * n_steps) == 0
        specs.append(pl.BlockSpec((rows // n_steps, cols), lambda *idx: (flat_step(*idx), 0)))
        shapes.append(jax.ShapeDtypeStruct(w.shape, BF16))
    return specs, list(specs), shapes


def _cast_riders(in_refs, out_refs):
    for src, dst in zip(in_refs, out_refs):
        dst[...] = src[...].astype(dst.dtype)


def _gla_kernel(*refs, emit_o, n_riders):
    (qkf, vf, lrf, qkb, vb, lrb, wf, bf, wb, bb, s0f, s0b) = refs[:12]
    rider_in, refs = refs[12:12 + n_riders], refs[12 + n_riders:]
    if emit_o:
        of, ob, sf_out, sb_out = refs[:4]
        rider_out = refs[4:4 + n_riders]
    else:
        sf_out, sb_out = refs[:2]
        rider_out = refs[2:2 + n_riders]
        of = ob = None
    s_scr = refs[-1]
    i = pl.program_id(1)

    @pl.when(i == 0)
    def _():
        s_scr[0] = s0f[...]
        s_scr[1] = s0b[...]

    _cast_riders(rider_in, rider_out)

    _gla_chunks([(qkf, vf, lrf, wf, bf, s_scr.at[0], of, False),
                 (qkb, vb, lrb, wb, bb, s_scr.at[1], ob, True)], GLA_CHUNK)

    @pl.when(i == pl.num_programs(1) - 1)
    def _():
        sf_out[...] = s_scr[0]
        sb_out[...] = s_scr[1]


def _gla(p3, lr3, wlr_f, blr_f, wlr_b, blr_b, s0f, s0b, emit_o, riders=()):
    b, t, _ = p3.shape
    per_step = max(s for s in range(1, GLA_CHUNKS_PER_STEP + 1) if t % (GLA_CHUNK * s) == 0)
    c = GLA_CHUNK * per_step
    n = t // c
    fwd = lambda blk: (lambda bi, i: (bi, i, blk))
    bwd = lambda blk: (lambda bi, i: (bi, n - 1 - i, blk))

    def seq_specs(mk):
        return [pl.BlockSpec((None, c, 2 * KEY_W), mk(0)),
                pl.BlockSpec((None, c, VAL_W), mk(1)),
                pl.BlockSpec((None, c, LR_PAD), mk(0))]
    full2 = lambda shape: pl.BlockSpec(shape, lambda bi, i: (0, 0))
    st_spec = pl.BlockSpec((None, HEADS, DK, DV), lambda bi, i: (bi, 0, 0, 0))
    in_specs = (seq_specs(fwd) + seq_specs(bwd)
                + [full2(wlr_f.shape), full2(blr_f.shape), full2(wlr_b.shape), full2(blr_b.shape),
                   st_spec, st_spec])
    st_shape = jax.ShapeDtypeStruct((b, HEADS, DK, DV), F32)
    out_shape = [st_shape, st_shape]
    out_specs = [st_spec, st_spec]
    if emit_o:
        o_shape = jax.ShapeDtypeStruct((b, t, VAL_W), BF16)
        out_shape = [o_shape, o_shape] + out_shape
        out_specs = [pl.BlockSpec((None, c, VAL_W), fwd(0)), pl.BlockSpec((None, c, VAL_W), bwd(0))] + out_specs
    r_in, r_out, r_shapes = _rider_specs(riders, b * n, lambda bi, i: bi * n + i)
    return pl.pallas_call(
        functools.partial(_gla_kernel, emit_o=emit_o, n_riders=len(riders)),
        out_shape=tuple(out_shape + r_shapes),
        grid=(b, n),
        in_specs=in_specs + r_in,
        out_specs=tuple(out_specs + r_out),
        scratch_shapes=[pltpu.VMEM((2, HEADS, DK, DV), F32)],
        compiler_params=_cparams(("arbitrary", "arbitrary")),
        name="gla_seq" if emit_o else "gla_ctx",
    )(p3, p3, lr3, p3, p3, lr3, wlr_f, blr_f, wlr_b, blr_b, s0f, s0b, *riders)


def _cadd(a, b):
    return a[0] + b[0], a[1] + b[1]


def _csub(a, b):
    return a[0] - b[0], a[1] - b[1]


def _cmul_neg_i(a):
    return a[1], -a[0]


def _dft4(y):
    t0, t1 = _cadd(y[0], y[2]), _csub(y[0], y[2])
    t2, t3 = _cadd(y[1], y[3]), _cmul_neg_i(_csub(y[1], y[3]))
    return [_cadd(t0, t2), _cadd(t1, t3), _csub(t0, t2), _csub(t1, t3)]


def _dft8(z):
    r = math.sqrt(0.5)
    s = [_cadd(z[a], z[a + 4]) for a in range(4)]
    d = [_csub(z[a], z[a + 4]) for a in range(4)]
    d1 = ((d[1][0] + d[1][1]) * r, (d[1][1] - d[1][0]) * r)
    d2 = _cmul_neg_i(d[2])
    d3 = ((d[3][1] - d[3][0]) * r, (-d[3][1] - d[3][0]) * r)
    ev = _dft4(s)
    od = _dft4([d[0], d1, d2, d3])
    out = [None] * 8
    for j in range(4):
        out[2 * j] = ev[j]
        out[2 * j + 1] = od[j]
    return out


def _fft_kernel(*refs, n_slabs, n_riders):
    u_refs = refs[:n_slabs]
    f_ref, twc_ref, tws_ref = refs[n_slabs:n_slabs + 3]
    rider_in = refs[n_slabs + 3:n_slabs + 3 + n_riders]
    xr_ref, xi_ref = refs[n_slabs + 3 + n_riders:n_slabs + 5 + n_riders]
    rider_out = refs[n_slabs + 5 + n_riders:n_slabs + 5 + 2 * n_riders]
    wide_scr, ub_scr = refs[-2:]
    m = ub_scr.shape[1]

    @pl.when(pl.program_id(2) == 0)
    def _():
        for s in range(n_slabs):
            wide_scr[s] = u_refs[s][...].astype(F32)
        for a in range(RADIX):
            rows = [wide_scr[s, pl.ds(a, m, stride=RADIX), :].astype(BF16) for s in range(n_slabs)]
            ub_scr[a] = jnp.concatenate(rows, axis=1)

    _cast_riders(rider_in, rider_out)

    t = f_ref.shape[0] // 2
    f = f_ref[...]
    z = []
    for a in range(RADIX):
        za = jnp.dot(f, ub_scr[a], preferred_element_type=F32)
        zr, zi = za[:t], za[t:]
        cc, ss = twc_ref[a], tws_ref[a]
        z.append((zr * cc + zi * ss, zi * cc - zr * ss))
    x = _dft8(z)
    for k1 in range(RADIX):
        xr_ref[k1] = x[k1][0].astype(xr_ref.dtype)
        xi_ref[k1] = x[k1][1].astype(xi_ref.dtype)


def _fft(p3, col0, fmat, twc, tws, riders=()):
    b, seq, _ = p3.shape
    w = FWIDTH
    m = seq // RADIX
    nt, t2, _ = fmat.shape
    t = t2 // 2
    chb = 256
    n_slabs = chb // LANES
    slab0 = col0 // LANES
    out = jax.ShapeDtypeStruct((b, RADIX, m, w), BF16)
    o_spec = pl.BlockSpec((None, RADIX, t, chb), lambda bi, cj, kt: (bi, 0, kt, cj))
    slab = lambda s: pl.BlockSpec((None, seq, LANES), lambda bi, cj, kt: (bi, 0, slab0 + cj * n_slabs + s))
    ncj = w // chb
    n_steps = b * ncj * nt
    flat = lambda bi, cj, kt: (bi * ncj + cj) * nt + kt
    r_in, r_out, r_shapes = _rider_specs(riders, n_steps, flat)
    return pl.pallas_call(
        functools.partial(_fft_kernel, n_slabs=n_slabs, n_riders=len(riders)),
        out_shape=tuple([out, out] + r_shapes),
        grid=(b, ncj, nt),
        in_specs=[slab(s) for s in range(n_slabs)] + [
                  pl.BlockSpec((None, t2, m), lambda bi, cj, kt: (kt, 0, 0)),
                  pl.BlockSpec((RADIX, t, 1), lambda bi, cj, kt: (0, kt, 0)),
                  pl.BlockSpec((RADIX, t, 1), lambda bi, cj, kt: (0, kt, 0))] + r_in,
        out_specs=tuple([o_spec, o_spec] + r_out),
        scratch_shapes=[pltpu.VMEM((n_slabs, seq, LANES), F32), pltpu.VMEM((RADIX, m, chb), BF16)],
        compiler_params=_cparams(("arbitrary", "arbitrary", "arbitrary")),
        name="fft",
    )(*([p3] * n_slabs), fmat, twc, tws, *riders)


def _merge_kernel(of_ref, ob_ref, g_ref, gates_ref, xr_ref, xi_ref, x_ref, rt_ref, ct_ref, gt_ref,
                  gng_ref, wgo_ref, wfo_ref, wo_ref, csg_ref, o_ref):
    o = of_ref[...].astype(F32) + ob_ref[...].astype(F32)
    g = g_ref[...].astype(F32)
    gng = gng_ref[...]
    heads = []
    for h in range(HEADS):
        sl = slice(h * DV, (h + 1) * DV)
        heads.append((_rms(o[:, sl], gng) * _silu(g[:, sl])).astype(BF16))
    y_gla = jnp.dot(jnp.concatenate(heads, axis=1), wgo_ref[...], preferred_element_type=F32)

    csg = csg_ref[...]
    groups = []
    for gi in range(FGROUPS):
        sl = slice(gi * FGDIM, (gi + 1) * FGDIM)
        xg = jnp.concatenate([xr_ref[:, sl], xi_ref[:, sl]], axis=1)
        groups.append(jnp.dot(xg, csg, preferred_element_type=F32).astype(BF16))
    y_fft = jnp.dot(jnp.concatenate(groups, axis=1), wfo_ref[...], preferred_element_type=F32)

    gates = jax.nn.sigmoid(gates_ref[...].astype(F32))
    d = y_fft.shape[1]
    zmix = (gates[:, :d] * y_fft + gates[:, d:] * y_gla).astype(BF16)
    y = jnp.dot(zmix, wo_ref[...], preferred_element_type=F32)
    o_ref[...] = _add_pos(x_ref[...], rt_ref, ct_ref) + gt_ref[...] * y


def _const_spec(shape):
    zeros = (0,) * len(shape)
    return pl.BlockSpec(shape, lambda *idx: zeros, pipeline_mode=pl.Buffered(1))


def _merge(o_f, o_b, p2, xr, xi, x2, rt3, ct, mod3, gng, wgo, wfo, wo, csg, t):
    ntok, d = x2.shape
    tm = 256
    tiles_per_seq = t // tm
    rpt = tm // GRID_W
    row = lambda blk: (lambda i: (i, blk))
    in_specs = [pl.BlockSpec((tm, VAL_W), row(0)),
                pl.BlockSpec((tm, VAL_W), row(0)),
                pl.BlockSpec((tm, VAL_W), row(2)),
                pl.BlockSpec((tm, 2 * d), row(1)),
                pl.BlockSpec((tm, FWIDTH), row(0)),
                pl.BlockSpec((tm, FWIDTH), row(0)),
                pl.BlockSpec((tm, d), row(0)),
                pl.BlockSpec((rpt, 1, d // 2), lambda i: (i % tiles_per_seq, 0, 0)),
                _const_spec((GRID_W, d // 2)),
                pl.BlockSpec((None, 1, d), lambda i: (i // tiles_per_seq, 0, 2 - N_MOD_EARLY)),
                _const_spec(gng.shape), _const_spec(wgo.shape), _const_spec(wfo.shape),
                _const_spec(wo.shape), _const_spec(csg.shape)]
    return pl.pallas_call(
        _merge_kernel,
        out_shape=jax.ShapeDtypeStruct((ntok, d), F32),
        grid=(ntok // tm,),
        in_specs=in_specs,
        out_specs=pl.BlockSpec((tm, d), row(0)),
        compiler_params=_cparams(("arbitrary",)),
        name="merge",
    )(o_f, o_b, p2, p2, xr, xi, x2, rt3, ct, mod3, gng, wgo, wfo, wo, csg)


def _mlp_kernel(x_ref, xn_ref, sh_ref, sc_ref, shn_ref, scn_ref, gt_ref, g2_ref, w1_ref, w2_ref, fg_ref,
                o_ref, h_even, h_odd):
    i, j = pl.program_id(0), pl.program_id(1)
    nf = pl.num_programs(1)
    tm = x_ref.shape[0]
    slab = tm // nf
    slot = i % 2

    def normed(x, sh, sc):
        return (_rms(x, g2_ref[...]) * (1.0 + sc) + sh).astype(BF16)

    @pl.when((i == 0) & (j == 0))
    def _():
        h_even[...] = normed(x_ref[...], sh_ref[...], sc_ref[...])

    @pl.when(j == 0)
    def _():
        o_ref[...] = jnp.zeros_like(o_ref)

    def step(h_cur, h_next):
        r0 = pl.multiple_of(j * slab, slab)
        h_next[pl.ds(r0, slab), :] = normed(xn_ref[...], shn_ref[...], scn_ref[...])
        tf = w1_ref.shape[1]
        for c0 in range(0, tf, MLP_CHUNK):
            hid = jnp.dot(h_cur[...], w1_ref[:, c0:c0 + MLP_CHUNK], preferred_element_type=F32)
            hid = jnp.square(jnp.maximum(hid, 0.0)).astype(BF16)
            o_ref[...] += jnp.dot(hid, w2_ref[c0:c0 + MLP_CHUNK, :], preferred_element_type=F32)

    @pl.when(slot == 0)
    def _():
        step(h_even, h_odd)

    @pl.when(slot == 1)
    def _():
        step(h_odd, h_even)

    @pl.when(j == nf - 1)
    def _():
        xo = x_ref[...] + gt_ref[...] * o_ref[...]
        o_ref[...] = _rms(xo, fg_ref[...])


def _mlp(x2, mod3, g2, w1, w2, fg, t):
    ntok, d = x2.shape
    dff = w1.shape[1]
    tm, tf = 512, 2048
    nf = dff // tf
    slab = tm // nf
    assert slab % 8 == 0 and tf % MLP_CHUNK == 0
    nt = ntok // tm
    tiles_per_seq = t // tm
    nxt = lambda i: jnp.minimum(i + 1, nt - 1)
    modspec = lambda blk: pl.BlockSpec((None, 1, d), lambda i, j: (i // tiles_per_seq, 0, blk))
    modspec_next = lambda blk: pl.BlockSpec((None, 1, d), lambda i, j: (nxt(i) // tiles_per_seq, 0, blk))
    return pl.pallas_call(
        _mlp_kernel,
        out_shape=jax.ShapeDtypeStruct((ntok, d), F32),
        grid=(nt, dff // tf),
        in_specs=[pl.BlockSpec((tm, d), lambda i, j: (jnp.where(j >= 2, i, jnp.maximum(i - 1, 0)), 0)),
                  pl.BlockSpec((slab, d), lambda i, j: (nxt(i) * nf + j, 0)),
                  modspec(3 - N_MOD_EARLY), modspec(4 - N_MOD_EARLY), modspec_next(3 - N_MOD_EARLY),
                  modspec_next(4 - N_MOD_EARLY), modspec(5 - N_MOD_EARLY),
                  pl.BlockSpec((1, d), lambda i, j: (0, 0)),
                  pl.BlockSpec((d, tf), lambda i, j: (0, j)),
                  pl.BlockSpec((tf, d), lambda i, j: (j, 0)),
                  pl.BlockSpec((1, d), lambda i, j: (0, 0))],
        out_specs=pl.BlockSpec((tm, d), lambda i, j: (i, 0)),
        scratch_shapes=[pltpu.VMEM((tm, d), BF16), pltpu.VMEM((tm, d), BF16)],
        compiler_params=_cparams(("arbitrary", "arbitrary"), vmem=BIG_TILE_VMEM_LIMIT),
        name="mlp",
    )(x2, x2, mod3, mod3, mod3, mod3, mod3, g2, w1, w2, fg)


def _pos_tables(t, d):
    quarter = d // 4
    omega = 1.0 / (POS_TEMP ** (jnp.arange(quarter, dtype=F32) / quarter))
    er = jnp.arange(t // GRID_W, dtype=F32)[:, None] * omega[None, :]
    ec = jnp.arange(GRID_W, dtype=F32)[:, None] * omega[None, :]
    rt = jnp.concatenate([jnp.sin(er), jnp.cos(er)], axis=-1)
    ct = jnp.concatenate([jnp.sin(ec), jnp.cos(ec)], axis=-1)
    return rt[:, None, :], ct


def _dft_tables(t):
    m = t // RADIX
    tile = min(256, m)
    k = np.arange(m)
    ang = 2.0 * np.pi * ((k[:, None] * k[None, :]) % m) / m
    cos_t = np.cos(ang).reshape(m // tile, tile, m)
    sin_t = np.sin(ang).reshape(m // tile, tile, m)
    fmat = np.concatenate([cos_t, -sin_t], axis=1)
    a = np.arange(RADIX)
    tw = 2.0 * np.pi * (a[:, None] * k[None, :]) / t
    scale = 1.0 / math.sqrt(t * FGDIM)
    twc = (np.cos(tw) * scale)[:, :, None]
    tws = (np.sin(tw) * scale)[:, :, None]
    c = np.arange(FGDIM)
    cang = 2.0 * np.pi * ((c[:, None] * c[None, :]) % FGDIM) / FGDIM
    csg = np.concatenate([np.cos(cang), np.sin(cang)], axis=0)
    as_f32 = lambda a: jnp.asarray(a.astype(np.float32))
    return as_f32(fmat).astype(BF16), as_f32(twc), as_f32(tws), as_f32(csg).astype(BF16)


def _pad_lr_weight(w_lr, row0):
    out = jnp.zeros((LR_PAD, KEY_W), F32)
    return out.at[row0:row0 + RANK].set(w_lr).astype(BF16)


def kernel(x, c, ctx, c_ctx, w_mod, b_mod, norm1_g, norm2_g, w_in, w_lr_f, b_lr_f, w_lr_b, b_lr_b,
           gla_norm_g, w_fourier_out, w_gla_out, w_out, w_mlp_in, w_mlp_out, final_norm_g):
    b, t, d = x.shape
    tc = ctx.shape[1]
    depth = w_mod.shape[0]
    assert depth == 1 and d == D_MODEL and t % (RADIX * GRID_W) == 0 and tc % GLA_CHUNK == 0
    li = 0

    w_main, w_lr = _wprep(w_in[li].T)
    wlr_f, wlr_b = _pad_lr_weight(w_lr_f[li], 0), _pad_lr_weight(w_lr_b[li], RANK)
    blr_f, blr_b = b_lr_f[li][None, :], b_lr_b[li][None, :]

    rt3, ct = _pos_tables(t, d)
    fmat, twc, tws, csg = _dft_tables(t)

    rows = 8
    cpad_t = jnp.concatenate([c, c_ctx[None, :], jnp.zeros((rows - b - 1, d), F32)], axis=0).T
    n_early = N_MOD_EARLY * d
    mod3 = _mod(cpad_t, b + 1, w_mod[li], b_mod[li][None, :], n_early).reshape(rows, 1, n_early)

    tm_ctx = min(512, b * tc)
    qkv_w = 2 * KEY_W + VAL_W
    p_ctx, lr_ctx = _inproj(ctx.reshape(b * tc, d), mod3, lambda i: b, norm1_g[li][None, :],
                            w_main, w_lr, None, tm_ctx, qkv_w)
    s_zero = jnp.zeros((b, HEADS, DK, DV), F32)
    s_f, s_b = _gla(p_ctx.reshape(b, tc, qkv_w), lr_ctx.reshape(b, tc, LR_PAD),
                    wlr_f, blr_f, wlr_b, blr_b, s_zero, s_zero, emit_o=False)

    tm = min(1024, t)
    tiles = t // tm
    x2 = x.reshape(b * t, d)
    p, lr, mod_late = _inproj(x2, mod3, lambda i: i // tiles, norm1_g[li][None, :], w_main, w_lr,
                              (rt3, ct, tiles), tm, P_WIDTH,
                              mod_job=(cpad_t, b + 1, w_mod[li], b_mod[li][None, :], n_early))
    mod_late3 = mod_late.reshape(rows, 1, N_MOD * d - n_early)
    p3 = p.reshape(b, t, P_WIDTH)
    o_f, o_b, _, _, w1, wgo, wfo, wo = _gla(
        p3, lr.reshape(b, t, LR_PAD), wlr_f, blr_f, wlr_b, blr_b, s_f, s_b, emit_o=True,
        riders=(w_mlp_in[li], w_gla_out[li], w_fourier_out[li], w_out[li]))

    xr, xi, w2 = _fft(p3, QKVG_W, fmat, twc, tws, riders=(w_mlp_out[li],))

    x1 = _merge(o_f.reshape(b * t, VAL_W), o_b.reshape(b * t, VAL_W), p,
                xr.reshape(b * t, FWIDTH), xi.reshape(b * t, FWIDTH), x2, rt3, ct, mod_late3,
                gla_norm_g[li][None, :], wgo, wfo, wo, csg, t)

    out = _mlp(x1, mod_late3, norm2_g[li][None, :], w1, w2, final_norm_g[None, :], t)
    return out.reshape(b, t, d)
```

```python
import functools
import math

import jax
import jax.numpy as jnp
import numpy as np
from jax import lax
from jax.experimental import pallas as pl
from jax.experimental.pallas import tpu as pltpu

F32 = jnp.float32
BF16 = jnp.bfloat16

D_MODEL = 2048
GRID_W = 64
HEADS = 4
DK = 128
DV = 256
KEY_W = HEADS * DK
VAL_W = HEADS * DV
RANK = 16
TAU = 16.0
FGROUPS = 4
FGDIM = 256
FWIDTH = FGROUPS * FGDIM
D_FF = 4 * D_MODEL
N_MOD = 6
N_MOD_EARLY = 2
EPS = 1e-6
POS_TEMP = 10000.0

QKVG_W = 2 * KEY_W + 2 * VAL_W
P_WIDTH = QKVG_W + FWIDTH + 2 * D_MODEL
LR_PAD = 128
GLA_CHUNK = 128
GLA_CHUNKS_PER_STEP = 4
RADIX = 8
LANES = 128

V7X_VMEM_LIMIT = 56 * 1024 * 1024
BIG_TILE_VMEM_LIMIT = 63 * 1024 * 1024
MLP_CHUNK = 2048


def _cparams(sem, vmem=V7X_VMEM_LIMIT):
    return pltpu.CompilerParams(dimension_semantics=sem, vmem_limit_bytes=vmem)


def _silu(x):
    return x * jax.nn.sigmoid(x)


def _mod_kernel(ct_ref, w_ref, b_ref, o_ref, *, n_used):
    st = _silu(ct_ref[...])
    w = w_ref[...]
    bias = b_ref[...]
    rows = [jnp.sum(w * st[:, m:m + 1], axis=0, keepdims=True) + bias for m in range(n_used)]
    rows += [bias] * (o_ref.shape[0] - n_used)
    o_ref[...] = jnp.concatenate(rows, axis=0)


def _mod(cpad_t, n_used, w_mod, b_mod, n):
    d, rows = cpad_t.shape
    tn = 1024
    return pl.pallas_call(
        functools.partial(_mod_kernel, n_used=n_used),
        out_shape=jax.ShapeDtypeStruct((rows, n), F32),
        grid=(n // tn,),
        in_specs=[pl.BlockSpec((d, rows), lambda j: (0, 0)),
                  pl.BlockSpec((d, tn), lambda j: (0, j)),
                  pl.BlockSpec((1, tn), lambda j: (0, j))],
        out_specs=pl.BlockSpec((rows, tn), lambda j: (0, j)),
        compiler_params=_cparams(("arbitrary",)),
        name="mod",
    )(cpad_t, w_mod, b_mod)


def _wprep_kernel(wt_ref, lrt_ref, wm_ref, wlr_ref):
    wm_ref[...] = wt_ref[...].T.astype(BF16)

    @pl.when(pl.program_id(0) == 0)
    def _():
        lrt = lrt_ref[...]
        r = lax.broadcasted_iota(jnp.int32, lrt.shape, 0)
        wlr_ref[...] = jnp.where(r < 2 * RANK, lrt, 0.0).T.astype(BF16)


def _wprep(w_in_t):
    n, d = w_in_t.shape
    tc = 512
    n_before = QKVG_W // tc

    def src_row(j):
        return pl.multiple_of(jnp.where(j >= n_before, j * tc + 2 * RANK, j * tc), 2 * RANK)

    return pl.pallas_call(
        _wprep_kernel,
        out_shape=(jax.ShapeDtypeStruct((d, P_WIDTH), BF16), jax.ShapeDtypeStruct((d, LR_PAD), BF16)),
        grid=(P_WIDTH // tc,),
        in_specs=[pl.BlockSpec((pl.Element(tc), pl.Element(d)), lambda j: (src_row(j), 0)),
                  pl.BlockSpec((LR_PAD, d), lambda j: (QKVG_W // LR_PAD, 0))],
        out_specs=(pl.BlockSpec((d, tc), lambda j: (0, j)), pl.BlockSpec((d, LR_PAD), lambda j: (0, 0))),
        compiler_params=_cparams(("arbitrary",)),
        name="wprep",
    )(w_in_t, w_in_t)


def _add_pos(x, rt_ref, ct_ref):
    tm, d = x.shape
    x3 = x.reshape(tm // GRID_W, GRID_W, d)
    half = d // 2
    lo = x3[:, :, :half] + rt_ref[...]
    hi = x3[:, :, half:] + ct_ref[...][None]
    return jnp.concatenate([lo, hi], axis=-1).reshape(tm, d)


def _rms(x, g):
    return x * lax.rsqrt(jnp.mean(x * x, axis=-1, keepdims=True) + EPS) * g


def _inproj_kernel(*refs, add_pos, n_mod_rows):
    refs = list(refs)
    x_ref = refs.pop(0)
    rt_ref, ct_ref = (refs.pop(0), refs.pop(0)) if add_pos else (None, None)
    sh_ref, sc_ref, g_ref, w_ref, wlr_ref = refs[:5]
    refs = refs[5:]
    mod_in = [refs.pop(0) for _ in range(3)] if n_mod_rows else None
    p_ref, lr_ref = refs.pop(0), refs.pop(0)
    mod_out = refs.pop(0) if n_mod_rows else None
    (h_scr,) = refs

    @pl.when(pl.program_id(1) == 0)
    def _():
        x = x_ref[...]
        if add_pos:
            x = _add_pos(x, rt_ref, ct_ref)
        h = _rms(x, g_ref[...]) * (1.0 + sc_ref[...]) + sh_ref[...]
        hb = h.astype(BF16)
        h_scr[...] = hb
        lr_ref[...] = jnp.dot(hb, wlr_ref[...], preferred_element_type=F32)

    if n_mod_rows:
        _mod_kernel(*mod_in, mod_out, n_used=n_mod_rows)
    p_ref[...] = jnp.dot(h_scr[...], w_ref[...], preferred_element_type=F32).astype(BF16)


def _inproj(x2, mod3, mod_row_of_tile, norm_g, w_main, w_lr, pos_tabs, tm, width, mod_job=None):
    ntok, d = x2.shape
    tn = 2048
    add_pos = pos_tabs is not None
    in_specs = [pl.BlockSpec((tm, d), lambda i, j: (i, 0))]
    args = [x2]
    if add_pos:
        rt3, ct, tiles_per_seq = pos_tabs
        rpt = tm // GRID_W
        in_specs += [pl.BlockSpec((rpt, 1, d // 2), lambda i, j: (i % tiles_per_seq, 0, 0)),
                     pl.BlockSpec((GRID_W, d // 2), lambda i, j: (0, 0))]
        args += [rt3, ct]
    in_specs += [pl.BlockSpec((None, 1, d), lambda i, j: (mod_row_of_tile(i), 0, 0)),
                 pl.BlockSpec((None, 1, d), lambda i, j: (mod_row_of_tile(i), 0, 1)),
                 pl.BlockSpec((1, d), lambda i, j: (0, 0)),
                 pl.BlockSpec((d, tn), lambda i, j: (0, j)),
                 pl.BlockSpec((d, LR_PAD), lambda i, j: (0, 0))]
    args += [mod3, mod3, norm_g, w_main, w_lr]
    out_shape = [jax.ShapeDtypeStruct((ntok, width), BF16), jax.ShapeDtypeStruct((ntok, LR_PAD), F32)]
    out_specs = [pl.BlockSpec((tm, tn), lambda i, j: (i, j)), pl.BlockSpec((tm, LR_PAD), lambda i, j: (i, 0))]
    ncol = width // tn
    n_mod_rows = 0
    if mod_job is not None:
        cpad_t, n_mod_rows, w_mod, b_mod, col0 = mod_job
        n_steps = (ntok // tm) * ncol
        n_late = w_mod.shape[1] - col0
        cps = n_late // n_steps
        assert n_late % n_steps == 0 and cps % LANES == 0 and col0 % cps == 0
        flat = lambda i, j: i * ncol + j
        in_specs += [pl.BlockSpec(cpad_t.shape, lambda i, j: (0, 0), pipeline_mode=pl.Buffered(1)),
                     pl.BlockSpec((w_mod.shape[0], cps), lambda i, j: (0, col0 // cps + flat(i, j))),
                     pl.BlockSpec((1, cps), lambda i, j: (0, col0 // cps + flat(i, j)))]
        args += [cpad_t, w_mod, b_mod]
        out_shape.append(jax.ShapeDtypeStruct((cpad_t.shape[1], n_late), F32))
        out_specs.append(pl.BlockSpec((cpad_t.shape[1], cps), lambda i, j: (0, flat(i, j))))
    return pl.pallas_call(
        functools.partial(_inproj_kernel, add_pos=add_pos, n_mod_rows=n_mod_rows),
        out_shape=tuple(out_shape),
        grid=(ntok // tm, ncol),
        in_specs=in_specs,
        out_specs=tuple(out_specs),
        scratch_shapes=[pltpu.VMEM((tm, d), BF16)],
        compiler_params=_cparams(("arbitrary", "arbitrary"), vmem=BIG_TILE_VMEM_LIMIT),
        name="inproj_pos" if add_pos else "inproj_ctx",
    )(*args)


def _log_sigmoid(z):
    return jnp.minimum(z, 0.0) - jnp.log(1.0 + jnp.exp(-jnp.abs(z)))


def _gla_chunks(dirs, c):
    n_sub = dirs[0][0].shape[0] // c
    row = lax.broadcasted_iota(jnp.int32, (c, c), 0)
    col = lax.broadcasted_iota(jnp.int32, (c, c), 1)

    chains = {}
    log2_qscale = math.log2(DK ** -0.5)

    def front(step):
        for di, (qk_ref, v_ref, lr_ref, wlr_ref, blr_ref, s_scr, o_ref, backward) in enumerate(dirs):
            keep = (row <= col) if backward else (row >= col)
            tri = jnp.where(keep, 1.0, 0.0).astype(BF16)
            last = 0 if backward else c - 1
            mid = c // 2 if backward else c // 2 - 1
            sub = n_sub - 1 - step if backward else step
            rows = slice(sub * c, (sub + 1) * c)
            z = jnp.dot(lr_ref[rows, :].astype(BF16), wlr_ref[...], preferred_element_type=F32) + blr_ref[...]
            la = _log_sigmoid(z) * (math.log2(math.e) / TAU)
            la_hi = la.astype(BF16)
            la_lo = (la - la_hi.astype(F32)).astype(BF16)
            cum2 = jnp.dot(tri, jnp.concatenate([la_hi, la_lo], axis=1), preferred_element_type=F32)
            cum_all = cum2[:, :KEY_W] + cum2[:, KEY_W:]
            for h in range(HEADS):
                cum = cum_all[:, h * DK:(h + 1) * DK]
                tot = cum[last:last + 1, :]
                ref_pt = cum[mid:mid + 1, :]
                q = qk_ref[rows, h * DK:(h + 1) * DK].astype(F32)
                k = qk_ref[rows, KEY_W + h * DK:KEY_W + (h + 1) * DK].astype(F32)
                chains[(step, di, h)] = dict(
                    keep=keep, rows=rows,
                    v=v_ref[rows, h * DV:(h + 1) * DV],
                    q_mid=(q * jnp.exp2(cum - (ref_pt - log2_qscale))).astype(BF16),
                    k_mid=(k * jnp.exp2(ref_pt - cum)).astype(BF16),
                    q_dec=(q * jnp.exp2(cum + log2_qscale)).astype(BF16),
                    k_end_t=(k * jnp.exp2(tot - cum)).T.astype(BF16),
                    dec_col=jnp.broadcast_to(jnp.exp2(tot), (DK, DK)).T)

    def middle(step):
        for di in range(len(dirs)):
            for h in range(HEADS):
                ch = chains[(step, di, h)]
                s = lax.dot_general(ch["q_mid"], ch["k_mid"], (((1,), (1,)), ((), ())),
                                    preferred_element_type=F32)
                ch["scores"] = jnp.where(ch["keep"], s, 0.0).astype(BF16)
                ch["kv"] = jnp.dot(ch["k_end_t"], ch["v"], preferred_element_type=F32)

    state = {(di, h): d[5][h] for di, d in enumerate(dirs) for h in range(HEADS)}

    def tail(step):
        for di, d in enumerate(dirs):
            o_ref = d[6]
            for h in range(HEADS):
                ch = chains.pop((step, di, h))
                s_prev = state[(di, h)]
                lhs = jnp.concatenate([ch["scores"], ch["q_dec"]], axis=1)
                rhs = jnp.concatenate([ch["v"], s_prev.astype(BF16)], axis=0)
                o = jnp.dot(lhs, rhs, preferred_element_type=F32)
                if o_ref is not None:
                    o_ref[ch["rows"], h * DV:(h + 1) * DV] = o.astype(o_ref.dtype)
                dec = jnp.concatenate([ch["dec_col"]] * (DV // DK), axis=1)
                state[(di, h)] = s_prev * dec + ch["kv"]

    front(0)
    for step in range(n_sub):
        if step + 1 < n_sub:
            front(step + 1)
        middle(step)
        tail(step)
    for (di, h), s in state.items():
        dirs[di][5][h] = s


def _rider_specs(weights, n_steps, flat_step):
    specs, shapes = [], []
    for w in weights:
        rows, cols = w.shape
        assert rows % (16 * n_steps) == 0
        specs.append(pl.BlockSpec((rows // n_steps, cols), lambda *idx: (flat_step(*idx), 0)))
        shapes.append(jax.ShapeDtypeStruct(w.shape, BF16))
    return specs, list(specs), shapes


def _cast_riders(in_refs, out_refs):
    for src, dst in zip(in_refs, out_refs):
        dst[...] = src[...].astype(dst.dtype)


def _gla_kernel(*refs, emit_o, n_riders):
    (qkf, vf, lrf, qkb, vb, lrb, wf, bf, wb, bb, s0f, s0b) = refs[:12]
    rider_in, refs = refs[12:12 + n_riders], refs[12 + n_riders:]
    if emit_o:
        of, ob, sf_out, sb_out = refs[:4]
        rider_out = refs[4:4 + n_riders]
    else:
        sf_out, sb_out = refs[:2]
        rider_out = refs[2:2 + n_riders]
        of = ob = None
    s_scr = refs[-1]
    i = pl.program_id(1)

    @pl.when(i == 0)
    def _():
        s_scr[0] = s0f[...]
        s_scr[1] = s0b[...]

    _cast_riders(rider_in, rider_out)

    _gla_chunks([(qkf, vf, lrf, wf, bf, s_scr.at[0], of, False),
                 (qkb, vb, lrb, wb, bb, s_scr.at[1], ob, True)], GLA_CHUNK)

    @pl.when(i == pl.num_programs(1) - 1)
    def _():
        sf_out[...] = s_scr[0]
        sb_out[...] = s_scr[1]


def _gla(p3, lr3, wlr_f, blr_f, wlr_b, blr_b, s0f, s0b, emit_o, riders=()):
    b, t, _ = p3.shape
    per_step = max(s for s in range(1, GLA_CHUNKS_PER_STEP + 1) if t % (GLA_CHUNK * s) == 0)
    c = GLA_CHUNK * per_step
    n = t // c
    fwd = lambda blk: (lambda bi, i: (bi, i, blk))
    bwd = lambda blk: (lambda bi, i: (bi, n - 1 - i, blk))

    def seq_specs(mk):
        return [pl.BlockSpec((None, c, 2 * KEY_W), mk(0)),
                pl.BlockSpec((None, c, VAL_W), mk(1)),
                pl.BlockSpec((None, c, LR_PAD), mk(0))]
    full2 = lambda shape: pl.BlockSpec(shape, lambda bi, i: (0, 0))
    st_spec = pl.BlockSpec((None, HEADS, DK, DV), lambda bi, i: (bi, 0, 0, 0))
    in_specs = (seq_specs(fwd) + seq_specs(bwd)
                + [full2(wlr_f.shape), full2(blr_f.shape), full2(wlr_b.shape), full2(blr_b.shape),
                   st_spec, st_spec])
    st_shape = jax.ShapeDtypeStruct((b, HEADS, DK, DV), F32)
    out_shape = [st_shape, st_shape]
    out_specs = [st_spec, st_spec]
    if emit_o:
        o_shape = jax.ShapeDtypeStruct((b, t, VAL_W), BF16)
        out_shape = [o_shape, o_shape] + out_shape
        out_specs = [pl.BlockSpec((None, c, VAL_W), fwd(0)), pl.BlockSpec((None, c, VAL_W), bwd(0))] + out_specs
    r_in, r_out, r_shapes = _rider_specs(riders, b * n, lambda bi, i: bi * n + i)
    return pl.pallas_call(
        functools.partial(_gla_kernel, emit_o=emit_o, n_riders=len(riders)),
        out_shape=tuple(out_shape + r_shapes),
        grid=(b, n),
        in_specs=in_specs + r_in,
        out_specs=tuple(out_specs + r_out),
        scratch_shapes=[pltpu.VMEM((2, HEADS, DK, DV), F32)],
        compiler_params=_cparams(("arbitrary", "arbitrary")),
        name="gla_seq" if emit_o else "gla_ctx",
    )(p3, p3, lr3, p3, p3, lr3, wlr_f, blr_f, wlr_b, blr_b, s0f, s0b, *riders)


def _cadd(a, b):
    return a[0] + b[0], a[1] + b[1]


def _csub(a, b):
    return a[0] - b[0], a[1] - b[1]


def _cmul_neg_i(a):
    return a[1], -a[0]


def _dft4(y):
    t0, t1 = _cadd(y[0], y[2]), _csub(y[0], y[2])
    t2, t3 = _cadd(y[1], y[3]), _cmul_neg_i(_csub(y[1], y[3]))
    return [_cadd(t0, t2), _cadd(t1, t3), _csub(t0, t2), _csub(t1, t3)]


def _dft8(z):
    r = math.sqrt(0.5)
    s = [_cadd(z[a], z[a + 4]) for a in range(4)]
    d = [_csub(z[a], z[a + 4]) for a in range(4)]
    d1 = ((d[1][0] + d[1][1]) * r, (d[1][1] - d[1][0]) * r)
    d2 = _cmul_neg_i(d[2])
    d3 = ((d[3][1] - d[3][0]) * r, (-d[3][1] - d[3][0]) * r)
    ev = _dft4(s)
    od = _dft4([d[0], d1, d2, d3])
    out = [None] * 8
    for j in range(4):
        out[2 * j] = ev[j]
        out[2 * j + 1] = od[j]
    return out


def _fft_kernel(*refs, n_slabs, n_riders):
    u_refs = refs[:n_slabs]
    f_ref, twc_ref, tws_ref = refs[n_slabs:n_slabs + 3]
    rider_in = refs[n_slabs + 3:n_slabs + 3 + n_riders]
    xr_ref, xi_ref = refs[n_slabs + 3 + n_riders:n_slabs + 5 + n_riders]
    rider_out = refs[n_slabs + 5 + n_riders:n_slabs + 5 + 2 * n_riders]
    wide_scr, ub_scr = refs[-2:]
    m = ub_scr.shape[1]

    @pl.when(pl.program_id(2) == 0)
    def _():
        for s in range(n_slabs):
            wide_scr[s] = u_refs[s][...].astype(F32)
        for a in range(RADIX):
            rows = [wide_scr[s, pl.ds(a, m, stride=RADIX), :].astype(BF16) for s in range(n_slabs)]
            ub_scr[a] = jnp.concatenate(rows, axis=1)

    _cast_riders(rider_in, rider_out)

    t = f_ref.shape[0] // 2
    f = f_ref[...]
    z = []
    for a in range(RADIX):
        za = jnp.dot(f, ub_scr[a], preferred_element_type=F32)
        zr, zi = za[:t], za[t:]
        cc, ss = twc_ref[a], tws_ref[a]
        z.append((zr * cc + zi * ss, zi * cc - zr * ss))
    x = _dft8(z)
    for k1 in range(RADIX):
        xr_ref[k1] = x[k1][0].astype(xr_ref.dtype)
        xi_ref[k1] = x[k1][1].astype(xi_ref.dtype)


def _fft(p3, col0, fmat, twc, tws, riders=()):
    b, seq, _ = p3.shape
    w = FWIDTH
    m = seq // RADIX
    nt, t2, _ = fmat.shape
    t = t2 // 2
    chb = 256
    n_slabs = chb // LANES
    slab0 = col0 // LANES
    out = jax.ShapeDtypeStruct((b, RADIX, m, w), BF16)
    o_spec = pl.BlockSpec((None, RADIX, t, chb), lambda bi, cj, kt: (bi, 0, kt, cj))
    slab = lambda s: pl.BlockSpec((None, seq, LANES), lambda bi, cj, kt: (bi, 0, slab0 + cj * n_slabs + s))
    ncj = w // chb
    n_steps = b * ncj * nt
    flat = lambda bi, cj, kt: (bi * ncj + cj) * nt + kt
    r_in, r_out, r_shapes = _rider_specs(riders, n_steps, flat)
    return pl.pallas_call(
        functools.partial(_fft_kernel, n_slabs=n_slabs, n_riders=len(riders)),
        out_shape=tuple([out, out] + r_shapes),
        grid=(b, ncj, nt),
        in_specs=[slab(s) for s in range(n_slabs)] + [
                  pl.BlockSpec((None, t2, m), lambda bi, cj, kt: (kt, 0, 0)),
                  pl.BlockSpec((RADIX, t, 1), lambda bi, cj, kt: (0, kt, 0)),
                  pl.BlockSpec((RADIX, t, 1), lambda bi, cj, kt: (0, kt, 0))] + r_in,
        out_specs=tuple([o_spec, o_spec] + r_out),
        scratch_shapes=[pltpu.VMEM((n_slabs, seq, LANES), F32), pltpu.VMEM((RADIX, m, chb), BF16)],
        compiler_params=_cparams(("arbitrary", "arbitrary", "arbitrary")),
        name="fft",
    )(*([p3] * n_slabs), fmat, twc, tws, *riders)


def _merge_kernel(of_ref, ob_ref, g_ref, gates_ref, xr_ref, xi_ref, x_ref, rt_ref, ct_ref, gt_ref,
                  gng_ref, wgo_ref, wfo_ref, wo_ref, csg_ref, o_ref):
    o = of_ref[...].astype(F32) + ob_ref[...].astype(F32)
    g = g_ref[...].astype(F32)
    gng = gng_ref[...]
    heads = []
    for h in range(HEADS):
        sl = slice(h * DV, (h + 1) * DV)
        heads.append((_rms(o[:, sl], gng) * _silu(g[:, sl])).astype(BF16))
    y_gla = jnp.dot(jnp.concatenate(heads, axis=1), wgo_ref[...], preferred_element_type=F32)

    csg = csg_ref[...]
    groups = []
    for gi in range(FGROUPS):
        sl = slice(gi * FGDIM, (gi + 1) * FGDIM)
        xg = jnp.concatenate([xr_ref[:, sl], xi_ref[:, sl]], axis=1)
        groups.append(jnp.dot(xg, csg, preferred_element_type=F32).astype(BF16))
    y_fft = jnp.dot(jnp.concatenate(groups, axis=1), wfo_ref[...], preferred_element_type=F32)

    gates = jax.nn.sigmoid(gates_ref[...].astype(F32))
    d = y_fft.shape[1]
    zmix = (gates[:, :d] * y_fft + gates[:, d:] * y_gla).astype(BF16)
    y = jnp.dot(zmix, wo_ref[...], preferred_element_type=F32)
    o_ref[...] = _add_pos(x_ref[...], rt_ref, ct_ref) + gt_ref[...] * y


def _const_spec(shape):
    zeros = (0,) * len(shape)
    return pl.BlockSpec(shape, lambda *idx: zeros, pipeline_mode=pl.Buffered(1))


def _merge(o_f, o_b, p2, xr, xi, x2, rt3, ct, mod3, gng, wgo, wfo, wo, csg, t):
    ntok, d = x2.shape
    tm = 256
    tiles_per_seq = t // tm
    rpt = tm // GRID_W
    row = lambda blk: (lambda i: (i, blk))
    in_specs = [pl.BlockSpec((tm, VAL_W), row(0)),
                pl.BlockSpec((tm, VAL_W), row(0)),
                pl.BlockSpec((tm, VAL_W), row(2)),
                pl.BlockSpec((tm, 2 * d), row(1)),
                pl.BlockSpec((tm, FWIDTH), row(0)),
                pl.BlockSpec((tm, FWIDTH), row(0)),
                pl.BlockSpec((tm, d), row(0)),
                pl.BlockSpec((rpt, 1, d // 2), lambda i: (i % tiles_per_seq, 0, 0)),
                _const_spec((GRID_W, d // 2)),
                pl.BlockSpec((None, 1, d), lambda i: (i // tiles_per_seq, 0, 2 - N_MOD_EARLY)),
                _const_spec(gng.shape), _const_spec(wgo.shape), _const_spec(wfo.shape),
                _const_spec(wo.shape), _const_spec(csg.shape)]
    return pl.pallas_call(
        _merge_kernel,
        out_shape=jax.ShapeDtypeStruct((ntok, d), F32),
        grid=(ntok // tm,),
        in_specs=in_specs,
        out_specs=pl.BlockSpec((tm, d), row(0)),
        compiler_params=_cparams(("arbitrary",)),
        name="merge",
    )(o_f, o_b, p2, p2, xr, xi, x2, rt3, ct, mod3, gng, wgo, wfo, wo, csg)


def _mlp_kernel(x_ref, xn_ref, sh_ref, sc_ref, shn_ref, scn_ref, gt_ref, g2_ref, w1_ref, w2_ref, fg_ref,
                o_ref, h_even, h_odd):
    i, j = pl.program_id(0), pl.program_id(1)
    nf = pl.num_programs(1)
    tm = x_ref.shape[0]
    slab = tm // nf
    slot = i % 2

    def normed(x, sh, sc):
        return (_rms(x, g2_ref[...]) * (1.0 + sc) + sh).astype(BF16)

    @pl.when((i == 0) & (j == 0))
    def _():
        h_even[...] = normed(x_ref[...], sh_ref[...], sc_ref[...])

    def step(h_cur, h_next, first, last):
        r0 = pl.multiple_of(j * slab, slab)
        h_next[pl.ds(r0, slab), :] = normed(xn_ref[...], shn_ref[...], scn_ref[...])
        tf = w1_ref.shape[1]
        acc = None if first else o_ref[...]
        for c0 in range(0, tf, MLP_CHUNK):
            hid = jnp.dot(h_cur[...], w1_ref[:, c0:c0 + MLP_CHUNK], preferred_element_type=F32)
            hid = jnp.square(jnp.maximum(hid, 0.0)).astype(BF16)
            part = jnp.dot(hid, w2_ref[c0:c0 + MLP_CHUNK, :], preferred_element_type=F32)
            acc = part if acc is None else acc + part
        if last:
            acc = _rms(x_ref[...] + gt_ref[...] * acc, fg_ref[...])
        o_ref[...] = acc

    for parity, (h_cur, h_next) in enumerate(((h_even, h_odd), (h_odd, h_even))):
        for first, last, cond in ((True, False, j == 0), (False, False, (j > 0) & (j < nf - 1)),
                                  (False, True, j == nf - 1)):
            @pl.when((slot == parity) & cond)
            def _(h_cur=h_cur, h_next=h_next, first=first, last=last):
                step(h_cur, h_next, first, last)


def _mlp(x2, mod3, g2, w1, w2, fg, t):
    ntok, d = x2.shape
    dff = w1.shape[1]
    tm, tf = 512, 2048
    nf = dff // tf
    slab = tm // nf
    assert slab % 8 == 0 and tf % MLP_CHUNK == 0
    nt = ntok // tm
    tiles_per_seq = t // tm
    nxt = lambda i: jnp.minimum(i + 1, nt - 1)
    modspec = lambda blk: pl.BlockSpec((None, 1, d), lambda i, j: (i // tiles_per_seq, 0, blk))
    modspec_next = lambda blk: pl.BlockSpec((None, 1, d), lambda i, j: (nxt(i) // tiles_per_seq, 0, blk))
    return pl.pallas_call(
        _mlp_kernel,
        out_shape=jax.ShapeDtypeStruct((ntok, d), F32),
        grid=(nt, dff // tf),
        in_specs=[pl.BlockSpec((tm, d), lambda i, j: (jnp.where(j >= 2, i, jnp.maximum(i - 1, 0)), 0)),
                  pl.BlockSpec((slab, d), lambda i, j: (nxt(i) * nf + j, 0)),
                  modspec(3 - N_MOD_EARLY), modspec(4 - N_MOD_EARLY), modspec_next(3 - N_MOD_EARLY),
                  modspec_next(4 - N_MOD_EARLY), modspec(5 - N_MOD_EARLY),
                  pl.BlockSpec((1, d), lambda i, j: (0, 0)),
                  pl.BlockSpec((d, tf), lambda i, j: (0, j)),
                  pl.BlockSpec((tf, d), lambda i, j: (j, 0)),
                  pl.BlockSpec((1, d), lambda i, j: (0, 0))],
        out_specs=pl.BlockSpec((tm, d), lambda i, j: (i, 0)),
        scratch_shapes=[pltpu.VMEM((tm, d), BF16), pltpu.VMEM((tm, d), BF16)],
        compiler_params=_cparams(("arbitrary", "arbitrary"), vmem=BIG_TILE_VMEM_LIMIT),
        name="mlp",
    )(x2, x2, mod3, mod3, mod3, mod3, mod3, g2, w1, w2, fg)


def _pos_tables(t, d):
    quarter = d // 4
    omega = 1.0 / (POS_TEMP ** (jnp.arange(quarter, dtype=F32) / quarter))
    er = jnp.arange(t // GRID_W, dtype=F32)[:, None] * omega[None, :]
    ec = jnp.arange(GRID_W, dtype=F32)[:, None] * omega[None, :]
    rt = jnp.concatenate([jnp.sin(er), jnp.cos(er)], axis=-1)
    ct = jnp.concatenate([jnp.sin(ec), jnp.cos(ec)], axis=-1)
    return rt[:, None, :], ct


def _dft_tables(t):
    m = t // RADIX
    tile = min(256, m)
    k = np.arange(m)
    ang = 2.0 * np.pi * ((k[:, None] * k[None, :]) % m) / m
    cos_t = np.cos(ang).reshape(m // tile, tile, m)
    sin_t = np.sin(ang).reshape(m // tile, tile, m)
    fmat = np.concatenate([cos_t, -sin_t], axis=1)
    a = np.arange(RADIX)
    tw = 2.0 * np.pi * (a[:, None] * k[None, :]) / t
    scale = 1.0 / math.sqrt(t * FGDIM)
    twc = (np.cos(tw) * scale)[:, :, None]
    tws = (np.sin(tw) * scale)[:, :, None]
    c = np.arange(FGDIM)
    cang = 2.0 * np.pi * ((c[:, None] * c[None, :]) % FGDIM) / FGDIM
    csg = np.concatenate([np.cos(cang), np.sin(cang)], axis=0)
    as_f32 = lambda a: jnp.asarray(a.astype(np.float32))
    return as_f32(fmat).astype(BF16), as_f32(twc), as_f32(tws), as_f32(csg).astype(BF16)


def _pad_lr_weight(w_lr, row0):
    out = jnp.zeros((LR_PAD, KEY_W), F32)
    return out.at[row0:row0 + RANK].set(w_lr).astype(BF16)


def kernel(x, c, ctx, c_ctx, w_mod, b_mod, norm1_g, norm2_g, w_in, w_lr_f, b_lr_f, w_lr_b, b_lr_b,
           gla_norm_g, w_fourier_out, w_gla_out, w_out, w_mlp_in, w_mlp_out, final_norm_g):
    b, t, d = x.shape
    tc = ctx.shape[1]
    depth = w_mod.shape[0]
    assert depth == 1 and d == D_MODEL and t % (RADIX * GRID_W) == 0 and tc % GLA_CHUNK == 0
    li = 0

    w_main, w_lr = _wprep(w_in[li].T)
    wlr_f, wlr_b = _pad_lr_weight(w_lr_f[li], 0), _pad_lr_weight(w_lr_b[li], RANK)
    blr_f, blr_b = b_lr_f[li][None, :], b_lr_b[li][None, :]

    rt3, ct = _pos_tables(t, d)
    fmat, twc, tws, csg = _dft_tables(t)

    rows = 8
    cpad_t = jnp.concatenate([c, c_ctx[None, :], jnp.zeros((rows - b - 1, d), F32)], axis=0).T
    n_early = N_MOD_EARLY * d
    mod3 = _mod(cpad_t, b + 1, w_mod[li], b_mod[li][None, :], n_early).reshape(rows, 1, n_early)

    tm_ctx = min(512, b * tc)
    qkv_w = 2 * KEY_W + VAL_W
    p_ctx, lr_ctx = _inproj(ctx.reshape(b * tc, d), mod3, lambda i: b, norm1_g[li][None, :],
                            w_main, w_lr, None, tm_ctx, qkv_w)
    s_zero = jnp.zeros((b, HEADS, DK, DV), F32)
    s_f, s_b = _gla(p_ctx.reshape(b, tc, qkv_w), lr_ctx.reshape(b, tc, LR_PAD),
                    wlr_f, blr_f, wlr_b, blr_b, s_zero, s_zero, emit_o=False)

    tm = min(1024, t)
    tiles = t // tm
    x2 = x.reshape(b * t, d)
    p, lr, mod_late = _inproj(x2, mod3, lambda i: i // tiles, norm1_g[li][None, :], w_main, w_lr,
                              (rt3, ct, tiles), tm, P_WIDTH,
                              mod_job=(cpad_t, b + 1, w_mod[li], b_mod[li][None, :], n_early))
    mod_late3 = mod_late.reshape(rows, 1, N_MOD * d - n_early)
    p3 = p.reshape(b, t, P_WIDTH)
    o_f, o_b, _, _, w1, wgo, wfo, wo = _gla(
        p3, lr.reshape(b, t, LR_PAD), wlr_f, blr_f, wlr_b, blr_b, s_f, s_b, emit_o=True,
        riders=(w_mlp_in[li], w_gla_out[li], w_fourier_out[li], w_out[li]))

    xr, xi, w2 = _fft(p3, QKVG_W, fmat, twc, tws, riders=(w_mlp_out[li],))

    x1 = _merge(o_f.reshape(b * t, VAL_W), o_b.reshape(b * t, VAL_W), p,
                xr.reshape(b * t, FWIDTH), xi.reshape(b * t, FWIDTH), x2, rt3, ct, mod_late3,
                gla_norm_g[li][None, :], wgo, wfo, wo, csg, t)

    out = _mlp(x1, mod_late3, norm2_g[li][None, :], w1, w2, final_norm_g[None, :], t)
    return out.reshape(b, t, d)
```

```python
import functools
import math

import jax
import jax.numpy as jnp
import numpy as np
from jax import lax
from jax.experimental import pallas as pl
from jax.experimental.pallas import tpu as pltpu

F32 = jnp.float32
BF16 = jnp.bfloat16

D_MODEL = 2048
GRID_W = 64
HEADS = 4
DK = 128
DV = 256
KEY_W = HEADS * DK
VAL_W = HEADS * DV
RANK = 16
TAU = 16.0
FGROUPS = 4
FGDIM = 256
FWIDTH = FGROUPS * FGDIM
D_FF = 4 * D_MODEL
N_MOD = 6
N_MOD_EARLY = 2
EPS = 1e-6
POS_TEMP = 10000.0

QKVG_W = 2 * KEY_W + 2 * VAL_W
P_WIDTH = QKVG_W + FWIDTH + 2 * D_MODEL
LR_PAD = 128
GLA_CHUNK = 128
GLA_CHUNKS_PER_STEP = 4
RADIX = 8
LANES = 128

V7X_VMEM_LIMIT = 56 * 1024 * 1024
BIG_TILE_VMEM_LIMIT = 63 * 1024 * 1024
MLP_CHUNK = 2048


def _cparams(sem, vmem=V7X_VMEM_LIMIT):
    return pltpu.CompilerParams(dimension_semantics=sem, vmem_limit_bytes=vmem)


def _silu(x):
    return x * jax.nn.sigmoid(x)


def _mod_kernel(ct_ref, w_ref, b_ref, o_ref, *, n_used):
    st = _silu(ct_ref[...])
    w = w_ref[...]
    bias = b_ref[...]
    rows = [jnp.sum(w * st[:, m:m + 1], axis=0, keepdims=True) + bias for m in range(n_used)]
    rows += [bias] * (o_ref.shape[0] - n_used)
    o_ref[...] = jnp.concatenate(rows, axis=0)


def _mod(cpad_t, n_used, w_mod, b_mod, n):
    d, rows = cpad_t.shape
    tn = 1024
    return pl.pallas_call(
        functools.partial(_mod_kernel, n_used=n_used),
        out_shape=jax.ShapeDtypeStruct((rows, n), F32),
        grid=(n // tn,),
        in_specs=[pl.BlockSpec((d, rows), lambda j: (0, 0)),
                  pl.BlockSpec((d, tn), lambda j: (0, j)),
                  pl.BlockSpec((1, tn), lambda j: (0, j))],
        out_specs=pl.BlockSpec((rows, tn), lambda j: (0, j)),
        compiler_params=_cparams(("arbitrary",)),
        name="mod",
    )(cpad_t, w_mod, b_mod)


def _wprep_kernel(wt_ref, lrt_ref, wm_ref, wlr_ref):
    wm_ref[...] = wt_ref[...].T.astype(BF16)

    @pl.when(pl.program_id(0) == 0)
    def _():
        lrt = lrt_ref[...]
        r = lax.broadcasted_iota(jnp.int32, lrt.shape, 0)
        wlr_ref[...] = jnp.where(r < 2 * RANK, lrt, 0.0).T.astype(BF16)


def _wprep(w_in_t):
    n, d = w_in_t.shape
    tc = 512
    n_before = QKVG_W // tc

    def src_row(j):
        return pl.multiple_of(jnp.where(j >= n_before, j * tc + 2 * RANK, j * tc), 2 * RANK)

    return pl.pallas_call(
        _wprep_kernel,
        out_shape=(jax.ShapeDtypeStruct((d, P_WIDTH), BF16), jax.ShapeDtypeStruct((d, LR_PAD), BF16)),
        grid=(P_WIDTH // tc,),
        in_specs=[pl.BlockSpec((pl.Element(tc), pl.Element(d)), lambda j: (src_row(j), 0)),
                  pl.BlockSpec((LR_PAD, d), lambda j: (QKVG_W // LR_PAD, 0))],
        out_specs=(pl.BlockSpec((d, tc), lambda j: (0, j)), pl.BlockSpec((d, LR_PAD), lambda j: (0, 0))),
        compiler_params=_cparams(("arbitrary",)),
        name="wprep",
    )(w_in_t, w_in_t)


def _add_pos(x, rt_ref, ct_ref):
    tm, d = x.shape
    x3 = x.reshape(tm // GRID_W, GRID_W, d)
    half = d // 2
    lo = x3[:, :, :half] + rt_ref[...]
    hi = x3[:, :, half:] + ct_ref[...][None]
    return jnp.concatenate([lo, hi], axis=-1).reshape(tm, d)


def _rms(x, g):
    return x * lax.rsqrt(jnp.mean(x * x, axis=-1, keepdims=True) + EPS) * g


def _inproj_kernel(*refs, add_pos, n_mod_rows):
    refs = list(refs)
    x_ref = refs.pop(0)
    rt_ref, ct_ref = (refs.pop(0), refs.pop(0)) if add_pos else (None, None)
    sh_ref, sc_ref, g_ref, w_ref, wlr_ref = refs[:5]
    refs = refs[5:]
    mod_in = [refs.pop(0) for _ in range(3)] if n_mod_rows else None
    p_ref, lr_ref = refs.pop(0), refs.pop(0)
    mod_out = refs.pop(0) if n_mod_rows else None
    (h_scr,) = refs

    @pl.when(pl.program_id(1) == 0)
    def _():
        x = x_ref[...]
        if add_pos:
            x = _add_pos(x, rt_ref, ct_ref)
        h = _rms(x, g_ref[...]) * (1.0 + sc_ref[...]) + sh_ref[...]
        hb = h.astype(BF16)
        h_scr[...] = hb
        lr_ref[...] = jnp.dot(hb, wlr_ref[...], preferred_element_type=F32)

    if n_mod_rows:
        _mod_kernel(*mod_in, mod_out, n_used=n_mod_rows)
    p_ref[...] = jnp.dot(h_scr[...], w_ref[...], preferred_element_type=F32).astype(BF16)


def _inproj(x2, mod3, mod_row_of_tile, norm_g, w_main, w_lr, pos_tabs, tm, width, mod_job=None):
    ntok, d = x2.shape
    tn = 2048
    add_pos = pos_tabs is not None
    in_specs = [pl.BlockSpec((tm, d), lambda i, j: (i, 0))]
    args = [x2]
    if add_pos:
        rt3, ct, tiles_per_seq = pos_tabs
        rpt = tm // GRID_W
        in_specs += [pl.BlockSpec((rpt, 1, d // 2), lambda i, j: (i % tiles_per_seq, 0, 0)),
                     pl.BlockSpec((GRID_W, d // 2), lambda i, j: (0, 0))]
        args += [rt3, ct]
    in_specs += [pl.BlockSpec((None, 1, d), lambda i, j: (mod_row_of_tile(i), 0, 0)),
                 pl.BlockSpec((None, 1, d), lambda i, j: (mod_row_of_tile(i), 0, 1)),
                 pl.BlockSpec((1, d), lambda i, j: (0, 0)),
                 pl.BlockSpec((d, tn), lambda i, j: (0, j)),
                 pl.BlockSpec((d, LR_PAD), lambda i, j: (0, 0))]
    args += [mod3, mod3, norm_g, w_main, w_lr]
    out_shape = [jax.ShapeDtypeStruct((ntok, width), BF16), jax.ShapeDtypeStruct((ntok, LR_PAD), F32)]
    out_specs = [pl.BlockSpec((tm, tn), lambda i, j: (i, j)), pl.BlockSpec((tm, LR_PAD), lambda i, j: (i, 0))]
    ncol = width // tn
    n_mod_rows = 0
    if mod_job is not None:
        cpad_t, n_mod_rows, w_mod, b_mod, col0 = mod_job
        n_steps = (ntok // tm) * ncol
        n_late = w_mod.shape[1] - col0
        cps = n_late // n_steps
        assert n_late % n_steps == 0 and cps % LANES == 0 and col0 % cps == 0
        flat = lambda i, j: i * ncol + j
        in_specs += [pl.BlockSpec(cpad_t.shape, lambda i, j: (0, 0), pipeline_mode=pl.Buffered(1)),
                     pl.BlockSpec((w_mod.shape[0], cps), lambda i, j: (0, col0 // cps + flat(i, j))),
                     pl.BlockSpec((1, cps), lambda i, j: (0, col0 // cps + flat(i, j)))]
        args += [cpad_t, w_mod, b_mod]
        out_shape.append(jax.ShapeDtypeStruct((cpad_t.shape[1], n_late), F32))
        out_specs.append(pl.BlockSpec((cpad_t.shape[1], cps), lambda i, j: (0, flat(i, j))))
    return pl.pallas_call(
        functools.partial(_inproj_kernel, add_pos=add_pos, n_mod_rows=n_mod_rows),
        out_shape=tuple(out_shape),
        grid=(ntok // tm, ncol),
        in_specs=in_specs,
        out_specs=tuple(out_specs),
        scratch_shapes=[pltpu.VMEM((tm, d), BF16)],
        compiler_params=_cparams(("arbitrary", "arbitrary"), vmem=BIG_TILE_VMEM_LIMIT),
        name="inproj_pos" if add_pos else "inproj_ctx",
    )(*args)


def _log_sigmoid(z):
    return jnp.minimum(z, 0.0) - jnp.log(1.0 + jnp.exp(-jnp.abs(z)))


def _gla_chunks(dirs, c):
    n_sub = dirs[0][0].shape[0] // c
    row = lax.broadcasted_iota(jnp.int32, (c, c), 0)
    col = lax.broadcasted_iota(jnp.int32, (c, c), 1)

    chains = {}
    log2_qscale = math.log2(DK ** -0.5)

    def front(step):
        for di, (qk_ref, v_ref, lr_ref, wlr_ref, blr_ref, s_scr, o_ref, backward) in enumerate(dirs):
            keep = (row <= col) if backward else (row >= col)
            tri = jnp.where(keep, 1.0, 0.0).astype(BF16)
            last = 0 if backward else c - 1
            mid = c // 2 if backward else c // 2 - 1
            sub = n_sub - 1 - step if backward else step
            rows = slice(sub * c, (sub + 1) * c)
            z = jnp.dot(lr_ref[rows, :].astype(BF16), wlr_ref[...], preferred_element_type=F32) + blr_ref[...]
            la = _log_sigmoid(z) * (math.log2(math.e) / TAU)
            la_hi = la.astype(BF16)
            la_lo = (la - la_hi.astype(F32)).astype(BF16)
            cum2 = jnp.dot(tri, jnp.concatenate([la_hi, la_lo], axis=1), preferred_element_type=F32)
            cum_all = cum2[:, :KEY_W] + cum2[:, KEY_W:]
            for h in range(HEADS):
                cum = cum_all[:, h * DK:(h + 1) * DK]
                tot = cum[last:last + 1, :]
                ref_pt = cum[mid:mid + 1, :]
                q = qk_ref[rows, h * DK:(h + 1) * DK].astype(F32)
                k = qk_ref[rows, KEY_W + h * DK:KEY_W + (h + 1) * DK].astype(F32)
                chains[(step, di, h)] = dict(
                    keep=keep, rows=rows,
                    v=v_ref[rows, h * DV:(h + 1) * DV],
                    q_mid=(q * jnp.exp2(cum - (ref_pt - log2_qscale))).astype(BF16),
                    k_mid=(k * jnp.exp2(ref_pt - cum)).astype(BF16),
                    q_dec=(q * jnp.exp2(cum + log2_qscale)).astype(BF16),
                    k_end_t=(k * jnp.exp2(tot - cum)).T.astype(BF16),
                    dec_col=jnp.broadcast_to(jnp.exp2(tot), (DK, DK)).T)

    def middle(step):
        for di in range(len(dirs)):
            for h in range(HEADS):
                ch = chains[(step, di, h)]
                s = lax.dot_general(ch["q_mid"], ch["k_mid"], (((1,), (1,)), ((), ())),
                                    preferred_element_type=F32)
                ch["scores"] = jnp.where(ch["keep"], s, 0.0).astype(BF16)
                ch["kv"] = jnp.dot(ch["k_end_t"], ch["v"], preferred_element_type=F32)

    state = {(di, h): d[5][h] for di, d in enumerate(dirs) for h in range(HEADS)}

    def tail(step):
        for di, d in enumerate(dirs):
            o_ref = d[6]
            for h in range(HEADS):
                ch = chains.pop((step, di, h))
                s_prev = state[(di, h)]
                lhs = jnp.concatenate([ch["scores"], ch["q_dec"]], axis=1)
                rhs = jnp.concatenate([ch["v"], s_prev.astype(BF16)], axis=0)
                o = jnp.dot(lhs, rhs, preferred_element_type=F32)
                if o_ref is not None:
                    o_ref[ch["rows"], h * DV:(h + 1) * DV] = o.astype(o_ref.dtype)
                dec = jnp.concatenate([ch["dec_col"]] * (DV // DK), axis=1)
                state[(di, h)] = s_prev * dec + ch["kv"]

    front(0)
    for step in range(n_sub):
        if step + 1 < n_sub:
            front(step + 1)
        middle(step)
        tail(step)
    for (di, h), s in state.items():
        dirs[di][5][h] = s


def _rider_specs(weights, n_steps, flat_step):
    specs, shapes = [], []
    for w in weights:
        rows, cols = w.shape
        assert rows % (16 * n_steps) == 0
        specs.append(pl.BlockSpec((rows // n_steps, cols), lambda *idx: (flat_step(*idx), 0)))
        shapes.append(jax.ShapeDtypeStruct(w.shape, BF16))
    return specs, list(specs), shapes


def _cast_riders(in_refs, out_refs):
    for src, dst in zip(in_refs, out_refs):
        dst[...] = src[...].astype(dst.dtype)


def _gla_kernel(*refs, emit_o, n_riders):
    (qkf, vf, lrf, qkb, vb, lrb, wf, bf, wb, bb, s0f, s0b) = refs[:12]
    rider_in, refs = refs[12:12 + n_riders], refs[12 + n_riders:]
    if emit_o:
        of, ob, sf_out, sb_out = refs[:4]
        rider_out = refs[4:4 + n_riders]
    else:
        sf_out, sb_out = refs[:2]
        rider_out = refs[2:2 + n_riders]
        of = ob = None
    s_scr = refs[-1]
    i = pl.program_id(1)

    @pl.when(i == 0)
    def _():
        s_scr[0] = s0f[...]
        s_scr[1] = s0b[...]

    _cast_riders(rider_in, rider_out)

    _gla_chunks([(qkf, vf, lrf, wf, bf, s_scr.at[0], of, False),
                 (qkb, vb, lrb, wb, bb, s_scr.at[1], ob, True)], GLA_CHUNK)

    @pl.when(i == pl.num_programs(1) - 1)
    def _():
        sf_out[...] = s_scr[0]
        sb_out[...] = s_scr[1]


def _gla(p3, lr3, wlr_f, blr_f, wlr_b, blr_b, s0f, s0b, emit_o, riders=()):
    b, t, _ = p3.shape
    per_step = max(s for s in range(1, GLA_CHUNKS_PER_STEP + 1) if t % (GLA_CHUNK * s) == 0)
    c = GLA_CHUNK * per_step
    n = t // c
    fwd = lambda blk: (lambda bi, i: (bi, i, blk))
    bwd = lambda blk: (lambda bi, i: (bi, n - 1 - i, blk))

    def seq_specs(mk):
        return [pl.BlockSpec((None, c, 2 * KEY_W), mk(0)),
                pl.BlockSpec((None, c, VAL_W), mk(1)),
                pl.BlockSpec((None, c, LR_PAD), mk(0))]
    full2 = lambda shape: pl.BlockSpec(shape, lambda bi, i: (0, 0))
    st_spec = pl.BlockSpec((None, HEADS, DK, DV), lambda bi, i: (bi, 0, 0, 0))
    in_specs = (seq_specs(fwd) + seq_specs(bwd)
                + [full2(wlr_f.shape), full2(blr_f.shape), full2(wlr_b.shape), full2(blr_b.shape),
                   st_spec, st_spec])
    st_shape = jax.ShapeDtypeStruct((b, HEADS, DK, DV), F32)
    out_shape = [st_shape, st_shape]
    out_specs = [st_spec, st_spec]
    if emit_o:
        o_shape = jax.ShapeDtypeStruct((b, t, VAL_W), BF16)
        out_shape = [o_shape, o_shape] + out_shape
        out_specs = [pl.BlockSpec((None, c, VAL_W), fwd(0)), pl.BlockSpec((None, c, VAL_W), bwd(0))] + out_specs
    r_in, r_out, r_shapes = _rider_specs(riders, b * n, lambda bi, i: bi * n + i)
    return pl.pallas_call(
        functools.partial(_gla_kernel, emit_o=emit_o, n_riders=len(riders)),
        out_shape=tuple(out_shape + r_shapes),
        grid=(b, n),
        in_specs=in_specs + r_in,
        out_specs=tuple(out_specs + r_out),
        scratch_shapes=[pltpu.VMEM((2, HEADS, DK, DV), F32)],
        compiler_params=_cparams(("arbitrary", "arbitrary")),
        name="gla_seq" if emit_o else "gla_ctx",
    )(p3, p3, lr3, p3, p3, lr3, wlr_f, blr_f, wlr_b, blr_b, s0f, s0b, *riders)


def _cadd(a, b):
    return a[0] + b[0], a[1] + b[1]


def _csub(a, b):
    return a[0] - b[0], a[1] - b[1]


def _cmul_neg_i(a):
    return a[1], -a[0]


def _dft4(y):
    t0, t1 = _cadd(y[0], y[2]), _csub(y[0], y[2])
    t2, t3 = _cadd(y[1], y[3]), _cmul_neg_i(_csub(y[1], y[3]))
    return [_cadd(t0, t2), _cadd(t1, t3), _csub(t0, t2), _csub(t1, t3)]


def _dft8(z):
    r = math.sqrt(0.5)
    s = [_cadd(z[a], z[a + 4]) for a in range(4)]
    d = [_csub(z[a], z[a + 4]) for a in range(4)]
    d1 = ((d[1][0] + d[1][1]) * r, (d[1][1] - d[1][0]) * r)
    d2 = _cmul_neg_i(d[2])
    d3 = ((d[3][1] - d[3][0]) * r, (-d[3][1] - d[3][0]) * r)
    ev = _dft4(s)
    od = _dft4([d[0], d1, d2, d3])
    out = [None] * 8
    for j in range(4):
        out[2 * j] = ev[j]
        out[2 * j + 1] = od[j]
    return out


def _fft_kernel(*refs, n_slabs, n_riders):
    u_refs = refs[:n_slabs]
    f_ref, twc_ref, tws_ref = refs[n_slabs:n_slabs + 3]
    rider_in = refs[n_slabs + 3:n_slabs + 3 + n_riders]
    xr_ref, xi_ref = refs[n_slabs + 3 + n_riders:n_slabs + 5 + n_riders]
    rider_out = refs[n_slabs + 5 + n_riders:n_slabs + 5 + 2 * n_riders]
    wide_scr, ub_scr = refs[-2:]
    m = ub_scr.shape[1]

    @pl.when(pl.program_id(2) == 0)
    def _():
        for s in range(n_slabs):
            wide_scr[s] = u_refs[s][...].astype(F32)
        for a in range(RADIX):
            rows = [wide_scr[s, pl.ds(a, m, stride=RADIX), :].astype(BF16) for s in range(n_slabs)]
            ub_scr[a] = jnp.concatenate(rows, axis=1)

    _cast_riders(rider_in, rider_out)

    t = f_ref.shape[0] // 2
    f = f_ref[...]
    z = []
    for a in range(RADIX):
        za = jnp.dot(f, ub_scr[a], preferred_element_type=F32)
        zr, zi = za[:t], za[t:]
        cc, ss = twc_ref[a], tws_ref[a]
        z.append((zr * cc + zi * ss, zi * cc - zr * ss))
    x = _dft8(z)
    for k1 in range(RADIX):
        xr_ref[k1] = x[k1][0].astype(xr_ref.dtype)
        xi_ref[k1] = x[k1][1].astype(xi_ref.dtype)


def _fft(p3, col0, fmat, twc, tws, riders=()):
    b, seq, _ = p3.shape
    w = FWIDTH
    m = seq // RADIX
    nt, t2, _ = fmat.shape
    t = t2 // 2
    chb = 256
    n_slabs = chb // LANES
    slab0 = col0 // LANES
    out = jax.ShapeDtypeStruct((b, RADIX, m, w), BF16)
    o_spec = pl.BlockSpec((None, RADIX, t, chb), lambda bi, cj, kt: (bi, 0, kt, cj))
    slab = lambda s: pl.BlockSpec((None, seq, LANES), lambda bi, cj, kt: (bi, 0, slab0 + cj * n_slabs + s))
    ncj = w // chb
    n_steps = b * ncj * nt
    flat = lambda bi, cj, kt: (bi * ncj + cj) * nt + kt
    r_in, r_out, r_shapes = _rider_specs(riders, n_steps, flat)
    return pl.pallas_call(
        functools.partial(_fft_kernel, n_slabs=n_slabs, n_riders=len(riders)),
        out_shape=tuple([out, out] + r_shapes),
        grid=(b, ncj, nt),
        in_specs=[slab(s) for s in range(n_slabs)] + [
                  pl.BlockSpec((None, t2, m), lambda bi, cj, kt: (kt, 0, 0)),
                  pl.BlockSpec((RADIX, t, 1), lambda bi, cj, kt: (0, kt, 0)),
                  pl.BlockSpec((RADIX, t, 1), lambda bi, cj, kt: (0, kt, 0))] + r_in,
        out_specs=tuple([o_spec, o_spec] + r_out),
        scratch_shapes=[pltpu.VMEM((n_slabs, seq, LANES), F32), pltpu.VMEM((RADIX, m, chb), BF16)],
        compiler_params=_cparams(("arbitrary", "arbitrary", "arbitrary")),
        name="fft",
    )(*([p3] * n_slabs), fmat, twc, tws, *riders)


def _merge_kernel(of_ref, ob_ref, g_ref, gates_ref, xr_ref, xi_ref, x_ref, rt_ref, ct_ref, gt_ref,
                  gng_ref, wgo_ref, wfo_ref, wo_ref, csg_ref, o_ref):
    o = of_ref[...].astype(F32) + ob_ref[...].astype(F32)
    g = g_ref[...].astype(F32)
    gng = gng_ref[...]
    heads = []
    for h in range(HEADS):
        sl = slice(h * DV, (h + 1) * DV)
        heads.append((_rms(o[:, sl], gng) * _silu(g[:, sl])).astype(BF16))
    y_gla = jnp.dot(jnp.concatenate(heads, axis=1), wgo_ref[...], preferred_element_type=F32)

    csg = csg_ref[...]
    groups = []
    for gi in range(FGROUPS):
        sl = slice(gi * FGDIM, (gi + 1) * FGDIM)
        xg = jnp.concatenate([xr_ref[:, sl], xi_ref[:, sl]], axis=1)
        groups.append(jnp.dot(xg, csg, preferred_element_type=F32).astype(BF16))
    y_fft = jnp.dot(jnp.concatenate(groups, axis=1), wfo_ref[...], preferred_element_type=F32)

    gates = jax.nn.sigmoid(gates_ref[...].astype(F32))
    d = y_fft.shape[1]
    zmix = (gates[:, :d] * y_fft + gates[:, d:] * y_gla).astype(BF16)
    y = jnp.dot(zmix, wo_ref[...], preferred_element_type=F32)
    o_ref[...] = _add_pos(x_ref[...], rt_ref, ct_ref) + gt_ref[...] * y


def _const_spec(shape):
    zeros = (0,) * len(shape)
    return pl.BlockSpec(shape, lambda *idx: zeros, pipeline_mode=pl.Buffered(1))


def _merge(o_f, o_b, p2, xr, xi, x2, rt3, ct, mod3, gng, wgo, wfo, wo, csg, t):
    ntok, d = x2.shape
    tm = 256
    tiles_per_seq = t // tm
    rpt = tm // GRID_W
    row = lambda blk: (lambda i: (i, blk))
    in_specs = [pl.BlockSpec((tm, VAL_W), row(0)),
                pl.BlockSpec((tm, VAL_W), row(0)),
                pl.BlockSpec((tm, VAL_W), row(2)),
                pl.BlockSpec((tm, 2 * d), row(1)),
                pl.BlockSpec((tm, FWIDTH), row(0)),
                pl.BlockSpec((tm, FWIDTH), row(0)),
                pl.BlockSpec((tm, d), row(0)),
                pl.BlockSpec((rpt, 1, d // 2), lambda i: (i % tiles_per_seq, 0, 0)),
                _const_spec((GRID_W, d // 2)),
                pl.BlockSpec((None, 1, d), lambda i: (i // tiles_per_seq, 0, 2 - N_MOD_EARLY)),
                _const_spec(gng.shape), _const_spec(wgo.shape), _const_spec(wfo.shape),
                _const_spec(wo.shape), _const_spec(csg.shape)]
    return pl.pallas_call(
        _merge_kernel,
        out_shape=jax.ShapeDtypeStruct((ntok, d), F32),
        grid=(ntok // tm,),
        in_specs=in_specs,
        out_specs=pl.BlockSpec((tm, d), row(0)),
        compiler_params=_cparams(("arbitrary",)),
        name="merge",
    )(o_f, o_b, p2, p2, xr, xi, x2, rt3, ct, mod3, gng, wgo, wfo, wo, csg)


def _mlp_kernel(x_ref, xn_ref, sh_ref, sc_ref, shn_ref, scn_ref, gt_ref, g2_ref, w1_ref, w2_ref, fg_ref,
                o_ref, h_even, h_odd):
    i, j = pl.program_id(0), pl.program_id(1)
    nf = pl.num_programs(1)
    tm = x_ref.shape[0]
    slab = tm // nf

    def normed(x, sh, sc):
        return (_rms(x, g2_ref[...]) * (1.0 + sc) + sh).astype(BF16)

    h_cur, h_next = h_even, h_odd

    @pl.when((i == 0) & (j == 0))
    def _():
        h_cur[...] = normed(x_ref[...], sh_ref[...], sc_ref[...])

    def step(first, last):
        r0 = pl.multiple_of(j * slab, slab)
        h_next[pl.ds(r0, slab), :] = normed(xn_ref[...], shn_ref[...], scn_ref[...])
        tf = w1_ref.shape[1]
        acc = None if first else o_ref[...]
        for c0 in range(0, tf, MLP_CHUNK):
            hid = jnp.dot(h_cur[...], w1_ref[:, c0:c0 + MLP_CHUNK], preferred_element_type=F32)
            hid = jnp.square(jnp.maximum(hid, 0.0)).astype(BF16)
            part = jnp.dot(hid, w2_ref[c0:c0 + MLP_CHUNK, :], preferred_element_type=F32)
            acc = part if acc is None else acc + part
        if last:
            acc = _rms(x_ref[...] + gt_ref[...] * acc, fg_ref[...])
            h_cur[...] = h_next[...]
        o_ref[...] = acc

    for first, last, cond in ((True, False, j == 0), (False, False, (j > 0) & (j < nf - 1)),
                              (False, True, j == nf - 1)):
        @pl.when(cond)
        def _(first=first, last=last):
            step(first, last)


def _mlp(x2, mod3, g2, w1, w2, fg, t):
    ntok, d = x2.shape
    dff = w1.shape[1]
    tm, tf = 512, 2048
    nf = dff // tf
    slab = tm // nf
    assert slab % 8 == 0 and tf % MLP_CHUNK == 0
    nt = ntok // tm
    tiles_per_seq = t // tm
    nxt = lambda i: jnp.minimum(i + 1, nt - 1)
    modspec = lambda blk: pl.BlockSpec((None, 1, d), lambda i, j: (i // tiles_per_seq, 0, blk))
    modspec_next = lambda blk: pl.BlockSpec((None, 1, d), lambda i, j: (nxt(i) // tiles_per_seq, 0, blk))
    return pl.pallas_call(
        _mlp_kernel,
        out_shape=jax.ShapeDtypeStruct((ntok, d), F32),
        grid=(nt, dff // tf),
        in_specs=[pl.BlockSpec((tm, d), lambda i, j: (jnp.where(j >= 2, i, jnp.maximum(i - 1, 0)), 0)),
                  pl.BlockSpec((slab, d), lambda i, j: (nxt(i) * nf + j, 0)),
                  modspec(3 - N_MOD_EARLY), modspec(4 - N_MOD_EARLY), modspec_next(3 - N_MOD_EARLY),
                  modspec_next(4 - N_MOD_EARLY), modspec(5 - N_MOD_EARLY),
                  pl.BlockSpec((1, d), lambda i, j: (0, 0)),
                  pl.BlockSpec((d, tf), lambda i, j: (0, j)),
                  pl.BlockSpec((tf, d), lambda i, j: (j, 0)),
                  pl.BlockSpec((1, d), lambda i, j: (0, 0))],
        out_specs=pl.BlockSpec((tm, d), lambda i, j: (i, 0)),
        scratch_shapes=[pltpu.VMEM((tm, d), BF16), pltpu.VMEM((tm, d), BF16)],
        compiler_params=_cparams(("arbitrary", "arbitrary"), vmem=BIG_TILE_VMEM_LIMIT),
        name="mlp",
    )(x2, x2, mod3, mod3, mod3, mod3, mod3, g2, w1, w2, fg)


def _pos_tables(t, d):
    quarter = d // 4
    omega = 1.0 / (POS_TEMP ** (jnp.arange(quarter, dtype=F32) / quarter))
    er = jnp.arange(t // GRID_W, dtype=F32)[:, None] * omega[None, :]
    ec = jnp.arange(GRID_W, dtype=F32)[:, None] * omega[None, :]
    rt = jnp.concatenate([jnp.sin(er), jnp.cos(er)], axis=-1)
    ct = jnp.concatenate([jnp.sin(ec), jnp.cos(ec)], axis=-1)
    return rt[:, None, :], ct


def _dft_tables(t):
    m = t // RADIX
    tile = min(256, m)
    k = np.arange(m)
    ang = 2.0 * np.pi * ((k[:, None] * k[None, :]) % m) / m
    cos_t = np.cos(ang).reshape(m // tile, tile, m)
    sin_t = np.sin(ang).reshape(m // tile, tile, m)
    fmat = np.concatenate([cos_t, -sin_t], axis=1)
    a = np.arange(RADIX)
    tw = 2.0 * np.pi * (a[:, None] * k[None, :]) / t
    scale = 1.0 / math.sqrt(t * FGDIM)
    twc = (np.cos(tw) * scale)[:, :, None]
    tws = (np.sin(tw) * scale)[:, :, None]
    c = np.arange(FGDIM)
    cang = 2.0 * np.pi * ((c[:, None] * c[None, :]) % FGDIM) / FGDIM
    csg = np.concatenate([np.cos(cang), np.sin(cang)], axis=0)
    as_f32 = lambda a: jnp.asarray(a.astype(np.float32))
    return as_f32(fmat).astype(BF16), as_f32(twc), as_f32(tws), as_f32(csg).astype(BF16)


def _pad_lr_weight(w_lr, row0):
    out = jnp.zeros((LR_PAD, KEY_W), F32)
    return out.at[row0:row0 + RANK].set(w_lr).astype(BF16)


def kernel(x, c, ctx, c_ctx, w_mod, b_mod, norm1_g, norm2_g, w_in, w_lr_f, b_lr_f, w_lr_b, b_lr_b,
           gla_norm_g, w_fourier_out, w_gla_out, w_out, w_mlp_in, w_mlp_out, final_norm_g):
    b, t, d = x.shape
    tc = ctx.shape[1]
    depth = w_mod.shape[0]
    assert depth == 1 and d == D_MODEL and t % (RADIX * GRID_W) == 0 and tc % GLA_CHUNK == 0
    li = 0

    w_main, w_lr = _wprep(w_in[li].T)
    wlr_f, wlr_b = _pad_lr_weight(w_lr_f[li], 0), _pad_lr_weight(w_lr_b[li], RANK)
    blr_f, blr_b = b_lr_f[li][None, :], b_lr_b[li][None, :]

    rt3, ct = _pos_tables(t, d)
    fmat, twc, tws, csg = _dft_tables(t)

    rows = 8
    cpad_t = jnp.concatenate([c, c_ctx[None, :], jnp.zeros((rows - b - 1, d), F32)], axis=0).T
    n_early = N_MOD_EARLY * d
    mod3 = _mod(cpad_t, b + 1, w_mod[li], b_mod[li][None, :], n_early).reshape(rows, 1, n_early)

    tm_ctx = min(512, b * tc)
    qkv_w = 2 * KEY_W + VAL_W
    p_ctx, lr_ctx = _inproj(ctx.reshape(b * tc, d), mod3, lambda i: b, norm1_g[li][None, :],
                            w_main, w_lr, None, tm_ctx, qkv_w)
    s_zero = jnp.zeros((b, HEADS, DK, DV), F32)
    s_f, s_b = _gla(p_ctx.reshape(b, tc, qkv_w), lr_ctx.reshape(b, tc, LR_PAD),
                    wlr_f, blr_f, wlr_b, blr_b, s_zero, s_zero, emit_o=False)

    tm = min(1024, t)
    tiles = t // tm
    x2 = x.reshape(b * t, d)
    p, lr, mod_late = _inproj(x2, mod3, lambda i: i // tiles, norm1_g[li][None, :], w_main, w_lr,
                              (rt3, ct, tiles), tm, P_WIDTH,
                              mod_job=(cpad_t, b + 1, w_mod[li], b_mod[li][None, :], n_early))
    mod_late3 = mod_late.reshape(rows, 1, N_MOD * d - n_early)
    p3 = p.reshape(b, t, P_WIDTH)
    o_f, o_b, _, _, w1, wgo, wfo, wo = _gla(
        p3, lr.reshape(b, t, LR_PAD), wlr_f, blr_f, wlr_b, blr_b, s_f, s_b, emit_o=True,
        riders=(w_mlp_in[li], w_gla_out[li], w_fourier_out[li], w_out[li]))

    xr, xi, w2 = _fft(p3, QKVG_W, fmat, twc, tws, riders=(w_mlp_out[li],))

    x1 = _merge(o_f.reshape(b * t, VAL_W), o_b.reshape(b * t, VAL_W), p,
                xr.reshape(b * t, FWIDTH), xi.reshape(b * t, FWIDTH), x2, rt3, ct, mod_late3,
                gla_norm_g[li][None, :], wgo, wfo, wo, csg, t)

    out = _mlp(x1, mod_late3, norm2_g[li][None, :], w1, w2, final_norm_g[None, :], t)
    return out.reshape(b, t, d)
```

```python
import functools
import math

import jax
import jax.numpy as jnp
import numpy as np
from jax import lax
from jax.experimental import pallas as pl
from jax.experimental.pallas import tpu as pltpu

F32 = jnp.float32
BF16 = jnp.bfloat16

D_MODEL = 2048
GRID_W = 64
HEADS = 4
DK = 128
DV = 256
KEY_W = HEADS * DK
VAL_W = HEADS * DV
RANK = 16
TAU = 16.0
FGROUPS = 4
FGDIM = 256
FWIDTH = FGROUPS * FGDIM
D_FF = 4 * D_MODEL
N_MOD = 6
N_MOD_EARLY = 2
EPS = 1e-6
POS_TEMP = 10000.0

QKVG_W = 2 * KEY_W + 2 * VAL_W
P_WIDTH = QKVG_W + FWIDTH + 2 * D_MODEL
LR_PAD = 128
GLA_CHUNK = 128
GLA_CHUNKS_PER_STEP = 4
RADIX = 8
LANES = 128

V7X_VMEM_LIMIT = 56 * 1024 * 1024
BIG_TILE_VMEM_LIMIT = 63 * 1024 * 1024
MLP_CHUNK = 2048


def _cparams(sem, vmem=V7X_VMEM_LIMIT):
    return pltpu.CompilerParams(dimension_semantics=sem, vmem_limit_bytes=vmem)


def _silu(x):
    return x * jax.nn.sigmoid(x)


def _mod_kernel(ct_ref, w_ref, b_ref, o_ref, *, n_used):
    st = _silu(ct_ref[...])
    w = w_ref[...]
    bias = b_ref[...]
    rows = [jnp.sum(w * st[:, m:m + 1], axis=0, keepdims=True) + bias for m in range(n_used)]
    rows += [bias] * (o_ref.shape[0] - n_used)
    o_ref[...] = jnp.concatenate(rows, axis=0)


def _mod(cpad_t, n_used, w_mod, b_mod, n):
    d, rows = cpad_t.shape
    tn = 1024
    return pl.pallas_call(
        functools.partial(_mod_kernel, n_used=n_used),
        out_shape=jax.ShapeDtypeStruct((rows, n), F32),
        grid=(n // tn,),
        in_specs=[pl.BlockSpec((d, rows), lambda j: (0, 0)),
                  pl.BlockSpec((d, tn), lambda j: (0, j)),
                  pl.BlockSpec((1, tn), lambda j: (0, j))],
        out_specs=pl.BlockSpec((rows, tn), lambda j: (0, j)),
        compiler_params=_cparams(("arbitrary",)),
        name="mod",
    )(cpad_t, w_mod, b_mod)


def _wprep_kernel(wt_ref, lrt_ref, wm_ref, wlr_ref):
    wm_ref[...] = wt_ref[...].T.astype(BF16)

    @pl.when(pl.program_id(0) == 0)
    def _():
        lrt = lrt_ref[...]
        r = lax.broadcasted_iota(jnp.int32, lrt.shape, 0)
        wlr_ref[...] = jnp.where(r < 2 * RANK, lrt, 0.0).T.astype(BF16)


def _wprep(w_in_t):
    n, d = w_in_t.shape
    tc = 512
    n_before = QKVG_W // tc

    def src_row(j):
        return pl.multiple_of(jnp.where(j >= n_before, j * tc + 2 * RANK, j * tc), 2 * RANK)

    return pl.pallas_call(
        _wprep_kernel,
        out_shape=(jax.ShapeDtypeStruct((d, P_WIDTH), BF16), jax.ShapeDtypeStruct((d, LR_PAD), BF16)),
        grid=(P_WIDTH // tc,),
        in_specs=[pl.BlockSpec((pl.Element(tc), pl.Element(d)), lambda j: (src_row(j), 0)),
                  pl.BlockSpec((LR_PAD, d), lambda j: (QKVG_W // LR_PAD, 0))],
        out_specs=(pl.BlockSpec((d, tc), lambda j: (0, j)), pl.BlockSpec((d, LR_PAD), lambda j: (0, 0))),
        compiler_params=_cparams(("arbitrary",)),
        name="wprep",
    )(w_in_t, w_in_t)


def _add_pos(x, rt_ref, ct_ref):
    tm, d = x.shape
    x3 = x.reshape(tm // GRID_W, GRID_W, d)
    half = d // 2
    lo = x3[:, :, :half] + rt_ref[...]
    hi = x3[:, :, half:] + ct_ref[...][None]
    return jnp.concatenate([lo, hi], axis=-1).reshape(tm, d)


def _rms(x, g):
    return x * lax.rsqrt(jnp.mean(x * x, axis=-1, keepdims=True) + EPS) * g


def _inproj_kernel(*refs, add_pos, n_mod_rows):
    refs = list(refs)
    x_ref = refs.pop(0)
    rt_ref, ct_ref = (refs.pop(0), refs.pop(0)) if add_pos else (None, None)
    sh_ref, sc_ref, g_ref, w_ref, wlr_ref = refs[:5]
    refs = refs[5:]
    mod_in = [refs.pop(0) for _ in range(3)] if n_mod_rows else None
    p_ref, lr_ref = refs.pop(0), refs.pop(0)
    mod_out = refs.pop(0) if n_mod_rows else None
    (h_scr,) = refs

    @pl.when(pl.program_id(1) == 0)
    def _():
        x = x_ref[...]
        if add_pos:
            x = _add_pos(x, rt_ref, ct_ref)
        gain = g_ref[...] * (1.0 + sc_ref[...])
        h = x * lax.rsqrt(jnp.mean(x * x, axis=-1, keepdims=True) + EPS) * gain + sh_ref[...]
        hb = h.astype(BF16)
        h_scr[...] = hb
        lr_ref[...] = jnp.dot(hb, wlr_ref[...], preferred_element_type=F32)

    if n_mod_rows:
        _mod_kernel(*mod_in, mod_out, n_used=n_mod_rows)
    p_ref[...] = jnp.dot(h_scr[...], w_ref[...], preferred_element_type=F32).astype(BF16)


def _inproj(x2, mod3, mod_row_of_tile, norm_g, w_main, w_lr, pos_tabs, tm, width, mod_job=None):
    ntok, d = x2.shape
    tn = 2048
    add_pos = pos_tabs is not None
    in_specs = [pl.BlockSpec((tm, d), lambda i, j: (i, 0))]
    args = [x2]
    if add_pos:
        rt3, ct, tiles_per_seq = pos_tabs
        rpt = tm // GRID_W
        in_specs += [pl.BlockSpec((rpt, 1, d // 2), lambda i, j: (i % tiles_per_seq, 0, 0)),
                     pl.BlockSpec((GRID_W, d // 2), lambda i, j: (0, 0))]
        args += [rt3, ct]
    in_specs += [pl.BlockSpec((None, 1, d), lambda i, j: (mod_row_of_tile(i), 0, 0)),
                 pl.BlockSpec((None, 1, d), lambda i, j: (mod_row_of_tile(i), 0, 1)),
                 pl.BlockSpec((1, d), lambda i, j: (0, 0)),
                 pl.BlockSpec((d, tn), lambda i, j: (0, j)),
                 pl.BlockSpec((d, LR_PAD), lambda i, j: (0, 0))]
    args += [mod3, mod3, norm_g, w_main, w_lr]
    out_shape = [jax.ShapeDtypeStruct((ntok, width), BF16), jax.ShapeDtypeStruct((ntok, LR_PAD), F32)]
    out_specs = [pl.BlockSpec((tm, tn), lambda i, j: (i, j)), pl.BlockSpec((tm, LR_PAD), lambda i, j: (i, 0))]
    ncol = width // tn
    n_mod_rows = 0
    if mod_job is not None:
        cpad_t, n_mod_rows, w_mod, b_mod, col0 = mod_job
        n_steps = (ntok // tm) * ncol
        n_late = w_mod.shape[1] - col0
        cps = n_late // n_steps
        assert n_late % n_steps == 0 and cps % LANES == 0 and col0 % cps == 0
        flat = lambda i, j: i * ncol + j
        in_specs += [pl.BlockSpec(cpad_t.shape, lambda i, j: (0, 0), pipeline_mode=pl.Buffered(1)),
                     pl.BlockSpec((w_mod.shape[0], cps), lambda i, j: (0, col0 // cps + flat(i, j))),
                     pl.BlockSpec((1, cps), lambda i, j: (0, col0 // cps + flat(i, j)))]
        args += [cpad_t, w_mod, b_mod]
        out_shape.append(jax.ShapeDtypeStruct((cpad_t.shape[1], n_late), F32))
        out_specs.append(pl.BlockSpec((cpad_t.shape[1], cps), lambda i, j: (0, flat(i, j))))
    return pl.pallas_call(
        functools.partial(_inproj_kernel, add_pos=add_pos, n_mod_rows=n_mod_rows),
        out_shape=tuple(out_shape),
        grid=(ntok // tm, ncol),
        in_specs=in_specs,
        out_specs=tuple(out_specs),
        scratch_shapes=[pltpu.VMEM((tm, d), BF16)],
        compiler_params=_cparams(("arbitrary", "arbitrary"), vmem=BIG_TILE_VMEM_LIMIT),
        name="inproj_pos" if add_pos else "inproj_ctx",
    )(*args)


def _log_sigmoid(z):
    return jnp.minimum(z, 0.0) - jnp.log(1.0 + jnp.exp(-jnp.abs(z)))


def _gla_chunks(dirs, c):
    n_sub = dirs[0][0].shape[0] // c
    row = lax.broadcasted_iota(jnp.int32, (c, c), 0)
    col = lax.broadcasted_iota(jnp.int32, (c, c), 1)

    chains = {}
    log2_qscale = math.log2(DK ** -0.5)

    def front(step):
        for di, (qk_ref, v_ref, lr_ref, wlr_ref, blr_ref, s_scr, o_ref, backward) in enumerate(dirs):
            keep = (row <= col) if backward else (row >= col)
            tri = jnp.where(keep, 1.0, 0.0).astype(BF16)
            last = 0 if backward else c - 1
            mid = c // 2 if backward else c // 2 - 1
            sub = n_sub - 1 - step if backward else step
            rows = slice(sub * c, (sub + 1) * c)
            z = jnp.dot(lr_ref[rows, :].astype(BF16), wlr_ref[...], preferred_element_type=F32) + blr_ref[...]
            la = _log_sigmoid(z) * (math.log2(math.e) / TAU)
            la_hi = la.astype(BF16)
            la_lo = (la - la_hi.astype(F32)).astype(BF16)
            cum2 = jnp.dot(tri, jnp.concatenate([la_hi, la_lo], axis=1), preferred_element_type=F32)
            cum_all = cum2[:, :KEY_W] + cum2[:, KEY_W:]
            for h in range(HEADS):
                cum = cum_all[:, h * DK:(h + 1) * DK]
                tot = cum[last:last + 1, :]
                ref_pt = cum[mid:mid + 1, :]
                q = qk_ref[rows, h * DK:(h + 1) * DK].astype(F32)
                k = qk_ref[rows, KEY_W + h * DK:KEY_W + (h + 1) * DK].astype(F32)
                chains[(step, di, h)] = dict(
                    keep=keep, rows=rows,
                    v=v_ref[rows, h * DV:(h + 1) * DV],
                    q_mid=(q * jnp.exp2(cum - (ref_pt - log2_qscale))).astype(BF16),
                    k_mid=(k * jnp.exp2(ref_pt - cum)).astype(BF16),
                    q_dec=(q * jnp.exp2(cum + log2_qscale)).astype(BF16),
                    k_end_t=(k * jnp.exp2(tot - cum)).T.astype(BF16),
                    dec_col=jnp.broadcast_to(jnp.exp2(tot), (DK, DK)).T)

    def middle(step):
        for di in range(len(dirs)):
            for h in range(HEADS):
                ch = chains[(step, di, h)]
                s = lax.dot_general(ch["q_mid"], ch["k_mid"], (((1,), (1,)), ((), ())),
                                    preferred_element_type=F32)
                ch["scores"] = jnp.where(ch["keep"], s, 0.0).astype(BF16)
                ch["kv"] = jnp.dot(ch["k_end_t"], ch["v"], preferred_element_type=F32)

    state = {(di, h): d[5][h] for di, d in enumerate(dirs) for h in range(HEADS)}

    def tail(step):
        for di, d in enumerate(dirs):
            o_ref = d[6]
            for h in range(HEADS):
                ch = chains.pop((step, di, h))
                s_prev = state[(di, h)]
                lhs = jnp.concatenate([ch["scores"], ch["q_dec"]], axis=1)
                rhs = jnp.concatenate([ch["v"], s_prev.astype(BF16)], axis=0)
                o = jnp.dot(lhs, rhs, preferred_element_type=F32)
                if o_ref is not None:
                    o_ref[ch["rows"], h * DV:(h + 1) * DV] = o.astype(o_ref.dtype)
                dec = jnp.concatenate([ch["dec_col"]] * (DV // DK), axis=1)
                state[(di, h)] = s_prev * dec + ch["kv"]

    front(0)
    for step in range(n_sub):
        if step + 1 < n_sub:
            front(step + 1)
        middle(step)
        tail(step)
    for (di, h), s in state.items():
        dirs[di][5][h] = s


def _rider_specs(weights, n_steps, flat_step):
    specs, shapes = [], []
    for w in weights:
        rows, cols = w.shape
        assert rows % (16 * n_steps) == 0
        specs.append(pl.BlockSpec((rows // n_steps, cols), lambda *idx: (flat_step(*idx), 0)))
        shapes.append(jax.ShapeDtypeStruct(w.shape, BF16))
    return specs, list(specs), shapes


def _cast_riders(in_refs, out_refs):
    for src, dst in zip(in_refs, out_refs):
        dst[...] = src[...].astype(dst.dtype)


def _gla_kernel(*refs, emit_o, n_riders):
    (qkf, vf, lrf, qkb, vb, lrb, wf, bf, wb, bb, s0f, s0b) = refs[:12]
    rider_in, refs = refs[12:12 + n_riders], refs[12 + n_riders:]
    if emit_o:
        of, ob, sf_out, sb_out = refs[:4]
        rider_out = refs[4:4 + n_riders]
    else:
        sf_out, sb_out = refs[:2]
        rider_out = refs[2:2 + n_riders]
        of = ob = None
    s_scr = refs[-1]
    i = pl.program_id(1)

    @pl.when(i == 0)
    def _():
        s_scr[0] = s0f[...]
        s_scr[1] = s0b[...]

    _cast_riders(rider_in, rider_out)

    _gla_chunks([(qkf, vf, lrf, wf, bf, s_scr.at[0], of, False),
                 (qkb, vb, lrb, wb, bb, s_scr.at[1], ob, True)], GLA_CHUNK)

    @pl.when(i == pl.num_programs(1) - 1)
    def _():
        sf_out[...] = s_scr[0]
        sb_out[...] = s_scr[1]


def _gla(p3, lr3, wlr_f, blr_f, wlr_b, blr_b, s0f, s0b, emit_o, riders=()):
    b, t, _ = p3.shape
    per_step = max(s for s in range(1, GLA_CHUNKS_PER_STEP + 1) if t % (GLA_CHUNK * s) == 0)
    c = GLA_CHUNK * per_step
    n = t // c
    fwd = lambda blk: (lambda bi, i: (bi, i, blk))
    bwd = lambda blk: (lambda bi, i: (bi, n - 1 - i, blk))

    def seq_specs(mk):
        return [pl.BlockSpec((None, c, 2 * KEY_W), mk(0)),
                pl.BlockSpec((None, c, VAL_W), mk(1)),
                pl.BlockSpec((None, c, LR_PAD), mk(0))]
    full2 = lambda shape: pl.BlockSpec(shape, lambda bi, i: (0, 0))
    st_spec = pl.BlockSpec((None, HEADS, DK, DV), lambda bi, i: (bi, 0, 0, 0))
    in_specs = (seq_specs(fwd) + seq_specs(bwd)
                + [full2(wlr_f.shape), full2(blr_f.shape), full2(wlr_b.shape), full2(blr_b.shape),
                   st_spec, st_spec])
    st_shape = jax.ShapeDtypeStruct((b, HEADS, DK, DV), F32)
    out_shape = [st_shape, st_shape]
    out_specs = [st_spec, st_spec]
    if emit_o:
        o_shape = jax.ShapeDtypeStruct((b, t, VAL_W), BF16)
        out_shape = [o_shape, o_shape] + out_shape
        out_specs = [pl.BlockSpec((None, c, VAL_W), fwd(0)), pl.BlockSpec((None, c, VAL_W), bwd(0))] + out_specs
    r_in, r_out, r_shapes = _rider_specs(riders, b * n, lambda bi, i: bi * n + i)
    return pl.pallas_call(
        functools.partial(_gla_kernel, emit_o=emit_o, n_riders=len(riders)),
        out_shape=tuple(out_shape + r_shapes),
        grid=(b, n),
        in_specs=in_specs + r_in,
        out_specs=tuple(out_specs + r_out),
        scratch_shapes=[pltpu.VMEM((2, HEADS, DK, DV), F32)],
        compiler_params=_cparams(("arbitrary", "arbitrary")),
        name="gla_seq" if emit_o else "gla_ctx",
    )(p3, p3, lr3, p3, p3, lr3, wlr_f, blr_f, wlr_b, blr_b, s0f, s0b, *riders)


def _cadd(a, b):
    return a[0] + b[0], a[1] + b[1]


def _csub(a, b):
    return a[0] - b[0], a[1] - b[1]


def _cmul_neg_i(a):
    return a[1], -a[0]


def _dft4(y):
    t0, t1 = _cadd(y[0], y[2]), _csub(y[0], y[2])
    t2, t3 = _cadd(y[1], y[3]), _cmul_neg_i(_csub(y[1], y[3]))
    return [_cadd(t0, t2), _cadd(t1, t3), _csub(t0, t2), _csub(t1, t3)]


def _dft8(z):
    r = math.sqrt(0.5)
    s = [_cadd(z[a], z[a + 4]) for a in range(4)]
    d = [_csub(z[a], z[a + 4]) for a in range(4)]
    d1 = ((d[1][0] + d[1][1]) * r, (d[1][1] - d[1][0]) * r)
    d2 = _cmul_neg_i(d[2])
    d3 = ((d[3][1] - d[3][0]) * r, (-d[3][1] - d[3][0]) * r)
    ev = _dft4(s)
    od = _dft4([d[0], d1, d2, d3])
    out = [None] * 8
    for j in range(4):
        out[2 * j] = ev[j]
        out[2 * j + 1] = od[j]
    return out


def _fft_kernel(*refs, n_slabs, n_riders):
    u_refs = refs[:n_slabs]
    f_ref, twc_ref, tws_ref = refs[n_slabs:n_slabs + 3]
    rider_in = refs[n_slabs + 3:n_slabs + 3 + n_riders]
    xr_ref, xi_ref = refs[n_slabs + 3 + n_riders:n_slabs + 5 + n_riders]
    rider_out = refs[n_slabs + 5 + n_riders:n_slabs + 5 + 2 * n_riders]
    wide_scr, ub_scr = refs[-2:]
    m = ub_scr.shape[1]

    @pl.when(pl.program_id(2) == 0)
    def _():
        for s in range(n_slabs):
            wide_scr[s] = u_refs[s][...].astype(F32)
        for a in range(RADIX):
            rows = [wide_scr[s, pl.ds(a, m, stride=RADIX), :].astype(BF16) for s in range(n_slabs)]
            ub_scr[a] = jnp.concatenate(rows, axis=1)

    _cast_riders(rider_in, rider_out)

    t = f_ref.shape[0] // 2
    f = f_ref[...]
    z = []
    for a in range(RADIX):
        za = jnp.dot(f, ub_scr[a], preferred_element_type=F32)
        zr, zi = za[:t], za[t:]
        cc, ss = twc_ref[a], tws_ref[a]
        z.append((zr * cc + zi * ss, zi * cc - zr * ss))
    x = _dft8(z)
    for k1 in range(RADIX):
        xr_ref[k1] = x[k1][0].astype(xr_ref.dtype)
        xi_ref[k1] = x[k1][1].astype(xi_ref.dtype)


def _fft(p3, col0, fmat, twc, tws, riders=()):
    b, seq, _ = p3.shape
    w = FWIDTH
    m = seq // RADIX
    nt, t2, _ = fmat.shape
    t = t2 // 2
    chb = 256
    n_slabs = chb // LANES
    slab0 = col0 // LANES
    out = jax.ShapeDtypeStruct((b, RADIX, m, w), BF16)
    o_spec = pl.BlockSpec((None, RADIX, t, chb), lambda bi, cj, kt: (bi, 0, kt, cj))
    slab = lambda s: pl.BlockSpec((None, seq, LANES), lambda bi, cj, kt: (bi, 0, slab0 + cj * n_slabs + s))
    ncj = w // chb
    n_steps = b * ncj * nt
    flat = lambda bi, cj, kt: (bi * ncj + cj) * nt + kt
    r_in, r_out, r_shapes = _rider_specs(riders, n_steps, flat)
    return pl.pallas_call(
        functools.partial(_fft_kernel, n_slabs=n_slabs, n_riders=len(riders)),
        out_shape=tuple([out, out] + r_shapes),
        grid=(b, ncj, nt),
        in_specs=[slab(s) for s in range(n_slabs)] + [
                  pl.BlockSpec((None, t2, m), lambda bi, cj, kt: (kt, 0, 0)),
                  pl.BlockSpec((RADIX, t, 1), lambda bi, cj, kt: (0, kt, 0)),
                  pl.BlockSpec((RADIX, t, 1), lambda bi, cj, kt: (0, kt, 0))] + r_in,
        out_specs=tuple([o_spec, o_spec] + r_out),
        scratch_shapes=[pltpu.VMEM((n_slabs, seq, LANES), F32), pltpu.VMEM((RADIX, m, chb), BF16)],
        compiler_params=_cparams(("arbitrary", "arbitrary", "arbitrary")),
        name="fft",
    )(*([p3] * n_slabs), fmat, twc, tws, *riders)


def _merge_kernel(of_ref, ob_ref, g_ref, gates_ref, xr_ref, xi_ref, x_ref, rt_ref, ct_ref, gt_ref,
                  gng_ref, wgo_ref, wfo_ref, wo_ref, csg_ref, o_ref):
    o = of_ref[...].astype(F32) + ob_ref[...].astype(F32)
    g = g_ref[...].astype(F32)
    gng = gng_ref[...]
    heads = []
    for h in range(HEADS):
        sl = slice(h * DV, (h + 1) * DV)
        heads.append((_rms(o[:, sl], gng) * _silu(g[:, sl])).astype(BF16))
    y_gla = jnp.dot(jnp.concatenate(heads, axis=1), wgo_ref[...], preferred_element_type=F32)

    csg = csg_ref[...]
    groups = []
    for gi in range(FGROUPS):
        sl = slice(gi * FGDIM, (gi + 1) * FGDIM)
        xg = jnp.concatenate([xr_ref[:, sl], xi_ref[:, sl]], axis=1)
        groups.append(jnp.dot(xg, csg, preferred_element_type=F32).astype(BF16))
    y_fft = jnp.dot(jnp.concatenate(groups, axis=1), wfo_ref[...], preferred_element_type=F32)

    gates = jax.nn.sigmoid(gates_ref[...].astype(F32))
    d = y_fft.shape[1]
    zmix = (gates[:, :d] * y_fft + gates[:, d:] * y_gla).astype(BF16)
    y = jnp.dot(zmix, wo_ref[...], preferred_element_type=F32)
    o_ref[...] = _add_pos(x_ref[...], rt_ref, ct_ref) + gt_ref[...] * y


def _const_spec(shape):
    zeros = (0,) * len(shape)
    return pl.BlockSpec(shape, lambda *idx: zeros, pipeline_mode=pl.Buffered(1))


def _merge(o_f, o_b, p2, xr, xi, x2, rt3, ct, mod3, gng, wgo, wfo, wo, csg, t):
    ntok, d = x2.shape
    tm = 256
    tiles_per_seq = t // tm
    rpt = tm // GRID_W
    row = lambda blk: (lambda i: (i, blk))
    in_specs = [pl.BlockSpec((tm, VAL_W), row(0)),
                pl.BlockSpec((tm, VAL_W), row(0)),
                pl.BlockSpec((tm, VAL_W), row(2)),
                pl.BlockSpec((tm, 2 * d), row(1)),
                pl.BlockSpec((tm, FWIDTH), row(0)),
                pl.BlockSpec((tm, FWIDTH), row(0)),
                pl.BlockSpec((tm, d), row(0)),
                pl.BlockSpec((rpt, 1, d // 2), lambda i: (i % tiles_per_seq, 0, 0)),
                _const_spec((GRID_W, d // 2)),
                pl.BlockSpec((None, 1, d), lambda i: (i // tiles_per_seq, 0, 2 - N_MOD_EARLY)),
                _const_spec(gng.shape), _const_spec(wgo.shape), _const_spec(wfo.shape),
                _const_spec(wo.shape), _const_spec(csg.shape)]
    return pl.pallas_call(
        _merge_kernel,
        out_shape=jax.ShapeDtypeStruct((ntok, d), F32),
        grid=(ntok // tm,),
        in_specs=in_specs,
        out_specs=pl.BlockSpec((tm, d), row(0)),
        compiler_params=_cparams(("arbitrary",)),
        name="merge",
    )(o_f, o_b, p2, p2, xr, xi, x2, rt3, ct, mod3, gng, wgo, wfo, wo, csg)


def _mlp_kernel(x_ref, xn_ref, sh_ref, sc_ref, shn_ref, scn_ref, gt_ref, g2_ref, w1_ref, w2_ref, fg_ref,
                o_ref, h_even, h_odd):
    i, j = pl.program_id(0), pl.program_id(1)
    nf = pl.num_programs(1)
    tm = x_ref.shape[0]
    slab = tm // nf

    def normed(x, sh, sc):
        return (_rms(x, g2_ref[...]) * (1.0 + sc) + sh).astype(BF16)

    h_cur, h_next = h_even, h_odd

    @pl.when((i == 0) & (j == 0))
    def _():
        h_cur[...] = normed(x_ref[...], sh_ref[...], sc_ref[...])

    def step(first, last):
        r0 = pl.multiple_of(j * slab, slab)
        h_next[pl.ds(r0, slab), :] = normed(xn_ref[...], shn_ref[...], scn_ref[...])
        tf = w1_ref.shape[1]
        acc = None if first else o_ref[...]
        for c0 in range(0, tf, MLP_CHUNK):
            hid = jnp.dot(h_cur[...], w1_ref[:, c0:c0 + MLP_CHUNK], preferred_element_type=F32)
            hid = jnp.square(jnp.maximum(hid, 0.0)).astype(BF16)
            part = jnp.dot(hid, w2_ref[c0:c0 + MLP_CHUNK, :], preferred_element_type=F32)
            acc = part if acc is None else acc + part
        if last:
            acc = _rms(x_ref[...] + gt_ref[...] * acc, fg_ref[...])
            h_cur[...] = h_next[...]
        o_ref[...] = acc

    for first, last, cond in ((True, False, j == 0), (False, False, (j > 0) & (j < nf - 1)),
                              (False, True, j == nf - 1)):
        @pl.when(cond)
        def _(first=first, last=last):
            step(first, last)


def _mlp(x2, mod3, g2, w1, w2, fg, t):
    ntok, d = x2.shape
    dff = w1.shape[1]
    tm, tf = 512, 2048
    nf = dff // tf
    slab = tm // nf
    assert slab % 8 == 0 and tf % MLP_CHUNK == 0
    nt = ntok // tm
    tiles_per_seq = t // tm
    nxt = lambda i: jnp.minimum(i + 1, nt - 1)
    modspec = lambda blk: pl.BlockSpec((None, 1, d), lambda i, j: (i // tiles_per_seq, 0, blk))
    modspec_next = lambda blk: pl.BlockSpec((None, 1, d), lambda i, j: (nxt(i) // tiles_per_seq, 0, blk))
    return pl.pallas_call(
        _mlp_kernel,
        out_shape=jax.ShapeDtypeStruct((ntok, d), F32),
        grid=(nt, dff // tf),
        in_specs=[pl.BlockSpec((tm, d), lambda i, j: (jnp.where(j >= 2, i, jnp.maximum(i - 1, 0)), 0)),
                  pl.BlockSpec((slab, d), lambda i, j: (nxt(i) * nf + j, 0)),
                  modspec(3 - N_MOD_EARLY), modspec(4 - N_MOD_EARLY), modspec_next(3 - N_MOD_EARLY),
                  modspec_next(4 - N_MOD_EARLY), modspec(5 - N_MOD_EARLY),
                  pl.BlockSpec((1, d), lambda i, j: (0, 0)),
                  pl.BlockSpec((d, tf), lambda i, j: (0, j)),
                  pl.BlockSpec((tf, d), lambda i, j: (j, 0)),
                  pl.BlockSpec((1, d), lambda i, j: (0, 0))],
        out_specs=pl.BlockSpec((tm, d), lambda i, j: (i, 0)),
        scratch_shapes=[pltpu.VMEM((tm, d), BF16), pltpu.VMEM((tm, d), BF16)],
        compiler_params=_cparams(("arbitrary", "arbitrary"), vmem=BIG_TILE_VMEM_LIMIT),
        name="mlp",
    )(x2, x2, mod3, mod3, mod3, mod3, mod3, g2, w1, w2, fg)


def _pos_tables(t, d):
    quarter = d // 4
    omega = 1.0 / (POS_TEMP ** (jnp.arange(quarter, dtype=F32) / quarter))
    er = jnp.arange(t // GRID_W, dtype=F32)[:, None] * omega[None, :]
    ec = jnp.arange(GRID_W, dtype=F32)[:, None] * omega[None, :]
    rt = jnp.concatenate([jnp.sin(er), jnp.cos(er)], axis=-1)
    ct = jnp.concatenate([jnp.sin(ec), jnp.cos(ec)], axis=-1)
    return rt[:, None, :], ct


def _dft_tables(t):
    m = t // RADIX
    tile = min(256, m)
    k = np.arange(m)
    ang = 2.0 * np.pi * ((k[:, None] * k[None, :]) % m) / m
    cos_t = np.cos(ang).reshape(m // tile, tile, m)
    sin_t = np.sin(ang).reshape(m // tile, tile, m)
    fmat = np.concatenate([cos_t, -sin_t], axis=1)
    a = np.arange(RADIX)
    tw = 2.0 * np.pi * (a[:, None] * k[None, :]) / t
    scale = 1.0 / math.sqrt(t * FGDIM)
    twc = (np.cos(tw) * scale)[:, :, None]
    tws = (np.sin(tw) * scale)[:, :, None]
    c = np.arange(FGDIM)
    cang = 2.0 * np.pi * ((c[:, None] * c[None, :]) % FGDIM) / FGDIM
    csg = np.concatenate([np.cos(cang), np.sin(cang)], axis=0)
    as_f32 = lambda a: jnp.asarray(a.astype(np.float32))
    return as_f32(fmat).astype(BF16), as_f32(twc), as_f32(tws), as_f32(csg).astype(BF16)


def _pad_lr_weight(w_lr, row0):
    out = jnp.zeros((LR_PAD, KEY_W), F32)
    return out.at[row0:row0 + RANK].set(w_lr).astype(BF16)


def kernel(x, c, ctx, c_ctx, w_mod, b_mod, norm1_g, norm2_g, w_in, w_lr_f, b_lr_f, w_lr_b, b_lr_b,
           gla_norm_g, w_fourier_out, w_gla_out, w_out, w_mlp_in, w_mlp_out, final_norm_g):
    b, t, d = x.shape
    tc = ctx.shape[1]
    depth = w_mod.shape[0]
    assert depth == 1 and d == D_MODEL and t % (RADIX * GRID_W) == 0 and tc % GLA_CHUNK == 0
    li = 0

    w_main, w_lr = _wprep(w_in[li].T)
    wlr_f, wlr_b = _pad_lr_weight(w_lr_f[li], 0), _pad_lr_weight(w_lr_b[li], RANK)
    blr_f, blr_b = b_lr_f[li][None, :], b_lr_b[li][None, :]

    rt3, ct = _pos_tables(t, d)
    fmat, twc, tws, csg = _dft_tables(t)

    rows = 8
    cpad_t = jnp.concatenate([c, c_ctx[None, :], jnp.zeros((rows - b - 1, d), F32)], axis=0).T
    n_early = N_MOD_EARLY * d
    mod3 = _mod(cpad_t, b + 1, w_mod[li], b_mod[li][None, :], n_early).reshape(rows, 1, n_early)

    tm_ctx = min(512, b * tc)
    qkv_w = 2 * KEY_W + VAL_W
    p_ctx, lr_ctx = _inproj(ctx.reshape(b * tc, d), mod3, lambda i: b, norm1_g[li][None, :],
                            w_main, w_lr, None, tm_ctx, qkv_w)
    s_zero = jnp.zeros((b, HEADS, DK, DV), F32)
    s_f, s_b = _gla(p_ctx.reshape(b, tc, qkv_w), lr_ctx.reshape(b, tc, LR_PAD),
                    wlr_f, blr_f, wlr_b, blr_b, s_zero, s_zero, emit_o=False)

    tm = min(1024, t)
    tiles = t // tm
    x2 = x.reshape(b * t, d)
    p, lr, mod_late = _inproj(x2, mod3, lambda i: i // tiles, norm1_g[li][None, :], w_main, w_lr,
                              (rt3, ct, tiles), tm, P_WIDTH,
                              mod_job=(cpad_t, b + 1, w_mod[li], b_mod[li][None, :], n_early))
    mod_late3 = mod_late.reshape(rows, 1, N_MOD * d - n_early)
    p3 = p.reshape(b, t, P_WIDTH)
    o_f, o_b, _, _, w1, wgo, wfo, wo = _gla(
        p3, lr.reshape(b, t, LR_PAD), wlr_f, blr_f, wlr_b, blr_b, s_f, s_b, emit_o=True,
        riders=(w_mlp_in[li], w_gla_out[li], w_fourier_out[li], w_out[li]))

    xr, xi, w2 = _fft(p3, QKVG_W, fmat, twc, tws, riders=(w_mlp_out[li],))

    x1 = _merge(o_f.reshape(b * t, VAL_W), o_b.reshape(b * t, VAL_W), p,
                xr.reshape(b * t, FWIDTH), xi.reshape(b * t, FWIDTH), x2, rt3, ct, mod_late3,
                gla_norm_g[li][None, :], wgo, wfo, wo, csg, t)

    out = _mlp(x1, mod_late3, norm2_g[li][None, :], w1, w2, final_norm_g[None, :], t)
    return out.reshape(b, t, d)
```
